```python
import jax, jax.numpy as jnp
from jax import lax
import numpy as np

D_MODEL = 1024
BATCH = 2
SEQ = 8192
DEPTH = 2

GRID_W = 64
CTX_LEN = 256
HEAD_DIM = D_MODEL // 16
N_ATT_HEADS = 6
N_ATT_KV = 2
N_NA_HEADS = 4
N_RET_HEADS = 6
ATT_Q = N_ATT_HEADS * HEAD_DIM
ATT_KV = N_ATT_KV * HEAD_DIM
NA_W = N_NA_HEADS * HEAD_DIM
RET_W = N_RET_HEADS * HEAD_DIM
MIX_W = ATT_Q + NA_W + RET_W
IN_COLS = ATT_Q + 2 * ATT_KV + 3 * NA_W + 4 * RET_W
Q_BLOCK = 128
NA_WIN_ROWS = 8
NA_WIN_COLS = 16
RET_CHUNK = 128
ROPE_THETA = 10000.0
N_EXPERTS = 16
N_EXPERT_GROUPS = 4
TOP_GROUPS = 1
TOP_K = 2
D_EXPERT = 512
EPS = 1e-6
NEG_INF = -1e30

kernel_name = 'hybrid_diffusion_trunk'


def _rms(x, g):
    xf = x.astype(jnp.float32)
    y = xf * lax.rsqrt(jnp.mean(xf * xf, axis=-1, keepdims=True) + EPS)
    return (y * g.astype(jnp.float32)).astype(x.dtype)


def _heads(t, n_heads):
    b, n, _ = t.shape
    return t.reshape(b, n, n_heads, HEAD_DIM).transpose(0, 2, 1, 3)


def _flat(o):
    b, h, n, d = o.shape
    return o.transpose(0, 2, 1, 3).reshape(b, n, h * d)


def _head_norm(y, g):
    b, n, w = y.shape
    hh = w // HEAD_DIM
    return _rms(y.reshape(b, n, hh, HEAD_DIM), g.reshape(hh, HEAD_DIM)).reshape(b, n, w)


def _axial_angles(n):
    t = jnp.arange(n, dtype=jnp.int32)
    row = (t // GRID_W).astype(jnp.float32)
    col = (t % GRID_W).astype(jnp.float32)
    nf = HEAD_DIM // 4
    inv = ROPE_THETA ** (-jnp.arange(nf, dtype=jnp.float32) / nf)
    return row[:, None] * inv[None, :], col[:, None] * inv[None, :]


def _rope_1d(t, ang):
    nf = ang.shape[-1]
    cos = jnp.cos(ang).astype(t.dtype)
    sin = jnp.sin(ang).astype(t.dtype)
    t1, t2 = t[..., :nf], t[..., nf:]
    return jnp.concatenate([t1 * cos - t2 * sin, t2 * cos + t1 * sin], axis=-1)


def _rope_2d(t, ang_r, ang_c):
    half = HEAD_DIM // 2
    return jnp.concatenate([_rope_1d(t[..., :half], ang_r), _rope_1d(t[..., half:], ang_c)], axis=-1)


def _project(h, w_in):
    z = h @ w_in
    widths = (ATT_Q, ATT_KV, ATT_KV, NA_W, NA_W, NA_W, RET_W, RET_W, RET_W)
    counts = (N_ATT_HEADS, N_ATT_KV, N_ATT_KV, N_NA_HEADS, N_NA_HEADS, N_NA_HEADS,
              N_RET_HEADS, N_RET_HEADS, N_RET_HEADS)
    cuts = [int(v) for v in np.cumsum(widths)]
    parts = jnp.split(z, cuts, axis=-1)
    return tuple(_heads(p, hc) for p, hc in zip(parts[:-1], counts)) + (parts[-1],)


def _gqa(q, k, v, qc, kc, vc, q_g, k_g, ang_r, ang_c, with_ctx_out):
    b, hq, n, d = q.shape
    grp = hq // N_ATT_KV
    scale = d ** -0.5
    q = _rope_2d(_rms(q, q_g), ang_r, ang_c)
    k = _rope_2d(_rms(k, k_g), ang_r, ang_c)
    kc = _rms(kc, k_g)
    k_all = jnp.concatenate([k, kc], axis=2)
    v_all = jnp.concatenate([v, vc], axis=2)
    nblk = n // Q_BLOCK
    qb = q.reshape(b, N_ATT_KV, grp, nblk, Q_BLOCK, d).transpose(3, 0, 1, 2, 4, 5)

    def block(qi):
        s = jnp.einsum('bhgqd,bhkd->bhgqk', qi, k_all).astype(jnp.float32) * scale
        p = jax.nn.softmax(s, axis=-1).astype(v_all.dtype)
        return jnp.einsum('bhgqk,bhkd->bhgqd', p, v_all)

    o = lax.map(block, qb).transpose(1, 2, 3, 0, 4, 5).reshape(b, hq, n, d)
    if not with_ctx_out:
        return o, None
    cl = qc.shape[2]
    qcn = _rms(qc, q_g).reshape(b, N_ATT_KV, grp, cl, d)
    s = jnp.einsum('bhgqd,bhkd->bhgqk', qcn, kc).astype(jnp.float32) * scale
    p = jax.nn.softmax(s, axis=-1).astype(vc.dtype)
    oc = jnp.einsum('bhgqk,bhkd->bhgqd', p, vc).reshape(b, hq, cl, d)
    return o, oc


def _neigh_attn(q, k, v, qc, kc, vc, rpb, with_ctx_out):
    b, h, n, d = q.shape
    rows = n // GRID_W
    wr = min(NA_WIN_ROWS, rows)
    wc = NA_WIN_COLS
    nkeys = wr * GRID_W
    scale = d ** -0.5
    r = jnp.arange(rows)
    ridx = jnp.clip(r - wr // 2, 0, rows - wr)[:, None] + jnp.arange(wr)[None, :]
    col = jnp.arange(GRID_W)
    cstart = jnp.clip(col - wc // 2, 0, GRID_W - wc)
    col_ok = (col[None, :] >= cstart[:, None]) & (col[None, :] < cstart[:, None] + wc)
    mask = jnp.tile(col_ok, (1, wr))
    dr = ridx - r[:, None] + (NA_WIN_ROWS - 1)
    dc = jnp.clip(col[None, :] - col[:, None] + (wc - 1), 0, 2 * wc - 2)
    bias = rpb[:, dr][..., dc].transpose(0, 1, 3, 2, 4).reshape(h, rows, GRID_W, nkeys)

    def grid(t):
        return t.reshape(b, h, rows, GRID_W, d)

    kg = jnp.take(grid(k), ridx, axis=2).reshape(b, h, rows, nkeys, d)
    vg = jnp.take(grid(v), ridx, axis=2).reshape(b, h, rows, nkeys, d)
    qr = grid(q) * scale
    s_nb = jnp.einsum('bhrqd,bhrkd->bhrqk', qr, kg).astype(jnp.float32) + bias.astype(jnp.float32)[None]
    s_nb = jnp.where(mask[None, None, None], s_nb, NEG_INF)
    s_cx = jnp.einsum('bhrqd,bhkd->bhrqk', qr, kc).astype(jnp.float32)
    p = jax.nn.softmax(jnp.concatenate([s_nb, s_cx], axis=-1), axis=-1).astype(v.dtype)
    o = (jnp.einsum('bhrqk,bhrkd->bhrqd', p[..., :nkeys], vg)
         + jnp.einsum('bhrqk,bhkd->bhrqd', p[..., nkeys:], vc)).reshape(b, h, n, d)
    if not with_ctx_out:
        return o, None
    sc = jnp.einsum('bhqd,bhkd->bhqk', qc * scale, kc).astype(jnp.float32)
    pc = jax.nn.softmax(sc, axis=-1).astype(vc.dtype)
    return o, jnp.einsum('bhqk,bhkd->bhqd', pc, vc)


def _ret_chunk_states(k, v, log_g, s0):
    b, h, L, d = k.shape
    nc = L // RET_CHUNK
    kc = k.reshape(b, h, nc, RET_CHUNK, d)
    vc = v.reshape(b, h, nc, RET_CHUNK, d)
    pos = jnp.arange(RET_CHUNK, dtype=jnp.float32)
    zeta = jnp.exp(log_g[:, None] * (RET_CHUNK - 1.0 - pos)[None, :])
    u = jnp.einsum('bhnjd,bhnje->nbhde', kc * zeta[None, :, None, :, None], vc)
    decay = jnp.exp(log_g * RET_CHUNK)[None, :, None, None]

    def step(s, u_n):
        return decay * s + u_n, s

    s_fin, s_before = lax.scan(step, s0, u)
    return s_before, s_fin


def _ret_chunk_out(q, k, v, log_g, s_before):
    b, h, L, d = q.shape
    nc = L // RET_CHUNK
    qc = q.reshape(b, h, nc, RET_CHUNK, d)
    kc = k.reshape(b, h, nc, RET_CHUNK, d)
    vc = v.reshape(b, h, nc, RET_CHUNK, d)
    pos = jnp.arange(RET_CHUNK, dtype=jnp.float32)
    diff = pos[:, None] - pos[None, :]
    dmat = jnp.where(diff >= 0, jnp.exp(log_g[:, None, None] * jnp.maximum(diff, 0.0)[None]), 0.0)
    scores = jnp.einsum('bhnid,bhnjd->bhnij', qc, kc) * dmat[None, :, None]
    inner = jnp.einsum('bhnij,bhnjd->bhnid', scores, vc)
    xi = jnp.exp(log_g[:, None] * (pos + 1.0)[None, :])
    cross = jnp.einsum('bhnid,nbhde->bhnie', qc * xi[None, :, None, :, None], s_before)
    return (inner + cross).reshape(b, h, L, d)


def _retention(q, k, v, qc, kc, vc, log_g2, with_ctx_out):
    dt = q.dtype
    ks = HEAD_DIM ** -0.5
    f32 = lambda t: t.astype(jnp.float32)
    ql, kl, vl = f32(q), f32(k) * ks, f32(v)
    qx, kx, vx = f32(qc), f32(kc) * ks, f32(vc)
    b, h, _, d = ql.shape
    s0 = jnp.zeros((b, h, d, d), jnp.float32)

    def one_direction(ql, kl, vl, qx, kx, vx, lg):
        sb_c, sf_c = _ret_chunk_states(kx, vx, lg, s0)
        ol = _ret_chunk_out(ql, kl, vl, lg, _ret_chunk_states(kl, vl, lg, sf_c)[0])
        oc = _ret_chunk_out(qx, kx, vx, lg, sb_c) if with_ctx_out else None
        return ol, oc

    flip = lambda t: t[:, :, ::-1]
    ol_f, oc_f = one_direction(ql, kl, vl, qx, kx, vx, log_g2[0])
    ol_b, oc_b = one_direction(*map(flip, (ql, kl, vl, qx, kx, vx)), log_g2[1])
    out_lat = (ol_f + flip(ol_b)).astype(dt)
    out_ctx = (oc_f + flip(oc_b)).astype(dt) if with_ctx_out else None
    return out_lat, out_ctx


def _merge(o_a, o_n, o_r, g_r, mix_g, w_out):
    ga, gn, gr = jnp.split(mix_g, [ATT_Q, ATT_Q + NA_W])
    y = jnp.concatenate([_head_norm(_flat(o_a), ga),
                         _head_norm(_flat(o_n), gn),
                         _head_norm(_flat(o_r), gr) * jax.nn.silu(g_r)], axis=-1)
    return y @ w_out


def _moe(h, w_router, b_router, w1, w3, w2):
    per = N_EXPERTS // N_EXPERT_GROUPS
    logits = jnp.einsum('btd,de->bte', h, w_router).astype(jnp.float32) + b_router.astype(jnp.float32)
    probs = jax.nn.softmax(logits, axis=-1)
    pg = probs.reshape(probs.shape[:-1] + (N_EXPERT_GROUPS, per))
    gscore = lax.top_k(pg, TOP_K)[0].sum(-1)
    gsel = jax.nn.one_hot(lax.top_k(gscore, TOP_GROUPS)[1], N_EXPERT_GROUPS, dtype=jnp.float32).sum(-2)
    cand = (pg * gsel[..., None]).reshape(probs.shape)
    w, idx = lax.top_k(cand, TOP_K)
    w = w / w.sum(-1, keepdims=True)
    gates = (jax.nn.one_hot(idx, N_EXPERTS, dtype=jnp.float32) * w[..., None]).sum(-2).astype(h.dtype)
    out = jnp.zeros_like(h)
    for e in range(N_EXPERTS):
        act = jax.nn.silu(h @ w1[e]) * (h @ w3[e])
        out = out + gates[..., e:e + 1] * (act @ w2[e])
    return out


def _layer(x, xc, mod, mod_c, n1_g, n2_g, w_in, q_g, k_g, rpb, log_g2, mix_g, w_out,
           w_router, b_router, w1, w3, w2, ang_r, ang_c, last):
    sh1, sc1, gt1, sh2, sc2, gt2 = jnp.split(mod[:, None, :], 6, axis=-1)
    csh1, csc1, cgt1, csh2, csc2, cgt2 = jnp.split(mod_c, 6, axis=-1)
    want_ctx = not last
    h = _rms(x, n1_g) * (1 + sc1) + sh1
    hc = _rms(xc, n1_g) * (1 + csc1) + csh1
    aq, ak, av, nq, nk, nv, rq, rk, rv, rg = _project(h, w_in)
    caq, cak, cav, cnq, cnk, cnv, crq, crk, crv, crg = _project(hc, w_in)
    o_a, oc_a = _gqa(aq, ak, av, caq, cak, cav, q_g, k_g, ang_r, ang_c, want_ctx)
    o_n, oc_n = _neigh_attn(nq, nk, nv, cnq, cnk, cnv, rpb, want_ctx)
    o_r, oc_r = _retention(rq, rk, rv, crq, crk, crv, log_g2, want_ctx)
    x = x + gt1 * _merge(o_a, o_n, o_r, rg, mix_g, w_out)
    h2 = _rms(x, n2_g) * (1 + sc2) + sh2
    if last:
        return x + gt2 * _moe(h2, w_router, b_router, w1, w3, w2), None
    xc = xc + cgt1 * _merge(oc_a, oc_n, oc_r, crg, mix_g, w_out)
    h2c = _rms(xc, n2_g) * (1 + csc2) + csh2
    n = x.shape[1]
    m = _moe(jnp.concatenate([h2, h2c], axis=1), w_router, b_router, w1, w3, w2)
    return x + gt2 * m[:, :n], xc + cgt2 * m[:, n:]


def setup_inputs(seed: int = 0) -> dict:
    key = jax.random.key(seed)
    ks = jax.random.split(key, 21)
    nrm = jax.random.normal
    f = jnp.float32
    sd = D_MODEL ** -0.5
    decay_init = jnp.log(2.0 ** (5.0 + jnp.arange(N_RET_HEADS, dtype=f)) - 1.0)
    return {
        'x': nrm(ks[0], (BATCH, SEQ, D_MODEL), f),
        'c': nrm(ks[1], (BATCH, D_MODEL), f),
        'ctx': nrm(ks[2], (BATCH, CTX_LEN, D_MODEL), f),
        'c_ctx': nrm(ks[3], (D_MODEL,), f),
        'w_ada': nrm(ks[4], (DEPTH, D_MODEL, 6 * D_MODEL), f) * (0.5 * sd),
        'b_ada': 0.02 * nrm(ks[5], (DEPTH, 6 * D_MODEL), f),
        'norm1_g': 1.0 + 0.02 * nrm(ks[6], (DEPTH, D_MODEL), f),
        'norm2_g': 1.0 + 0.02 * nrm(ks[7], (DEPTH, D_MODEL), f),
        'w_in': nrm(ks[8], (DEPTH, D_MODEL, IN_COLS), f) * sd,
        'q_norm_g': 1.0 + 0.02 * nrm(ks[9], (DEPTH, HEAD_DIM), f),
        'k_norm_g': 1.0 + 0.02 * nrm(ks[10], (DEPTH, HEAD_DIM), f),
        'na_rpb': 0.1 * nrm(ks[11], (DEPTH, N_NA_HEADS, 2 * NA_WIN_ROWS - 1, 2 * NA_WIN_COLS - 1), f),
        'ret_decay': decay_init[None, None, :] + 0.05 * nrm(ks[12], (DEPTH, 2, N_RET_HEADS), f),
        'mix_g': 1.0 + 0.02 * nrm(ks[13], (DEPTH, MIX_W), f),
        'w_out': nrm(ks[14], (DEPTH, MIX_W, D_MODEL), f) * MIX_W ** -0.5,
        'w_router': nrm(ks[15], (D_MODEL, N_EXPERTS), f) * sd,
        'b_router': 0.01 * nrm(ks[16], (N_EXPERTS,), f),
        'w_exp1': nrm(ks[17], (DEPTH, N_EXPERTS, D_MODEL, D_EXPERT), f) * sd,
        'w_exp3': nrm(ks[18], (DEPTH, N_EXPERTS, D_MODEL, D_EXPERT), f) * sd,
        'w_exp2': nrm(ks[19], (DEPTH, N_EXPERTS, D_EXPERT, D_MODEL), f) * D_EXPERT ** -0.5,
        'final_g': 1.0 + 0.02 * nrm(ks[20], (D_MODEL,), f),
    }


def reference(x, c, ctx, c_ctx, w_ada, b_ada, norm1_g, norm2_g, w_in, q_norm_g, k_norm_g,
              na_rpb, ret_decay, mix_g, w_out, w_router, b_router, w_exp1, w_exp3, w_exp2, final_g):
    ang_r, ang_c = _axial_angles(x.shape[1])
    sc = jax.nn.silu(c)
    scc = jax.nn.silu(c_ctx)
    xc = ctx
    for l in range(DEPTH):
        mod = sc @ w_ada[l] + b_ada[l]
        mod_c = scc @ w_ada[l] + b_ada[l]
        log_g2 = jax.nn.log_sigmoid(ret_decay[l].astype(jnp.float32))
        x, xc = _layer(x, xc, mod, mod_c, norm1_g[l], norm2_g[l], w_in[l], q_norm_g[l], k_norm_g[l],
                       na_rpb[l], log_g2, mix_g[l], w_out[l], w_router, b_router,
                       w_exp1[l], w_exp3[l], w_exp2[l], ang_r, ang_c, l == DEPTH - 1)
    return _rms(x, final_g)
```

```python
import functools

import numpy as np
import jax
import jax.numpy as jnp
from jax import lax
from jax.experimental import pallas as pl
from jax.experimental.pallas import tpu as pltpu

D_MODEL = 1024
BATCH = 2
SEQ = 8192
DEPTH = 2
GRID_W = 64
GRID_ROWS = SEQ // GRID_W
CTX_LEN = 256
HEAD_DIM = 64
N_ATT_HEADS = 6
N_ATT_KV = 2
ATT_GRP = N_ATT_HEADS // N_ATT_KV
N_NA_HEADS = 4
N_RET_HEADS = 6
ATT_Q = N_ATT_HEADS * HEAD_DIM
ATT_KV = N_ATT_KV * HEAD_DIM
NA_W = N_NA_HEADS * HEAD_DIM
RET_W = N_RET_HEADS * HEAD_DIM
NA_WIN_ROWS = 8
NA_WIN_COLS = 16
RET_CHUNK = 128
ROPE_THETA = 10000.0
N_EXPERTS = 16
N_EXPERT_GROUPS = 4
EXPERTS_PER_GROUP = N_EXPERTS // N_EXPERT_GROUPS
D_EXPERT = 512
EPS = 1e-6
NEG_INF = -1e30

LANES = 128
VMEM_LIMIT_CAP = 56 * 1024 * 1024

T_LAT = BATCH * SEQ
T_CTX = BATCH * CTX_LEN
T_ALL = T_LAT + T_CTX
TM = 256
LAT_TILES_PER_BATCH = SEQ // TM
LAT_TILES = T_LAT // TM
CTX_TILES = T_CTX // TM
ALL_TILES = LAT_TILES + CTX_TILES
TM_MOE = 512
HB = LANES

F32 = jnp.float32
BF16 = jnp.bfloat16

_SECTIONS = (
    ("aq", 0, N_ATT_HEADS, "pad"),
    ("ak", ATT_Q, N_ATT_KV, "pad"),
    ("av", ATT_Q + ATT_KV, N_ATT_KV, "pad"),
    ("nq", ATT_Q + 2 * ATT_KV, N_NA_HEADS, "pad"),
    ("nk", ATT_Q + 2 * ATT_KV + NA_W, N_NA_HEADS, "pad"),
    ("nv", ATT_Q + 2 * ATT_KV + 2 * NA_W, N_NA_HEADS, "pad"),
    ("rq", ATT_Q + 2 * ATT_KV + 3 * NA_W, N_RET_HEADS, "dup"),
    ("rk", ATT_Q + 2 * ATT_KV + 3 * NA_W + RET_W, N_RET_HEADS, "dup"),
    ("rv", ATT_Q + 2 * ATT_KV + 3 * NA_W + 2 * RET_W, N_RET_HEADS, "pad"),
    ("rg", ATT_Q + 2 * ATT_KV + 3 * NA_W + 3 * RET_W, N_RET_HEADS, "pad"),
)
_SEC_OFF = {}
_off = 0
for _name, _src, _heads, _mode in _SECTIONS:
    _SEC_OFF[_name] = (_off, _heads * HB)
    _off += _heads * HB
NC_PAD = _off
MIX_PAD = (N_ATT_HEADS + N_NA_HEADS + N_RET_HEADS) * HB


def _vmem_limit(nbytes):
    return int(min(VMEM_LIMIT_CAP, max(16 * 1024 * 1024, 2 * nbytes)))


def _proj_column_map():
    cols = np.zeros((NC_PAD,), np.int32)
    valid = np.zeros((NC_PAD,), bool)
    for name, src, heads, mode in _SECTIONS:
        off, _ = _SEC_OFF[name]
        for h in range(heads):
            base = off + h * HB
            srcs = src + h * HEAD_DIM + np.arange(HEAD_DIM)
            cols[base:base + HEAD_DIM] = srcs
            valid[base:base + HEAD_DIM] = True
            if mode == "dup":
                cols[base + HEAD_DIM:base + HB] = srcs
                valid[base + HEAD_DIM:base + HB] = True
    return cols, valid


def _pad_heads(v, heads):
    v = v.reshape(heads, 1, HEAD_DIM).astype(F32)
    return jnp.concatenate([v, jnp.zeros_like(v)], axis=-1)


def _tile_mod_row(i):
    return jnp.where(i < LAT_TILES_PER_BATCH, 0, jnp.where(i < LAT_TILES, 1, 2))


ADA_TN = 1536


def _ada_kernel(c_ref, w_ref, b_ref, o_ref):
    c = c_ref[...]
    s = c * jax.nn.sigmoid(c)
    o_ref[0] = jnp.dot(s, w_ref[0], preferred_element_type=F32,
                       precision=lax.Precision.HIGHEST) + b_ref[0]


def _ada_mod(cvec, w_ada, b_ada):
    n = 6 * D_MODEL
    return pl.pallas_call(
        _ada_kernel,
        out_shape=jax.ShapeDtypeStruct((DEPTH, 8, n), F32),
        grid=(DEPTH, n // ADA_TN),
        in_specs=[
            pl.BlockSpec((8, D_MODEL), lambda l, j: (0, 0)),
            pl.BlockSpec((1, D_MODEL, ADA_TN), lambda l, j: (l, 0, j)),
            pl.BlockSpec((1, 1, ADA_TN), lambda l, j: (l, 0, j)),
        ],
        out_specs=pl.BlockSpec((1, 8, ADA_TN), lambda l, j: (l, 0, j)),
        compiler_params=pltpu.CompilerParams(
            dimension_semantics=("arbitrary", "arbitrary"),
            vmem_limit_bytes=_vmem_limit(2 * D_MODEL * ADA_TN * 4)),
        name="ada_mod",
    )(cvec, w_ada, b_ada.reshape(DEPTH, 1, n))


def _rope_swap(t):
    lane = lax.broadcasted_iota(jnp.int32, t.shape, 1)
    first_half = (lane % 32) < 16
    return jnp.where(first_half, pltpu.roll(t, LANES - 16, 1), pltpu.roll(t, 16, 1))


def _inproj_kernel(x_ref, mod_ref, g1_ref, w_ref, cs_ref, sn_ref, qg_ref, kg_ref,
                   aq_ref, ak_ref, av_ref, nq_ref, nk_ref, nv_ref,
                   rq_ref, rk_ref, rv_ref, rg_ref):
    x = x_ref[...]
    mod = mod_ref[0]
    sh1 = mod[:, 0:D_MODEL]
    sc1 = mod[:, D_MODEL:2 * D_MODEL]
    ms = jnp.mean(x * x, axis=-1, keepdims=True)
    h = x * lax.rsqrt(ms + EPS) * g1_ref[...]
    h = (h * (1.0 + sc1) + sh1).astype(BF16)
    cs = cs_ref[...]
    sn = sn_ref[...]

    def proj(name, hidx):
        off, _ = _SEC_OFF[name]
        c0 = off + hidx * HB
        return jnp.dot(h, w_ref[:, c0:c0 + HB], preferred_element_type=F32)

    def normed_rope(z, g):
        ss = jnp.sum(z * z, axis=-1, keepdims=True)
        zn = z * lax.rsqrt(ss * (1.0 / HEAD_DIM) + EPS) * g
        return zn * cs + _rope_swap(zn) * sn

    scale = HEAD_DIM ** -0.5
    for hh in range(N_ATT_HEADS):
        z = normed_rope(proj("aq", hh), qg_ref[...]) * scale
        aq_ref[:, hh * HB:(hh + 1) * HB] = z.astype(BF16)
    for hh in range(N_ATT_KV):
        z = normed_rope(proj("ak", hh), kg_ref[...])
        ak_ref[:, hh * HB:(hh + 1) * HB] = z.astype(BF16)
        av_ref[:, hh * HB:(hh + 1) * HB] = proj("av", hh).astype(BF16)
    for hh in range(N_NA_HEADS):
        nq_ref[:, hh * HB:(hh + 1) * HB] = (proj("nq", hh) * scale).astype(BF16)
        nk_ref[:, hh * HB:(hh + 1) * HB] = proj("nk", hh).astype(BF16)
        nv_ref[:, hh * HB:(hh + 1) * HB] = proj("nv", hh).astype(BF16)
    for hh in range(N_RET_HEADS):
        rq_ref[:, hh * HB:(hh + 1) * HB] = proj("rq", hh).astype(BF16)
        rk_ref[:, hh * HB:(hh + 1) * HB] = (proj("rk", hh) * scale).astype(BF16)
        rv_ref[:, hh * HB:(hh + 1) * HB] = proj("rv", hh).astype(BF16)
        rg_ref[:, hh * HB:(hh + 1) * HB] = proj("rg", hh).astype(BF16)


def _inproj(x_all, mod3, g1, w_pad, cs_tab, sn_tab, qg, kg):
    names = [s[0] for s in _SECTIONS]
    widths = [_SEC_OFF[n][1] for n in names]

    def tab_map(i):
        return (jnp.where(i < LAT_TILES, i % LAT_TILES_PER_BATCH, LAT_TILES_PER_BATCH), 0)

    est = (2 * D_MODEL * NC_PAD * 2 + 2 * TM * D_MODEL * 4 + 2 * TM * NC_PAD * 2
           + 4 * TM * D_MODEL * 4)
    return pl.pallas_call(
        _inproj_kernel,
        out_shape=[jax.ShapeDtypeStruct((T_ALL, w), BF16) for w in widths],
        grid=(ALL_TILES,),
        in_specs=[
            pl.BlockSpec((TM, D_MODEL), lambda i: (i, 0)),
            pl.BlockSpec((1, 1, 6 * D_MODEL), lambda i: (_tile_mod_row(i), 0, 0)),
            pl.BlockSpec((1, D_MODEL), lambda i: (0, 0)),
            pl.BlockSpec((D_MODEL, NC_PAD), lambda i: (0, 0)),
            pl.BlockSpec((TM, HB), tab_map),
            pl.BlockSpec((TM, HB), tab_map),
            pl.BlockSpec((1, HB), lambda i: (0, 0)),
            pl.BlockSpec((1, HB), lambda i: (0, 0)),
        ],
        out_specs=[pl.BlockSpec((TM, w), lambda i: (i, 0)) for w in widths],
        compiler_params=pltpu.CompilerParams(
            dimension_semantics=("arbitrary",), vmem_limit_bytes=_vmem_limit(est)),
        name="norm_inproj",
    )(x_all, mod3, g1, w_pad, cs_tab, sn_tab, qg, kg)


ATT_TK = 1024
ATT_NK = SEQ // ATT_TK


def _attn_kernel(q_ref, k_ref, v_ref, kc_ref, vc_ref, g_ref, o_ref, m_sc, l_sc, acc_sc):
    i = pl.program_id(2)
    j = pl.program_id(3)
    is_ctx_q = i >= LAT_TILES_PER_BATCH

    @pl.when(j == 0)
    def _():
        m_sc[...] = jnp.full(m_sc.shape, -jnp.inf, F32)
        l_sc[...] = jnp.zeros(l_sc.shape, F32)
        acc_sc[...] = jnp.zeros(acc_sc.shape, F32)

    def step(k, v):
        for hh in range(ATT_GRP):
            q = q_ref[:, hh * HB:(hh + 1) * HB]
            s = lax.dot_general(q, k, (((1,), (1,)), ((), ())), preferred_element_type=F32)
            m_prev = m_sc[hh]
            m_new = jnp.maximum(m_prev, jnp.max(s, axis=-1, keepdims=True))
            alpha = jnp.exp(m_prev - m_new)
            p = jnp.exp(s - m_new)
            l_sc[hh] = alpha * l_sc[hh] + jnp.sum(p, axis=-1, keepdims=True)
            acc_sc[hh] = alpha * acc_sc[hh] + jnp.dot(p.astype(BF16), v,
                                                      preferred_element_type=F32)
            m_sc[hh] = m_new

    @pl.when(jnp.logical_not(is_ctx_q))
    def _():
        step(k_ref[...], v_ref[...])

    @pl.when(j == ATT_NK - 1)
    def _():
        step(kc_ref[...], vc_ref[...])
        for hh in range(ATT_GRP):
            o = acc_sc[hh] / l_sc[hh]
            ms = jnp.sum(o * o, axis=-1, keepdims=True) * (1.0 / HEAD_DIM)
            y = o * lax.rsqrt(ms + EPS) * g_ref[hh]
            o_ref[:, hh * HB:(hh + 1) * HB] = y.astype(BF16)


def _gqa(aq, ak, av, ga, want_ctx):
    nq = LAT_TILES_PER_BATCH + (1 if want_ctx else 0)

    def q_map(b, c, i, j):
        return (jnp.where(i < LAT_TILES_PER_BATCH, b * LAT_TILES_PER_BATCH + i, LAT_TILES + b), c)

    def k_map(b, c, i, j):
        return (b * ATT_NK + j, c)

    def kc_map(b, c, i, j):
        return (LAT_TILES + b, c)

    est = (2 * TM * ATT_GRP * HB * 2 * 2 + 4 * ATT_TK * HB * 2 + 4 * TM * HB * 2
           + ATT_GRP * TM * HB * 4 * 3 + 6 * TM * ATT_TK * 4)
    return pl.pallas_call(
        _attn_kernel,
        out_shape=jax.ShapeDtypeStruct((T_ALL, N_ATT_HEADS * HB), BF16),
        grid=(BATCH, N_ATT_KV, nq, ATT_NK),
        in_specs=[
            pl.BlockSpec((TM, ATT_GRP * HB), q_map),
            pl.BlockSpec((ATT_TK, HB), k_map),
            pl.BlockSpec((ATT_TK, HB), k_map),
            pl.BlockSpec((CTX_LEN, HB), kc_map),
            pl.BlockSpec((CTX_LEN, HB), kc_map),
            pl.BlockSpec((ATT_GRP, 1, HB), lambda b, c, i, j: (c, 0, 0)),
        ],
        out_specs=pl.BlockSpec((TM, ATT_GRP * HB), q_map),
        scratch_shapes=[
            pltpu.VMEM((ATT_GRP, TM, 1), F32),
            pltpu.VMEM((ATT_GRP, TM, 1), F32),
            pltpu.VMEM((ATT_GRP, TM, HB), F32),
        ],
        compiler_params=pltpu.CompilerParams(
            dimension_semantics=("arbitrary",) * 4, vmem_limit_bytes=_vmem_limit(est)),
        name="gqa_attn",
    )(aq, ak, av, ak, av, ga)


NA_BAND = NA_WIN_ROWS * GRID_W
NA_CLASSES = 8
_NA_CLASS_ROWS = (0, 1, 2, 3, GRID_ROWS // 2, GRID_ROWS - 3, GRID_ROWS - 2, GRID_ROWS - 1)


def _na_bias_table(rpb):
    wr, wc = NA_WIN_ROWS, NA_WIN_COLS
    r = np.asarray(_NA_CLASS_ROWS)
    ridx = np.clip(r - wr // 2, 0, GRID_ROWS - wr)[:, None] + np.arange(wr)[None, :]
    dr = ridx - r[:, None] + (wr - 1)
    col = np.arange(GRID_W)
    cstart = np.clip(col - wc // 2, 0, GRID_W - wc)
    col_ok = (col[None, :] >= cstart[:, None]) & (col[None, :] < cstart[:, None] + wc)
    dc = np.clip(col[None, :] - col[:, None] + (wc - 1), 0, 2 * wc - 2)
    bias = rpb[:, dr][..., dc]
    bias = bias.transpose(0, 1, 3, 2, 4).reshape(N_NA_HEADS, NA_CLASSES, GRID_W, NA_BAND)
    mask = np.tile(col_ok, (1, wr))
    return jnp.where(mask[None, None], bias.astype(F32), NEG_INF)


def _head_rms_gain(o, g):
    ms = jnp.sum(o * o, axis=-1, keepdims=True) * (1.0 / HEAD_DIM)
    return o * lax.rsqrt(ms + EPS) * g


def _na_kernel(q_ref, k_ref, v_ref, kc_ref, vc_ref, qc_ref, bias_ref, g_ref, o_ref, oc_ref,
               *, want_ctx):
    kc = kc_ref[...]
    vc = vc_ref[...]
    g = g_ref[0]
    half = NA_WIN_ROWS // 2
    last = GRID_ROWS - NA_WIN_ROWS

    def row(r, carry):
        start = jnp.clip(r - half, 0, last)
        cls = jnp.where(r < half, r, jnp.where(r > last + half, r - last, half))
        q = q_ref[pl.ds(pl.multiple_of(r * GRID_W, GRID_W), GRID_W), :]
        kb = k_ref[pl.ds(pl.multiple_of(start * GRID_W, GRID_W), NA_BAND), :]
        vb = v_ref[pl.ds(pl.multiple_of(start * GRID_W, GRID_W), NA_BAND), :]
        s = lax.dot_general(q, kb, (((1,), (1,)), ((), ())), preferred_element_type=F32)
        bt = bias_ref[0, cls]
        s = jnp.where(bt > 0.5 * NEG_INF, s + bt, NEG_INF)
        sc = lax.dot_general(q, kc, (((1,), (1,)), ((), ())), preferred_element_type=F32)
        m = jnp.maximum(jnp.max(s, axis=-1, keepdims=True), jnp.max(sc, axis=-1, keepdims=True))
        p = jnp.exp(s - m)
        pc = jnp.exp(sc - m)
        l = jnp.sum(p, axis=-1, keepdims=True) + jnp.sum(pc, axis=-1, keepdims=True)
        o = (jnp.dot(p.astype(BF16), vb, preferred_element_type=F32)
             + jnp.dot(pc.astype(BF16), vc, preferred_element_type=F32)) / l
        o_ref[pl.ds(pl.multiple_of(r * GRID_W, GRID_W), GRID_W), :] = (
            _head_rms_gain(o, g).astype(BF16))
        return carry

    lax.fori_loop(0, GRID_ROWS, row, 0, unroll=2)

    if want_ctx:
        sc = lax.dot_general(qc_ref[...], kc, (((1,), (1,)), ((), ())),
                             preferred_element_type=F32)
        m = jnp.max(sc, axis=-1, keepdims=True)
        pc = jnp.exp(sc - m)
        l = jnp.sum(pc, axis=-1, keepdims=True)
        o = jnp.dot(pc.astype(BF16), vc, preferred_element_type=F32) / l
        oc_ref[...] = _head_rms_gain(o, g).astype(BF16)
    else:
        oc_ref[...] = jnp.zeros(oc_ref.shape, oc_ref.dtype)


def _neigh(nq, nk, nv, bias_tab, gn, want_ctx):
    lat = pl.BlockSpec((SEQ, HB), lambda b, h: (b, h))
    ctx = pl.BlockSpec((CTX_LEN, HB), lambda b, h: (T_LAT // CTX_LEN + b, h))
    est = 2 * (4 * SEQ * HB * 2 + 4 * CTX_LEN * HB * 2 + NA_CLASSES * GRID_W * NA_BAND * 4)
    return pl.pallas_call(
        functools.partial(_na_kernel, want_ctx=want_ctx),
        out_shape=[jax.ShapeDtypeStruct((T_LAT, N_NA_HEADS * HB), BF16),
                   jax.ShapeDtypeStruct((T_CTX, N_NA_HEADS * HB), BF16)],
        grid=(BATCH, N_NA_HEADS),
        in_specs=[lat, lat, lat, ctx, ctx, ctx,
                  pl.BlockSpec((1, NA_CLASSES, GRID_W, NA_BAND), lambda b, h: (h, 0, 0, 0)),
                  pl.BlockSpec((1, 1, HB), lambda b, h: (h, 0, 0))],
        out_specs=[pl.BlockSpec((SEQ, HB), lambda b, h: (b, h)),
                   pl.BlockSpec((CTX_LEN, HB), lambda b, h: (b, h))],
        compiler_params=pltpu.CompilerParams(
            dimension_semantics=("arbitrary", "arbitrary"), vmem_limit_bytes=_vmem_limit(est)),
        name="neigh_attn",
    )(nq, nk, nv, nk, nv, nq, bias_tab, gn)


RET_NCHUNK = SEQ // RET_CHUNK
RET_NCHUNK_CTX = CTX_LEN // RET_CHUNK


def _ret_tables(log_g2):
    lf = log_g2[0][:, None, None]
    lb = log_g2[1][:, None, None]
    pos = jnp.arange(RET_CHUNK, dtype=F32)
    i = pos[None, :, None]
    j = pos[None, None, :]
    diff = i - j
    dm = jnp.where(diff > 0, jnp.exp(lf * jnp.maximum(diff, 0.0)),
                   jnp.where(diff < 0, jnp.exp(lb * jnp.maximum(-diff, 0.0)), 2.0)) * 0.5
    fwd_lane = (jnp.arange(LANES) < HEAD_DIM)[None, None, :]
    xi = jnp.where(fwd_lane, jnp.exp(lf * (i + 1.0)), jnp.exp(lb * (RET_CHUNK - i)))
    zt = jnp.where(fwd_lane, jnp.exp(lf * (RET_CHUNK - 1.0 - i)), jnp.exp(lb * i))
    fwd_row = (jnp.arange(LANES) < HEAD_DIM)[None, :, None]
    dec = jnp.where(fwd_row, jnp.exp(lf * RET_CHUNK), jnp.exp(lb * RET_CHUNK))
    dec = jnp.broadcast_to(dec, (N_RET_HEADS, LANES, LANES))
    return dm.astype(F32), xi.astype(F32), zt.astype(F32), dec.astype(F32)


def _ret_kernel(q_ref, k_ref, v_ref, gt_ref, qc_ref, kc_ref, vc_ref, gtc_ref,
                dm_ref, xi_ref, zt_ref, dec_ref, g_ref, o_ref, oc_ref,
                u_sc, s_sc, uc_sc, sc_sc, *, want_ctx):
    dm = dm_ref[0]
    xi = xi_ref[0]
    zt = zt_ref[0]
    dec = dec_ref[0]
    dec_f = dec[0:HEAD_DIM]
    dec_b = dec[HEAD_DIM:LANES]
    g = g_ref[0]
    C = RET_CHUNK

    def chunk_state_update(kr, vr, usc, n):
        rows = pl.ds(pl.multiple_of(n * C, C), C)
        kz = (kr[rows, :].astype(F32) * zt).T.astype(BF16)
        usc[n] = jnp.dot(kz, vr[rows, :], preferred_element_type=F32)

    def chunk_out(qr, kr, vr, gtr, ssc, outr, n):
        rows = pl.ds(pl.multiple_of(n * C, C), C)
        qd = qr[rows, :]
        s2 = lax.dot_general(qd, kr[rows, :], (((1,), (1,)), ((), ())),
                             preferred_element_type=F32)
        inner = jnp.dot((s2 * dm).astype(BF16), vr[rows, :], preferred_element_type=F32)
        qx = (qd.astype(F32) * xi).astype(BF16)
        cross = jnp.dot(qx, ssc[n].astype(BF16), preferred_element_type=F32)
        y = _head_rms_gain(inner + cross, g)
        gate = gtr[rows, :].astype(F32)
        outr[rows, :] = (y * (gate * jax.nn.sigmoid(gate))).astype(BF16)

    def scan_states(usc, ssc, nchunk, init_f, init_b):
        def fwd(n, sf):
            ssc[n, 0:HEAD_DIM, :] = sf
            return dec_f * sf + usc[n, 0:HEAD_DIM, :]

        def bwd(t, sb):
            n = nchunk - 1 - t
            ssc[n, HEAD_DIM:LANES, :] = sb
            return dec_b * sb + usc[n, HEAD_DIM:LANES, :]

        return (lax.fori_loop(0, nchunk, fwd, init_f), lax.fori_loop(0, nchunk, bwd, init_b))

    zero = jnp.zeros((HEAD_DIM, LANES), F32)
    for n in range(RET_NCHUNK_CTX):
        chunk_state_update(kc_ref, vc_ref, uc_sc, n)
    ctx_f, ctx_b = scan_states(uc_sc, sc_sc, RET_NCHUNK_CTX, zero, zero)
    if want_ctx:
        for n in range(RET_NCHUNK_CTX):
            chunk_out(qc_ref, kc_ref, vc_ref, gtc_ref, sc_sc, oc_ref, n)
    else:
        oc_ref[...] = jnp.zeros(oc_ref.shape, oc_ref.dtype)

    def upd(n, carry):
        chunk_state_update(k_ref, v_ref, u_sc, n)
        return carry

    lax.fori_loop(0, RET_NCHUNK, upd, 0, unroll=2)
    scan_states(u_sc, s_sc, RET_NCHUNK, ctx_f, ctx_b)

    def out(n, carry):
        chunk_out(q_ref, k_ref, v_ref, gt_ref, s_sc, o_ref, n)
        return carry

    lax.fori_loop(0, RET_NCHUNK, out, 0, unroll=2)


def _retention(rq, rk, rv, rg, tables, gr, want_ctx):
    lat = pl.BlockSpec((SEQ, HB), lambda b, h: (b, h))
    ctx = pl.BlockSpec((CTX_LEN, HB), lambda b, h: (T_LAT // CTX_LEN + b, h))
    tab = pl.BlockSpec((1, LANES, LANES), lambda b, h: (h, 0, 0))
    est = (2 * 5 * SEQ * HB * 2 + 2 * RET_NCHUNK * LANES * LANES * 4 + 8 * LANES * LANES * 4)
    return pl.pallas_call(
        functools.partial(_ret_kernel, want_ctx=want_ctx),
        out_shape=[jax.ShapeDtypeStruct((T_LAT, N_RET_HEADS * HB), BF16),
                   jax.ShapeDtypeStruct((T_CTX, N_RET_HEADS * HB), BF16)],
        grid=(BATCH, N_RET_HEADS),
        in_specs=[lat, lat, lat, lat, ctx, ctx, ctx, ctx, tab, tab, tab, tab,
                  pl.BlockSpec((1, 1, HB), lambda b, h: (h, 0, 0))],
        out_specs=[pl.BlockSpec((SEQ, HB), lambda b, h: (b, h)),
                   pl.BlockSpec((CTX_LEN, HB), lambda b, h: (b, h))],
        scratch_shapes=[
            pltpu.VMEM((RET_NCHUNK, LANES, LANES), F32),
            pltpu.VMEM((RET_NCHUNK, LANES, LANES), F32),
            pltpu.VMEM((RET_NCHUNK_CTX, LANES, LANES), F32),
            pltpu.VMEM((RET_NCHUNK_CTX, LANES, LANES), F32),
        ],
        compiler_params=pltpu.CompilerParams(
            dimension_semantics=("arbitrary", "arbitrary"), vmem_limit_bytes=_vmem_limit(est)),
        name="retention",
    )(rq, rk, rv, rg, rq, rk, rv, rg, *tables, gr)


_YA_W = N_ATT_HEADS * HB
_YN_W = N_NA_HEADS * HB
_YR_W = N_RET_HEADS * HB


def _route(logits):
    lane = lax.broadcasted_iota(jnp.int32, logits.shape, 1).astype(F32)
    valid = lane < N_EXPERTS
    lg = jnp.where(valid, logits, -jnp.inf)
    mx = jnp.max(lg, axis=-1, keepdims=True)
    p = jnp.where(valid, jnp.exp(lg - mx), 0.0)
    best = None
    for grp in range(N_EXPERT_GROUPS):
        lo = float(grp * EXPERTS_PER_GROUP)
        ing = (lane >= lo) & (lane < lo + EXPERTS_PER_GROUP)
        pg = jnp.where(ing, p, -1.0)
        m1 = jnp.max(pg, axis=-1, keepdims=True)
        i1 = jnp.min(jnp.where(pg == m1, lane, float(LANES)), axis=-1, keepdims=True)
        pg2 = jnp.where(lane == i1, -1.0, pg)
        m2 = jnp.max(pg2, axis=-1, keepdims=True)
        i2 = jnp.min(jnp.where(pg2 == m2, lane, float(LANES)), axis=-1, keepdims=True)
        cand = (m1 + m2, m1, m2, i1, i2)
        if best is None:
            best = cand
        else:
            better = cand[0] > best[0]
            best = tuple(jnp.where(better, c, b) for c, b in zip(cand, best))
    _, m1, m2, i1, i2 = best
    w = m1 + m2
    return jnp.where(lane == i1, m1 / w, jnp.where(lane == i2, m2 / w, 0.0))


def _merge_kernel(ya_ref, yn_ref, yr_ref, x_ref, mod_ref, g2_ref, wo_ref, wr_ref, br_ref,
                  xn_ref, h2_ref, gates_ref):
    mod = mod_ref[0]
    gt1 = mod[:, 2 * D_MODEL:3 * D_MODEL]
    sh2 = mod[:, 3 * D_MODEL:4 * D_MODEL]
    sc2 = mod[:, 4 * D_MODEL:5 * D_MODEL]
    m = (jnp.dot(ya_ref[...], wo_ref[0:_YA_W, :], preferred_element_type=F32)
         + jnp.dot(yn_ref[...], wo_ref[_YA_W:_YA_W + _YN_W, :], preferred_element_type=F32)
         + jnp.dot(yr_ref[...], wo_ref[_YA_W + _YN_W:MIX_PAD, :], preferred_element_type=F32))
    x = x_ref[...] + gt1 * m
    xn_ref[...] = x
    ms = jnp.mean(x * x, axis=-1, keepdims=True)
    h2 = x * lax.rsqrt(ms + EPS) * g2_ref[...] * (1.0 + sc2) + sh2
    h2_ref[...] = h2.astype(BF16)
    logits = jnp.dot(h2, wr_ref[...], preferred_element_type=F32,
                     precision=lax.Precision.HIGHEST) + br_ref[...]
    gates_ref[...] = _route(logits)


def _merge(ya, yn, yr, x_all, mod3, g2, wo_pad, wr_pad, br_pad, ntiles):
    rows = ntiles * TM
    est = (2 * MIX_PAD * D_MODEL * 2 + 2 * TM * MIX_PAD * 2 + 6 * TM * D_MODEL * 4
           + 2 * D_MODEL * LANES * 4)
    return pl.pallas_call(
        _merge_kernel,
        out_shape=[jax.ShapeDtypeStruct((rows, D_MODEL), F32),
                   jax.ShapeDtypeStruct((rows, D_MODEL), BF16),
                   jax.ShapeDtypeStruct((rows, LANES), F32)],
        grid=(ntiles,),
        in_specs=[
            pl.BlockSpec((TM, _YA_W), lambda i: (i, 0)),
            pl.BlockSpec((TM, _YN_W), lambda i: (i, 0)),
            pl.BlockSpec((TM, _YR_W), lambda i: (i, 0)),
            pl.BlockSpec((TM, D_MODEL), lambda i: (i, 0)),
            pl.BlockSpec((1, 1, 6 * D_MODEL), lambda i: (_tile_mod_row(i), 0, 0)),
            pl.BlockSpec((1, D_MODEL), lambda i: (0, 0)),
            pl.BlockSpec((MIX_PAD, D_MODEL), lambda i: (0, 0)),
            pl.BlockSpec((D_MODEL, LANES), lambda i: (0, 0)),
            pl.BlockSpec((1, LANES), lambda i: (0, 0)),
        ],
        out_specs=[pl.BlockSpec((TM, D_MODEL), lambda i: (i, 0)),
                   pl.BlockSpec((TM, D_MODEL), lambda i: (i, 0)),
                   pl.BlockSpec((TM, LANES), lambda i: (i, 0))],
        compiler_params=pltpu.CompilerParams(
            dimension_semantics=("arbitrary",), vmem_limit_bytes=_vmem_limit(est)),
        name="merge_outproj_router",
    )(ya, yn, yr, x_all, mod3, g2, wo_pad, wr_pad, br_pad)


def _moe_kernel(h_ref, gates_ref, w1_ref, w3_ref, w2_ref, x_ref, mod_ref, fg_ref, o_ref, acc_sc,
                *, final):
    e = pl.program_id(1)

    @pl.when(e == 0)
    def _():
        acc_sc[...] = jnp.zeros(acc_sc.shape, F32)

    h = h_ref[...]
    a = jnp.dot(h, w1_ref[0], preferred_element_type=F32)
    b = jnp.dot(h, w3_ref[0], preferred_element_type=F32)
    gates = gates_ref[...]
    lane = lax.broadcasted_iota(jnp.int32, gates.shape, 1)
    ge = jnp.sum(jnp.where(lane == e, gates, 0.0), axis=-1, keepdims=True)
    act = (a * jax.nn.sigmoid(a)) * b * ge
    acc_sc[...] += jnp.dot(act.astype(BF16), w2_ref[0], preferred_element_type=F32)

    @pl.when(e == N_EXPERTS - 1)
    def _():
        gt2 = mod_ref[0][:, 5 * D_MODEL:6 * D_MODEL]
        x = x_ref[...] + gt2 * acc_sc[...]
        if final:
            ms = jnp.mean(x * x, axis=-1, keepdims=True)
            x = x * lax.rsqrt(ms + EPS) * fg_ref[...]
        o_ref[...] = x


def _moe(h2, gates, w1, w3, w2, xn, mod3, fg, ntiles, final):
    rows = ntiles * TM_MOE
    ratio = TM_MOE // TM
    est = (2 * 3 * D_MODEL * D_EXPERT * 2 + 2 * TM_MOE * D_MODEL * (2 + 4 + 4)
           + TM_MOE * D_MODEL * 4 + 4 * TM_MOE * D_EXPERT * 4)
    return pl.pallas_call(
        functools.partial(_moe_kernel, final=final),
        out_shape=jax.ShapeDtypeStruct((rows, D_MODEL), F32),
        grid=(ntiles, N_EXPERTS),
        in_specs=[
            pl.BlockSpec((TM_MOE, D_MODEL), lambda i, e: (i, 0)),
            pl.BlockSpec((TM_MOE, LANES), lambda i, e: (i, 0)),
            pl.BlockSpec((1, D_MODEL, D_EXPERT), lambda i, e: (e, 0, 0)),
            pl.BlockSpec((1, D_MODEL, D_EXPERT), lambda i, e: (e, 0, 0)),
            pl.BlockSpec((1, D_EXPERT, D_MODEL), lambda i, e: (e, 0, 0)),
            pl.BlockSpec((TM_MOE, D_MODEL), lambda i, e: (i, 0)),
            pl.BlockSpec((1, 1, 6 * D_MODEL), lambda i, e: (_tile_mod_row(i * ratio), 0, 0)),
            pl.BlockSpec((1, D_MODEL), lambda i, e: (0, 0)),
        ],
        out_specs=pl.BlockSpec((TM_MOE, D_MODEL), lambda i, e: (i, 0)),
        scratch_shapes=[pltpu.VMEM((TM_MOE, D_MODEL), F32)],
        compiler_params=pltpu.CompilerParams(
            dimension_semantics=("arbitrary", "arbitrary"), vmem_limit_bytes=_vmem_limit(est)),
        name="moe_experts",
    )(h2, gates, w1, w3, w2, xn, mod3, fg)


def _rope_tables():
    t = np.arange(SEQ)
    nf = HEAD_DIM // 4
    inv = jnp.asarray(ROPE_THETA, F32) ** (-jnp.arange(nf, dtype=F32) / nf)
    ang_r = jnp.asarray(t // GRID_W, F32)[:, None] * inv[None, :]
    ang_c = jnp.asarray(t % GRID_W, F32)[:, None] * inv[None, :]
    cr, sr, cc, sc = jnp.cos(ang_r), jnp.sin(ang_r), jnp.cos(ang_c), jnp.sin(ang_c)
    zeros = jnp.zeros((SEQ, HEAD_DIM), F32)
    cs = jnp.concatenate([cr, cr, cc, cc, zeros], axis=-1)
    sn = jnp.concatenate([-sr, sr, -sc, sc, zeros], axis=-1)
    ident = jnp.concatenate([jnp.ones((CTX_LEN, HEAD_DIM), F32),
                             jnp.zeros((CTX_LEN, HEAD_DIM), F32)], axis=-1)
    cs = jnp.concatenate([cs, ident], axis=0)
    sn = jnp.concatenate([sn, jnp.zeros((CTX_LEN, LANES), F32)], axis=0)
    return cs, sn


def _pad_out_weight(w_out_l):
    w = w_out_l.reshape(-1, HEAD_DIM, D_MODEL)
    w = jnp.concatenate([w, jnp.zeros_like(w)], axis=1)
    return w.reshape(MIX_PAD, D_MODEL).astype(BF16)


def kernel(x, c, ctx, c_ctx, w_ada, b_ada, norm1_g, norm2_g, w_in, q_norm_g, k_norm_g, na_rpb,
           ret_decay, mix_g, w_out, w_router, b_router, w_exp1, w_exp3, w_exp2, final_g):
    cols, valid = _proj_column_map()
    cs_tab, sn_tab = _rope_tables()

    cvec = jnp.concatenate([c, c_ctx[None, :], jnp.zeros((8 - BATCH - 1, D_MODEL), F32)], axis=0)
    mod_all = _ada_mod(cvec, w_ada, b_ada)

    wr_pad = jnp.concatenate([w_router, jnp.zeros((D_MODEL, LANES - N_EXPERTS), F32)], axis=1)
    br_pad = jnp.concatenate([b_router, jnp.zeros((LANES - N_EXPERTS,), F32)])[None, :]
    zero_lane = jnp.zeros((HEAD_DIM,), F32)

    x_all = jnp.concatenate([x.reshape(T_LAT, D_MODEL), ctx.reshape(T_CTX, D_MODEL)], axis=0)

    for l in range(DEPTH):
        last = l == DEPTH - 1
        want_ctx = not last
        mod3 = mod_all[l].reshape(8, 1, 6 * D_MODEL)
        w_pad = jnp.where(valid[None, :], w_in[l][:, cols], 0.0).astype(BF16)
        qg = jnp.concatenate([q_norm_g[l], zero_lane])[None, :]
        kg = jnp.concatenate([k_norm_g[l], zero_lane])[None, :]
        aq, ak, av, nq, nk, nv, rq, rk, rv, rg = _inproj(
            x_all, mod3, norm1_g[l][None, :], w_pad, cs_tab, sn_tab, qg, kg)

        ga = _pad_heads(mix_g[l][:ATT_Q], N_ATT_HEADS)
        gn = _pad_heads(mix_g[l][ATT_Q:ATT_Q + NA_W], N_NA_HEADS)
        gr = _pad_heads(mix_g[l][ATT_Q + NA_W:], N_RET_HEADS)

        ya = _gqa(aq, ak, av, ga, want_ctx)
        yn_lat, yn_ctx = _neigh(nq, nk, nv, _na_bias_table(na_rpb[l]), gn, want_ctx)
        log_g2 = jax.nn.log_sigmoid(ret_decay[l].astype(F32))
        yr_lat, yr_ctx = _retention(rq, rk, rv, rg, _ret_tables(log_g2), gr, want_ctx)

        wo_pad = _pad_out_weight(w_out[l])
        w1 = w_exp1[l].astype(BF16)
        w3 = w_exp3[l].astype(BF16)
        w2 = w_exp2[l].astype(BF16)
        if last:
            ntiles = LAT_TILES
            yn, yr = yn_lat, yr_lat
        else:
            ntiles = ALL_TILES
            yn = jnp.concatenate([yn_lat, yn_ctx], axis=0)
            yr = jnp.concatenate([yr_lat, yr_ctx], axis=0)
        xn, h2, gates = _merge(ya, yn, yr, x_all, mod3, norm2_g[l][None, :], wo_pad,
                               wr_pad, br_pad, ntiles)
        x_all = _moe(h2, gates, w1, w3, w2, xn, mod3, final_g[None, :],
                     ntiles * TM // TM_MOE, last)

    return x_all[:T_LAT].reshape(BATCH, SEQ, D_MODEL)
```

```python
import functools

import numpy as np
import jax
import jax.numpy as jnp
from jax import lax
from jax.experimental import pallas as pl
from jax.experimental.pallas import tpu as pltpu

D_MODEL = 1024
BATCH = 2
SEQ = 8192
DEPTH = 2
GRID_W = 64
GRID_ROWS = SEQ // GRID_W
CTX_LEN = 256
HEAD_DIM = 64
N_ATT_HEADS = 6
N_ATT_KV = 2
ATT_GRP = N_ATT_HEADS // N_ATT_KV
N_NA_HEADS = 4
N_RET_HEADS = 6
ATT_Q = N_ATT_HEADS * HEAD_DIM
ATT_KV = N_ATT_KV * HEAD_DIM
NA_W = N_NA_HEADS * HEAD_DIM
RET_W = N_RET_HEADS * HEAD_DIM
NA_WIN_ROWS = 8
NA_WIN_COLS = 16
RET_CHUNK = 128
ROPE_THETA = 10000.0
N_EXPERTS = 16
N_EXPERT_GROUPS = 4
EXPERTS_PER_GROUP = N_EXPERTS // N_EXPERT_GROUPS
D_EXPERT = 512
EPS = 1e-6
NEG_INF = -1e30

LANES = 128
VMEM_LIMIT_CAP = 56 * 1024 * 1024

T_LAT = BATCH * SEQ
T_CTX = BATCH * CTX_LEN
T_ALL = T_LAT + T_CTX
TM = 256
LAT_TILES_PER_BATCH = SEQ // TM
LAT_TILES = T_LAT // TM
CTX_TILES = T_CTX // TM
ALL_TILES = LAT_TILES + CTX_TILES
TM_MOE = 512
HB = LANES

F32 = jnp.float32
BF16 = jnp.bfloat16

_SECTIONS = (
    ("aq", 0, N_ATT_HEADS, "pad"),
    ("ak", ATT_Q, N_ATT_KV, "pad"),
    ("av", ATT_Q + ATT_KV, N_ATT_KV, "pad"),
    ("nq", ATT_Q + 2 * ATT_KV, N_NA_HEADS, "pad"),
    ("nk", ATT_Q + 2 * ATT_KV + NA_W, N_NA_HEADS, "pad"),
    ("nv", ATT_Q + 2 * ATT_KV + 2 * NA_W, N_NA_HEADS, "pad"),
    ("rq", ATT_Q + 2 * ATT_KV + 3 * NA_W, N_RET_HEADS, "dup"),
    ("rk", ATT_Q + 2 * ATT_KV + 3 * NA_W + RET_W, N_RET_HEADS, "dup"),
    ("rv", ATT_Q + 2 * ATT_KV + 3 * NA_W + 2 * RET_W, N_RET_HEADS, "pad"),
    ("rg", ATT_Q + 2 * ATT_KV + 3 * NA_W + 3 * RET_W, N_RET_HEADS, "pad"),
)
_SEC_OFF = {}
_off = 0
for _name, _src, _heads, _mode in _SECTIONS:
    _SEC_OFF[_name] = (_off, _heads * HB)
    _off += _heads * HB
NC_PAD = _off
_FEATURE_MAJOR = ("aq", "av")
LOG2E = 1.4426950408889634
MIX_PAD = (N_ATT_HEADS + N_NA_HEADS + N_RET_HEADS) * HB


def _vmem_limit(nbytes):
    return int(min(VMEM_LIMIT_CAP, max(16 * 1024 * 1024, 2 * nbytes)))


def _proj_column_map():
    cols = np.zeros((NC_PAD,), np.int32)
    valid = np.zeros((NC_PAD,), bool)
    for name, src, heads, mode in _SECTIONS:
        off, _ = _SEC_OFF[name]
        for h in range(heads):
            base = off + h * HB
            srcs = src + h * HEAD_DIM + np.arange(HEAD_DIM)
            cols[base:base + HEAD_DIM] = srcs
            valid[base:base + HEAD_DIM] = True
            if mode == "dup":
                cols[base + HEAD_DIM:base + HB] = srcs
                valid[base + HEAD_DIM:base + HB] = True
    return cols, valid


def _pad_heads(v, heads):
    v = v.reshape(heads, 1, HEAD_DIM).astype(F32)
    return jnp.concatenate([v, jnp.zeros_like(v)], axis=-1)


def _tile_mod_row(i):
    return jnp.where(i < LAT_TILES_PER_BATCH, 0, jnp.where(i < LAT_TILES, 1, 2))


ADA_TN = 1536


def _ada_kernel(c_ref, w_ref, b_ref, o_ref):
    c = c_ref[...]
    s = c * jax.nn.sigmoid(c)
    o_ref[0] = jnp.dot(s, w_ref[0], preferred_element_type=F32,
                       precision=lax.Precision.HIGHEST) + b_ref[0]


def _ada_mod(cvec, w_ada, b_ada):
    n = 6 * D_MODEL
    return pl.pallas_call(
        _ada_kernel,
        out_shape=jax.ShapeDtypeStruct((DEPTH, 8, n), F32),
        grid=(DEPTH, n // ADA_TN),
        in_specs=[
            pl.BlockSpec((8, D_MODEL), lambda l, j: (0, 0)),
            pl.BlockSpec((1, D_MODEL, ADA_TN), lambda l, j: (l, 0, j)),
            pl.BlockSpec((1, 1, ADA_TN), lambda l, j: (l, 0, j)),
        ],
        out_specs=pl.BlockSpec((1, 8, ADA_TN), lambda l, j: (l, 0, j)),
        compiler_params=pltpu.CompilerParams(
            dimension_semantics=("arbitrary", "arbitrary"),
            vmem_limit_bytes=_vmem_limit(2 * D_MODEL * ADA_TN * 4)),
        name="ada_mod",
    )(cvec, w_ada, b_ada.reshape(DEPTH, 1, n))


def _rope_swap(t):
    lane = lax.broadcasted_iota(jnp.int32, t.shape, 1)
    first_half = (lane % 32) < 16
    return jnp.where(first_half, pltpu.roll(t, LANES - 16, 1), pltpu.roll(t, 16, 1))


def _inproj_kernel(x_ref, mod_ref, g1_ref, w_ref, cs_ref, sn_ref, qg_ref, kg_ref,
                   aq_ref, ak_ref, av_ref, nq_ref, nk_ref, nv_ref,
                   rq_ref, rk_ref, rv_ref, rg_ref):
    x = x_ref[...]
    mod = mod_ref[0]
    sh1 = mod[:, 0:D_MODEL]
    sc1 = mod[:, D_MODEL:2 * D_MODEL]
    ms = jnp.mean(x * x, axis=-1, keepdims=True)
    h = x * lax.rsqrt(ms + EPS) * g1_ref[...]
    h = (h * (1.0 + sc1) + sh1).astype(BF16)
    cs = cs_ref[...]
    sn = sn_ref[...]

    def proj(name, hidx):
        off, _ = _SEC_OFF[name]
        c0 = off + hidx * HB
        return jnp.dot(h, w_ref[:, c0:c0 + HB], preferred_element_type=F32)

    def normed_rope(z, g):
        ss = jnp.sum(z * z, axis=-1, keepdims=True)
        zn = z * lax.rsqrt(ss * (1.0 / HEAD_DIM) + EPS) * g
        return zn * cs + _rope_swap(zn) * sn

    scale = HEAD_DIM ** -0.5
    for hh in range(N_ATT_HEADS):
        z = normed_rope(proj("aq", hh), qg_ref[...]) * (scale * LOG2E)
        aq_ref[hh * HB:(hh + 1) * HB, :] = z.T.astype(BF16)
    for hh in range(N_ATT_KV):
        z = normed_rope(proj("ak", hh), kg_ref[...])
        ak_ref[:, hh * HB:(hh + 1) * HB] = z.astype(BF16)
        av_ref[hh * HB:(hh + 1) * HB, :] = proj("av", hh).T.astype(BF16)
    for hh in range(N_NA_HEADS):
        nq_ref[:, hh * HB:(hh + 1) * HB] = (proj("nq", hh) * scale).astype(BF16)
        nk_ref[:, hh * HB:(hh + 1) * HB] = proj("nk", hh).astype(BF16)
        nv_ref[:, hh * HB:(hh + 1) * HB] = proj("nv", hh).astype(BF16)
    for hh in range(N_RET_HEADS):
        rq_ref[:, hh * HB:(hh + 1) * HB] = proj("rq", hh).astype(BF16)
        rk_ref[:, hh * HB:(hh + 1) * HB] = (proj("rk", hh) * scale).astype(BF16)
        rv_ref[:, hh * HB:(hh + 1) * HB] = proj("rv", hh).astype(BF16)
        rg_ref[:, hh * HB:(hh + 1) * HB] = proj("rg", hh).astype(BF16)


def _inproj(x_all, mod3, g1, w_pad, cs_tab, sn_tab, qg, kg):
    names = [s[0] for s in _SECTIONS]
    widths = [_SEC_OFF[n][1] for n in names]

    def tab_map(i):
        return (jnp.where(i < LAT_TILES, i % LAT_TILES_PER_BATCH, LAT_TILES_PER_BATCH), 0)

    est = (2 * D_MODEL * NC_PAD * 2 + 2 * TM * D_MODEL * 4 + 2 * TM * NC_PAD * 2
           + 4 * TM * D_MODEL * 4)
    return pl.pallas_call(
        _inproj_kernel,
        out_shape=[jax.ShapeDtypeStruct((w, T_ALL) if n in _FEATURE_MAJOR else (T_ALL, w), BF16)
                   for n, w in zip(names, widths)],
        grid=(ALL_TILES,),
        in_specs=[
            pl.BlockSpec((TM, D_MODEL), lambda i: (i, 0)),
            pl.BlockSpec((1, 1, 6 * D_MODEL), lambda i: (_tile_mod_row(i), 0, 0)),
            pl.BlockSpec((1, D_MODEL), lambda i: (0, 0)),
            pl.BlockSpec((D_MODEL, NC_PAD), lambda i: (0, 0)),
            pl.BlockSpec((TM, HB), tab_map),
            pl.BlockSpec((TM, HB), tab_map),
            pl.BlockSpec((1, HB), lambda i: (0, 0)),
            pl.BlockSpec((1, HB), lambda i: (0, 0)),
        ],
        out_specs=[pl.BlockSpec((w, TM), lambda i: (0, i)) if n in _FEATURE_MAJOR
                   else pl.BlockSpec((TM, w), lambda i: (i, 0)) for n, w in zip(names, widths)],
        compiler_params=pltpu.CompilerParams(
            dimension_semantics=("arbitrary",), vmem_limit_bytes=_vmem_limit(est)),
        name="norm_inproj",
    )(x_all, mod3, g1, w_pad, cs_tab, sn_tab, qg, kg)


ATT_TK = 1024
ATT_NK = SEQ // ATT_TK
ATT_CK = 256


def _attn_kernel(qt_ref, k_ref, vt_ref, kc_ref, vct_ref, g_ref, o_ref, m_sc, l_sc, acc_sc):
    i = pl.program_id(2)
    j = pl.program_id(3)
    is_ctx_q = i >= LAT_TILES_PER_BATCH

    @pl.when(j == 0)
    def _():
        m_sc[...] = jnp.full(m_sc.shape, -jnp.inf, F32)
        l_sc[...] = jnp.zeros(l_sc.shape, F32)
        acc_sc[...] = jnp.zeros(acc_sc.shape, F32)

    def scores(kr, c):
        k = kr[c * ATT_CK:(c + 1) * ATT_CK, :]
        return [jnp.dot(k, qt_ref[hh * HB:(hh + 1) * HB, :], preferred_element_type=F32)
                for hh in range(ATT_GRP)]

    def step(kr, vtr, nkeys):
        nchunk = nkeys // ATT_CK
        ss = scores(kr, 0)
        for c in range(nchunk):
            cur = ss
            if c + 1 < nchunk:
                ss = scores(kr, c + 1)
            vt = vtr[0:HEAD_DIM, c * ATT_CK:(c + 1) * ATT_CK]
            ps, alphas = [], []
            for hh in range(ATT_GRP):
                m_prev = m_sc[hh]
                m_new = jnp.maximum(m_prev, jnp.max(cur[hh], axis=0, keepdims=True))
                alpha = jnp.exp2(m_prev - m_new)
                p = jnp.exp2(cur[hh] - m_new)
                l_sc[hh] = alpha * l_sc[hh] + jnp.sum(p, axis=0, keepdims=True)
                m_sc[hh] = m_new
                ps.append(p.astype(BF16))
                alphas.append(alpha)
            for hh in range(ATT_GRP):
                acc_sc[hh] = alphas[hh] * acc_sc[hh] + jnp.dot(vt, ps[hh],
                                                               preferred_element_type=F32)

    @pl.when(jnp.logical_not(is_ctx_q))
    def _():
        step(k_ref, vt_ref, ATT_TK)

    @pl.when(j == ATT_NK - 1)
    def _():
        step(kc_ref, vct_ref, CTX_LEN)
        for hh in range(ATT_GRP):
            o = acc_sc[hh] / l_sc[hh]
            ms = jnp.sum(o * o, axis=0, keepdims=True) * (1.0 / HEAD_DIM)
            g = g_ref[hh * HB:hh * HB + HEAD_DIM, :]
            y = o * lax.rsqrt(ms + EPS) * jnp.concatenate([g] * (TM // LANES), axis=1)
            y = jnp.concatenate([y, jnp.zeros_like(y)], axis=0)
            o_ref[:, hh * HB:(hh + 1) * HB] = y.T.astype(BF16)


def _gqa(aqt, ak, avt, ga_cols, want_ctx):
    nq = LAT_TILES_PER_BATCH + (1 if want_ctx else 0)

    def q_tile(b, i):
        return jnp.where(i < LAT_TILES_PER_BATCH, b * LAT_TILES_PER_BATCH + i, LAT_TILES + b)

    est = (2 * ATT_GRP * HB * TM * 2 + 4 * ATT_TK * HB * 2 + 4 * TM * HB * 2
           + ATT_GRP * TM * HB * 4 * 3 + 8 * TM * ATT_TK * 4)
    return pl.pallas_call(
        _attn_kernel,
        out_shape=jax.ShapeDtypeStruct((T_ALL, N_ATT_HEADS * HB), BF16),
        grid=(BATCH, N_ATT_KV, nq, ATT_NK),
        in_specs=[
            pl.BlockSpec((ATT_GRP * HB, TM), lambda b, c, i, j: (c, q_tile(b, i))),
            pl.BlockSpec((ATT_TK, HB), lambda b, c, i, j: (b * ATT_NK + j, c)),
            pl.BlockSpec((HB, ATT_TK), lambda b, c, i, j: (c, b * ATT_NK + j)),
            pl.BlockSpec((CTX_LEN, HB), lambda b, c, i, j: (LAT_TILES + b, c)),
            pl.BlockSpec((HB, CTX_LEN), lambda b, c, i, j: (c, LAT_TILES + b)),
            pl.BlockSpec((ATT_GRP * HB, LANES), lambda b, c, i, j: (c, 0)),
        ],
        out_specs=pl.BlockSpec((TM, ATT_GRP * HB), lambda b, c, i, j: (q_tile(b, i), c)),
        scratch_shapes=[
            pltpu.VMEM((ATT_GRP, 1, TM), F32),
            pltpu.VMEM((ATT_GRP, 1, TM), F32),
            pltpu.VMEM((ATT_GRP, HEAD_DIM, TM), F32),
        ],
        compiler_params=pltpu.CompilerParams(
            dimension_semantics=("arbitrary",) * 4, vmem_limit_bytes=_vmem_limit(est)),
        name="gqa_attn",
    )(aqt, ak, avt, ak, avt, ga_cols)


NA_BAND = NA_WIN_ROWS * GRID_W
NA_CLASSES = 8
_NA_CLASS_ROWS = (0, 1, 2, 3, GRID_ROWS // 2, GRID_ROWS - 3, GRID_ROWS - 2, GRID_ROWS - 1)


def _na_bias_table(rpb):
    wr, wc = NA_WIN_ROWS, NA_WIN_COLS
    r = np.asarray(_NA_CLASS_ROWS)
    ridx = np.clip(r - wr // 2, 0, GRID_ROWS - wr)[:, None] + np.arange(wr)[None, :]
    dr = ridx - r[:, None] + (wr - 1)
    col = np.arange(GRID_W)
    cstart = np.clip(col - wc // 2, 0, GRID_W - wc)
    col_ok = (col[None, :] >= cstart[:, None]) & (col[None, :] < cstart[:, None] + wc)
    dc = np.clip(col[None, :] - col[:, None] + (wc - 1), 0, 2 * wc - 2)
    bias = rpb[:, dr][..., dc]
    bias = bias.transpose(0, 1, 3, 2, 4).reshape(N_NA_HEADS, NA_CLASSES, GRID_W, NA_BAND)
    mask = np.tile(col_ok, (1, wr))
    return jnp.where(mask[None, None], bias.astype(F32), NEG_INF)


def _head_rms_gain(o, g):
    ms = jnp.sum(o * o, axis=-1, keepdims=True) * (1.0 / HEAD_DIM)
    return o * lax.rsqrt(ms + EPS) * g


def _na_kernel(q_ref, k_ref, v_ref, kc_ref, vc_ref, qc_ref, bias_ref, g_ref, o_ref, oc_ref,
               *, want_ctx):
    kc = kc_ref[...]
    vc = vc_ref[...]
    g = g_ref[0]
    half = NA_WIN_ROWS // 2
    last = GRID_ROWS - NA_WIN_ROWS

    def row(r, carry):
        start = jnp.clip(r - half, 0, last)
        cls = jnp.where(r < half, r, jnp.where(r > last + half, r - last, half))
        q = q_ref[pl.ds(pl.multiple_of(r * GRID_W, GRID_W), GRID_W), :]
        kb = k_ref[pl.ds(pl.multiple_of(start * GRID_W, GRID_W), NA_BAND), :]
        vb = v_ref[pl.ds(pl.multiple_of(start * GRID_W, GRID_W), NA_BAND), :]
        s = lax.dot_general(q, kb, (((1,), (1,)), ((), ())), preferred_element_type=F32)
        bt = bias_ref[0, cls]
        s = jnp.where(bt > 0.5 * NEG_INF, s + bt, NEG_INF)
        sc = lax.dot_general(q, kc, (((1,), (1,)), ((), ())), preferred_element_type=F32)
        m = jnp.maximum(jnp.max(s, axis=-1, keepdims=True), jnp.max(sc, axis=-1, keepdims=True))
        p = jnp.exp(s - m)
        pc = jnp.exp(sc - m)
        l = jnp.sum(p, axis=-1, keepdims=True) + jnp.sum(pc, axis=-1, keepdims=True)
        o = (jnp.dot(p.astype(BF16), vb, preferred_element_type=F32)
             + jnp.dot(pc.astype(BF16), vc, preferred_element_type=F32)) / l
        o_ref[pl.ds(pl.multiple_of(r * GRID_W, GRID_W), GRID_W), :] = (
            _head_rms_gain(o, g).astype(BF16))
        return carry

    lax.fori_loop(0, GRID_ROWS, row, 0, unroll=2)

    if want_ctx:
        sc = lax.dot_general(qc_ref[...], kc, (((1,), (1,)), ((), ())),
                             preferred_element_type=F32)
        m = jnp.max(sc, axis=-1, keepdims=True)
        pc = jnp.exp(sc - m)
        l = jnp.sum(pc, axis=-1, keepdims=True)
        o = jnp.dot(pc.astype(BF16), vc, preferred_element_type=F32) / l
        oc_ref[...] = _head_rms_gain(o, g).astype(BF16)
    else:
        oc_ref[...] = jnp.zeros(oc_ref.shape, oc_ref.dtype)


def _neigh(nq, nk, nv, bias_tab, gn, want_ctx):
    lat = pl.BlockSpec((SEQ, HB), lambda b, h: (b, h))
    ctx = pl.BlockSpec((CTX_LEN, HB), lambda b, h: (T_LAT // CTX_LEN + b, h))
    est = 2 * (4 * SEQ * HB * 2 + 4 * CTX_LEN * HB * 2 + NA_CLASSES * GRID_W * NA_BAND * 4)
    return pl.pallas_call(
        functools.partial(_na_kernel, want_ctx=want_ctx),
        out_shape=[jax.ShapeDtypeStruct((T_LAT, N_NA_HEADS * HB), BF16),
                   jax.ShapeDtypeStruct((T_CTX, N_NA_HEADS * HB), BF16)],
        grid=(BATCH, N_NA_HEADS),
        in_specs=[lat, lat, lat, ctx, ctx, ctx,
                  pl.BlockSpec((1, NA_CLASSES, GRID_W, NA_BAND), lambda b, h: (h, 0, 0, 0)),
                  pl.BlockSpec((1, 1, HB), lambda b, h: (h, 0, 0))],
        out_specs=[pl.BlockSpec((SEQ, HB), lambda b, h: (b, h)),
                   pl.BlockSpec((CTX_LEN, HB), lambda b, h: (b, h))],
        compiler_params=pltpu.CompilerParams(
            dimension_semantics=("arbitrary", "arbitrary"), vmem_limit_bytes=_vmem_limit(est)),
        name="neigh_attn",
    )(nq, nk, nv, nk, nv, nq, bias_tab, gn)


RET_NCHUNK = SEQ // RET_CHUNK
RET_NCHUNK_CTX = CTX_LEN // RET_CHUNK


def _ret_tables(log_g2):
    lf = log_g2[0][:, None, None]
    lb = log_g2[1][:, None, None]
    pos = jnp.arange(RET_CHUNK, dtype=F32)
    i = pos[None, :, None]
    j = pos[None, None, :]
    diff = i - j
    dm = jnp.where(diff > 0, jnp.exp(lf * jnp.maximum(diff, 0.0)),
                   jnp.where(diff < 0, jnp.exp(lb * jnp.maximum(-diff, 0.0)), 2.0)) * 0.5
    fwd_lane = (jnp.arange(LANES) < HEAD_DIM)[None, None, :]
    xi = jnp.where(fwd_lane, jnp.exp(lf * (i + 1.0)), jnp.exp(lb * (RET_CHUNK - i)))
    zt = jnp.where(fwd_lane, jnp.exp(lf * (RET_CHUNK - 1.0 - i)), jnp.exp(lb * i))
    fwd_row = (jnp.arange(LANES) < HEAD_DIM)[None, :, None]
    dec = jnp.where(fwd_row, jnp.exp(lf * RET_CHUNK), jnp.exp(lb * RET_CHUNK))
    dec = jnp.broadcast_to(dec, (N_RET_HEADS, LANES, LANES))
    return dm.astype(F32), xi.astype(F32), zt.astype(F32), dec.astype(F32)


def _ret_kernel(q_ref, k_ref, v_ref, gt_ref, qc_ref, kc_ref, vc_ref, gtc_ref,
                dm_ref, xi_ref, zt_ref, dec_ref, g_ref, o_ref, oc_ref,
                u_sc, s_sc, uc_sc, sc_sc, *, want_ctx):
    dm = dm_ref[0]
    xi = xi_ref[0]
    zt = zt_ref[0]
    dec = dec_ref[0]
    dec_f = dec[0:HEAD_DIM]
    dec_b = dec[HEAD_DIM:LANES]
    g = g_ref[0]
    C = RET_CHUNK

    def chunk_state_update(kr, vr, usc, n):
        rows = pl.ds(pl.multiple_of(n * C, C), C)
        kz = (kr[rows, :].astype(F32) * zt).T.astype(BF16)
        usc[n] = jnp.dot(kz, vr[rows, :], preferred_element_type=F32)

    def chunk_out(qr, kr, vr, gtr, ssc, outr, n):
        rows = pl.ds(pl.multiple_of(n * C, C), C)
        qd = qr[rows, :]
        s2 = lax.dot_general(qd, kr[rows, :], (((1,), (1,)), ((), ())),
                             preferred_element_type=F32)
        inner = jnp.dot((s2 * dm).astype(BF16), vr[rows, :], preferred_element_type=F32)
        qx = (qd.astype(F32) * xi).astype(BF16)
        cross = jnp.dot(qx, ssc[n].astype(BF16), preferred_element_type=F32)
        y = _head_rms_gain(inner + cross, g)
        gate = gtr[rows, :].astype(F32)
        outr[rows, :] = (y * (gate * jax.nn.sigmoid(gate))).astype(BF16)

    def scan_states(usc, ssc, nchunk, init_f, init_b):
        def fwd(n, sf):
            ssc[n, 0:HEAD_DIM, :] = sf
            return dec_f * sf + usc[n, 0:HEAD_DIM, :]

        def bwd(t, sb):
            n = nchunk - 1 - t
            ssc[n, HEAD_DIM:LANES, :] = sb
            return dec_b * sb + usc[n, HEAD_DIM:LANES, :]

        return (lax.fori_loop(0, nchunk, fwd, init_f), lax.fori_loop(0, nchunk, bwd, init_b))

    zero = jnp.zeros((HEAD_DIM, LANES), F32)
    for n in range(RET_NCHUNK_CTX):
        chunk_state_update(kc_ref, vc_ref, uc_sc, n)
    ctx_f, ctx_b = scan_states(uc_sc, sc_sc, RET_NCHUNK_CTX, zero, zero)
    if want_ctx:
        for n in range(RET_NCHUNK_CTX):
            chunk_out(qc_ref, kc_ref, vc_ref, gtc_ref, sc_sc, oc_ref, n)
    else:
        oc_ref[...] = jnp.zeros(oc_ref.shape, oc_ref.dtype)

    def upd(n, carry):
        chunk_state_update(k_ref, v_ref, u_sc, n)
        return carry

    lax.fori_loop(0, RET_NCHUNK, upd, 0, unroll=2)
    scan_states(u_sc, s_sc, RET_NCHUNK, ctx_f, ctx_b)

    def out(n, carry):
        chunk_out(q_ref, k_ref, v_ref, gt_ref, s_sc, o_ref, n)
        return carry

    lax.fori_loop(0, RET_NCHUNK, out, 0, unroll=2)


def _retention(rq, rk, rv, rg, tables, gr, want_ctx):
    lat = pl.BlockSpec((SEQ, HB), lambda b, h: (b, h))
    ctx = pl.BlockSpec((CTX_LEN, HB), lambda b, h: (T_LAT // CTX_LEN + b, h))
    tab = pl.BlockSpec((1, LANES, LANES), lambda b, h: (h, 0, 0))
    est = (2 * 5 * SEQ * HB * 2 + 2 * RET_NCHUNK * LANES * LANES * 4 + 8 * LANES * LANES * 4)
    return pl.pallas_call(
        functools.partial(_ret_kernel, want_ctx=want_ctx),
        out_shape=[jax.ShapeDtypeStruct((T_LAT, N_RET_HEADS * HB), BF16),
                   jax.ShapeDtypeStruct((T_CTX, N_RET_HEADS * HB), BF16)],
        grid=(BATCH, N_RET_HEADS),
        in_specs=[lat, lat, lat, lat, ctx, ctx, ctx, ctx, tab, tab, tab, tab,
                  pl.BlockSpec((1, 1, HB), lambda b, h: (h, 0, 0))],
        out_specs=[pl.BlockSpec((SEQ, HB), lambda b, h: (b, h)),
                   pl.BlockSpec((CTX_LEN, HB), lambda b, h: (b, h))],
        scratch_shapes=[
            pltpu.VMEM((RET_NCHUNK, LANES, LANES), F32),
            pltpu.VMEM((RET_NCHUNK, LANES, LANES), F32),
            pltpu.VMEM((RET_NCHUNK_CTX, LANES, LANES), F32),
            pltpu.VMEM((RET_NCHUNK_CTX, LANES, LANES), F32),
        ],
        compiler_params=pltpu.CompilerParams(
            dimension_semantics=("arbitrary", "arbitrary"), vmem_limit_bytes=_vmem_limit(est)),
        name="retention",
    )(rq, rk, rv, rg, rq, rk, rv, rg, *tables, gr)


_YA_W = N_ATT_HEADS * HB
_YN_W = N_NA_HEADS * HB
_YR_W = N_RET_HEADS * HB


def _route(logits):
    lane = lax.broadcasted_iota(jnp.int32, logits.shape, 1).astype(F32)
    valid = lane < N_EXPERTS
    lg = jnp.where(valid, logits, -jnp.inf)
    mx = jnp.max(lg, axis=-1, keepdims=True)
    p = jnp.where(valid, jnp.exp(lg - mx), 0.0)
    best = None
    for grp in range(N_EXPERT_GROUPS):
        lo = float(grp * EXPERTS_PER_GROUP)
        ing = (lane >= lo) & (lane < lo + EXPERTS_PER_GROUP)
        pg = jnp.where(ing, p, -1.0)
        m1 = jnp.max(pg, axis=-1, keepdims=True)
        i1 = jnp.min(jnp.where(pg == m1, lane, float(LANES)), axis=-1, keepdims=True)
        pg2 = jnp.where(lane == i1, -1.0, pg)
        m2 = jnp.max(pg2, axis=-1, keepdims=True)
        i2 = jnp.min(jnp.where(pg2 == m2, lane, float(LANES)), axis=-1, keepdims=True)
        cand = (m1 + m2, m1, m2, i1, i2)
        if best is None:
            best = cand
        else:
            better = cand[0] > best[0]
            best = tuple(jnp.where(better, c, b) for c, b in zip(cand, best))
    _, m1, m2, i1, i2 = best
    w = m1 + m2
    return jnp.where(lane == i1, m1 / w, jnp.where(lane == i2, m2 / w, 0.0))


def _merge_kernel(ya_ref, yn_ref, yr_ref, x_ref, mod_ref, g2_ref, wo_ref, wr_ref, br_ref,
                  xn_ref, h2_ref, gates_ref):
    mod = mod_ref[0]
    gt1 = mod[:, 2 * D_MODEL:3 * D_MODEL]
    sh2 = mod[:, 3 * D_MODEL:4 * D_MODEL]
    sc2 = mod[:, 4 * D_MODEL:5 * D_MODEL]
    m = (jnp.dot(ya_ref[...], wo_ref[0:_YA_W, :], preferred_element_type=F32)
         + jnp.dot(yn_ref[...], wo_ref[_YA_W:_YA_W + _YN_W, :], preferred_element_type=F32)
         + jnp.dot(yr_ref[...], wo_ref[_YA_W + _YN_W:MIX_PAD, :], preferred_element_type=F32))
    x = x_ref[...] + gt1 * m
    xn_ref[...] = x
    ms = jnp.mean(x * x, axis=-1, keepdims=True)
    h2 = x * lax.rsqrt(ms + EPS) * g2_ref[...] * (1.0 + sc2) + sh2
    h2_ref[...] = h2.astype(BF16)
    logits = jnp.dot(h2, wr_ref[...], preferred_element_type=F32,
                     precision=lax.Precision.HIGHEST) + br_ref[...]
    gates_ref[...] = _route(logits)


def _merge(ya, yn, yr, x_all, mod3, g2, wo_pad, wr_pad, br_pad, ntiles):
    rows = ntiles * TM
    est = (2 * MIX_PAD * D_MODEL * 2 + 2 * TM * MIX_PAD * 2 + 6 * TM * D_MODEL * 4
           + 2 * D_MODEL * LANES * 4)
    return pl.pallas_call(
        _merge_kernel,
        out_shape=[jax.ShapeDtypeStruct((rows, D_MODEL), F32),
                   jax.ShapeDtypeStruct((rows, D_MODEL), BF16),
                   jax.ShapeDtypeStruct((rows, LANES), F32)],
        grid=(ntiles,),
        in_specs=[
            pl.BlockSpec((TM, _YA_W), lambda i: (i, 0)),
            pl.BlockSpec((TM, _YN_W), lambda i: (i, 0)),
            pl.BlockSpec((TM, _YR_W), lambda i: (i, 0)),
            pl.BlockSpec((TM, D_MODEL), lambda i: (i, 0)),
            pl.BlockSpec((1, 1, 6 * D_MODEL), lambda i: (_tile_mod_row(i), 0, 0)),
            pl.BlockSpec((1, D_MODEL), lambda i: (0, 0)),
            pl.BlockSpec((MIX_PAD, D_MODEL), lambda i: (0, 0)),
            pl.BlockSpec((D_MODEL, LANES), lambda i: (0, 0)),
            pl.BlockSpec((1, LANES), lambda i: (0, 0)),
        ],
        out_specs=[pl.BlockSpec((TM, D_MODEL), lambda i: (i, 0)),
                   pl.BlockSpec((TM, D_MODEL), lambda i: (i, 0)),
                   pl.BlockSpec((TM, LANES), lambda i: (i, 0))],
        compiler_params=pltpu.CompilerParams(
            dimension_semantics=("arbitrary",), vmem_limit_bytes=_vmem_limit(est)),
        name="merge_outproj_router",
    )(ya, yn, yr, x_all, mod3, g2, wo_pad, wr_pad, br_pad)


def _moe_kernel(h_ref, gates_ref, w1_ref, w3_ref, w2_ref, x_ref, mod_ref, fg_ref, o_ref, acc_sc,
                *, final):
    e = pl.program_id(1)

    @pl.when(e == 0)
    def _():
        acc_sc[...] = jnp.zeros(acc_sc.shape, F32)

    h = h_ref[...]
    a = jnp.dot(h, w1_ref[0], preferred_element_type=F32)
    b = jnp.dot(h, w3_ref[0], preferred_element_type=F32)
    gates = gates_ref[...]
    lane = lax.broadcasted_iota(jnp.int32, gates.shape, 1)
    ge = jnp.sum(jnp.where(lane == e, gates, 0.0), axis=-1, keepdims=True)
    act = (a * jax.nn.sigmoid(a)) * b * ge
    acc_sc[...] += jnp.dot(act.astype(BF16), w2_ref[0], preferred_element_type=F32)

    @pl.when(e == N_EXPERTS - 1)
    def _():
        gt2 = mod_ref[0][:, 5 * D_MODEL:6 * D_MODEL]
        x = x_ref[...] + gt2 * acc_sc[...]
        if final:
            ms = jnp.mean(x * x, axis=-1, keepdims=True)
            x = x * lax.rsqrt(ms + EPS) * fg_ref[...]
        o_ref[...] = x


def _moe(h2, gates, w1, w3, w2, xn, mod3, fg, ntiles, final):
    rows = ntiles * TM_MOE
    ratio = TM_MOE // TM
    est = (2 * 3 * D_MODEL * D_EXPERT * 2 + 2 * TM_MOE * D_MODEL * (2 + 4 + 4)
           + TM_MOE * D_MODEL * 4 + 4 * TM_MOE * D_EXPERT * 4)
    return pl.pallas_call(
        functools.partial(_moe_kernel, final=final),
        out_shape=jax.ShapeDtypeStruct((rows, D_MODEL), F32),
        grid=(ntiles, N_EXPERTS),
        in_specs=[
            pl.BlockSpec((TM_MOE, D_MODEL), lambda i, e: (i, 0)),
            pl.BlockSpec((TM_MOE, LANES), lambda i, e: (i, 0)),
            pl.BlockSpec((1, D_MODEL, D_EXPERT), lambda i, e: (e, 0, 0)),
            pl.BlockSpec((1, D_MODEL, D_EXPERT), lambda i, e: (e, 0, 0)),
            pl.BlockSpec((1, D_EXPERT, D_MODEL), lambda i, e: (e, 0, 0)),
            pl.BlockSpec((TM_MOE, D_MODEL), lambda i, e: (i, 0)),
            pl.BlockSpec((1, 1, 6 * D_MODEL), lambda i, e: (_tile_mod_row(i * ratio), 0, 0)),
            pl.BlockSpec((1, D_MODEL), lambda i, e: (0, 0)),
        ],
        out_specs=pl.BlockSpec((TM_MOE, D_MODEL), lambda i, e: (i, 0)),
        scratch_shapes=[pltpu.VMEM((TM_MOE, D_MODEL), F32)],
        compiler_params=pltpu.CompilerParams(
            dimension_semantics=("arbitrary", "arbitrary"), vmem_limit_bytes=_vmem_limit(est)),
        name="moe_experts",
    )(h2, gates, w1, w3, w2, xn, mod3, fg)


def _rope_tables():
    t = np.arange(SEQ)
    nf = HEAD_DIM // 4
    inv = jnp.asarray(ROPE_THETA, F32) ** (-jnp.arange(nf, dtype=F32) / nf)
    ang_r = jnp.asarray(t // GRID_W, F32)[:, None] * inv[None, :]
    ang_c = jnp.asarray(t % GRID_W, F32)[:, None] * inv[None, :]
    cr, sr, cc, sc = jnp.cos(ang_r), jnp.sin(ang_r), jnp.cos(ang_c), jnp.sin(ang_c)
    zeros = jnp.zeros((SEQ, HEAD_DIM), F32)
    cs = jnp.concatenate([cr, cr, cc, cc, zeros], axis=-1)
    sn = jnp.concatenate([-sr, sr, -sc, sc, zeros], axis=-1)
    ident = jnp.concatenate([jnp.ones((CTX_LEN, HEAD_DIM), F32),
                             jnp.zeros((CTX_LEN, HEAD_DIM), F32)], axis=-1)
    cs = jnp.concatenate([cs, ident], axis=0)
    sn = jnp.concatenate([sn, jnp.zeros((CTX_LEN, LANES), F32)], axis=0)
    return cs, sn


def _pad_out_weight(w_out_l):
    w = w_out_l.reshape(-1, HEAD_DIM, D_MODEL)
    w = jnp.concatenate([w, jnp.zeros_like(w)], axis=1)
    return w.reshape(MIX_PAD, D_MODEL).astype(BF16)


def kernel(x, c, ctx, c_ctx, w_ada, b_ada, norm1_g, norm2_g, w_in, q_norm_g, k_norm_g, na_rpb,
           ret_decay, mix_g, w_out, w_router, b_router, w_exp1, w_exp3, w_exp2, final_g):
    cols, valid = _proj_column_map()
    cs_tab, sn_tab = _rope_tables()

    cvec = jnp.concatenate([c, c_ctx[None, :], jnp.zeros((8 - BATCH - 1, D_MODEL), F32)], axis=0)
    mod_all = _ada_mod(cvec, w_ada, b_ada)

    wr_pad = jnp.concatenate([w_router, jnp.zeros((D_MODEL, LANES - N_EXPERTS), F32)], axis=1)
    br_pad = jnp.concatenate([b_router, jnp.zeros((LANES - N_EXPERTS,), F32)])[None, :]
    zero_lane = jnp.zeros((HEAD_DIM,), F32)

    x_all = jnp.concatenate([x.reshape(T_LAT, D_MODEL), ctx.reshape(T_CTX, D_MODEL)], axis=0)

    for l in range(DEPTH):
        last = l == DEPTH - 1
        want_ctx = not last
        mod3 = mod_all[l].reshape(8, 1, 6 * D_MODEL)
        w_pad = jnp.where(valid[None, :], w_in[l][:, cols], 0.0).astype(BF16)
        qg = jnp.concatenate([q_norm_g[l], zero_lane])[None, :]
        kg = jnp.concatenate([k_norm_g[l], zero_lane])[None, :]
        aq, ak, av, nq, nk, nv, rq, rk, rv, rg = _inproj(
            x_all, mod3, norm1_g[l][None, :], w_pad, cs_tab, sn_tab, qg, kg)

        ga = jnp.broadcast_to(mix_g[l][:ATT_Q].reshape(N_ATT_HEADS, HEAD_DIM, 1),
                              (N_ATT_HEADS, HEAD_DIM, LANES))
        ga = jnp.concatenate([ga, jnp.zeros_like(ga)], axis=1).reshape(N_ATT_HEADS * HB, LANES)
        gn = _pad_heads(mix_g[l][ATT_Q:ATT_Q + NA_W], N_NA_HEADS)
        gr = _pad_heads(mix_g[l][ATT_Q + NA_W:], N_RET_HEADS)

        ya = _gqa(aq, ak, av, ga, want_ctx)
        yn_lat, yn_ctx = _neigh(nq, nk, nv, _na_bias_table(na_rpb[l]), gn, want_ctx)
        log_g2 = jax.nn.log_sigmoid(ret_decay[l].astype(F32))
        yr_lat, yr_ctx = _retention(rq, rk, rv, rg, _ret_tables(log_g2), gr, want_ctx)

        wo_pad = _pad_out_weight(w_out[l])
        w1 = w_exp1[l].astype(BF16)
        w3 = w_exp3[l].astype(BF16)
        w2 = w_exp2[l].astype(BF16)
        if last:
            ntiles = LAT_TILES
            yn, yr = yn_lat, yr_lat
        else:
            ntiles = ALL_TILES
            yn = jnp.concatenate([yn_lat, yn_ctx], axis=0)
            yr = jnp.concatenate([yr_lat, yr_ctx], axis=0)
        xn, h2, gates = _merge(ya, yn, yr, x_all, mod3, norm2_g[l][None, :], wo_pad,
                               wr_pad, br_pad, ntiles)
        x_all = _moe(h2, gates, w1, w3, w2, xn, mod3, final_g[None, :],
                     ntiles * TM // TM_MOE, last)

    return x_all[:T_LAT].reshape(BATCH, SEQ, D_MODEL)
```

```python
import functools

import numpy as np
import jax
import jax.numpy as jnp
from jax import lax
from jax.experimental import pallas as pl
from jax.experimental.pallas import tpu as pltpu

D_MODEL = 1024
BATCH = 2
SEQ = 8192
DEPTH = 2
GRID_W = 64
GRID_ROWS = SEQ // GRID_W
CTX_LEN = 256
HEAD_DIM = 64
N_ATT_HEADS = 6
N_ATT_KV = 2
ATT_GRP = N_ATT_HEADS // N_ATT_KV
N_NA_HEADS = 4
N_RET_HEADS = 6
ATT_Q = N_ATT_HEADS * HEAD_DIM
ATT_KV = N_ATT_KV * HEAD_DIM
NA_W = N_NA_HEADS * HEAD_DIM
RET_W = N_RET_HEADS * HEAD_DIM
NA_WIN_ROWS = 8
NA_WIN_COLS = 16
RET_CHUNK = 128
ROPE_THETA = 10000.0
N_EXPERTS = 16
N_EXPERT_GROUPS = 4
EXPERTS_PER_GROUP = N_EXPERTS // N_EXPERT_GROUPS
D_EXPERT = 512
EPS = 1e-6
NEG_INF = -1e30

LANES = 128
VMEM_LIMIT_CAP = 56 * 1024 * 1024

T_LAT = BATCH * SEQ
T_CTX = BATCH * CTX_LEN
T_ALL = T_LAT + T_CTX
TM = 256
LAT_TILES_PER_BATCH = SEQ // TM
LAT_TILES = T_LAT // TM
CTX_TILES = T_CTX // TM
ALL_TILES = LAT_TILES + CTX_TILES
HB = LANES

F32 = jnp.float32
BF16 = jnp.bfloat16

_SECTIONS = (
    ("aq", 0, N_ATT_HEADS, "pad"),
    ("ak", ATT_Q, N_ATT_KV, "pad"),
    ("av", ATT_Q + ATT_KV, N_ATT_KV, "pad"),
    ("nq", ATT_Q + 2 * ATT_KV, N_NA_HEADS, "pad"),
    ("nk", ATT_Q + 2 * ATT_KV + NA_W, N_NA_HEADS, "pad"),
    ("nv", ATT_Q + 2 * ATT_KV + 2 * NA_W, N_NA_HEADS, "pad"),
    ("rq", ATT_Q + 2 * ATT_KV + 3 * NA_W, N_RET_HEADS, "dup"),
    ("rk", ATT_Q + 2 * ATT_KV + 3 * NA_W + RET_W, N_RET_HEADS, "dup"),
    ("rv", ATT_Q + 2 * ATT_KV + 3 * NA_W + 2 * RET_W, N_RET_HEADS, "pad"),
    ("rg", ATT_Q + 2 * ATT_KV + 3 * NA_W + 3 * RET_W, N_RET_HEADS, "pad"),
)
_SEC_OFF = {}
_off = 0
for _name, _src, _heads, _mode in _SECTIONS:
    _SEC_OFF[_name] = (_off, _heads * HB)
    _off += _heads * HB
NC_PAD = _off
_FEATURE_MAJOR = ("aq", "av")
LOG2E = 1.4426950408889634
MIX_PAD = (N_ATT_HEADS + N_NA_HEADS + N_RET_HEADS) * HB


def _vmem_limit(nbytes):
    return int(min(VMEM_LIMIT_CAP, max(16 * 1024 * 1024, 2 * nbytes)))


def _pad_heads(v, heads):
    v = v.reshape(heads, 1, HEAD_DIM).astype(F32)
    return jnp.concatenate([v, jnp.zeros_like(v)], axis=-1)


def _tile_mod_row(i):
    return jnp.where(i < LAT_TILES_PER_BATCH, 0, jnp.where(i < LAT_TILES, 1, 2))


ADA_TN = 1536


def _ada_kernel(c_ref, w_ref, b_ref, o_ref):
    c = c_ref[...]
    s = c * jax.nn.sigmoid(c)
    o_ref[0] = jnp.dot(s, w_ref[0], preferred_element_type=F32,
                       precision=lax.Precision.HIGHEST) + b_ref[0]


def _ada_mod(cvec, w_ada, b_ada):
    n = 6 * D_MODEL
    return pl.pallas_call(
        _ada_kernel,
        out_shape=jax.ShapeDtypeStruct((DEPTH, 8, n), F32),
        grid=(DEPTH, n // ADA_TN),
        in_specs=[
            pl.BlockSpec((8, D_MODEL), lambda l, j: (0, 0)),
            pl.BlockSpec((1, D_MODEL, ADA_TN), lambda l, j: (l, 0, j)),
            pl.BlockSpec((1, 1, ADA_TN), lambda l, j: (l, 0, j)),
        ],
        out_specs=pl.BlockSpec((1, 8, ADA_TN), lambda l, j: (l, 0, j)),
        compiler_params=pltpu.CompilerParams(
            dimension_semantics=("arbitrary", "arbitrary"),
            vmem_limit_bytes=_vmem_limit(2 * D_MODEL * ADA_TN * 4)),
        name="ada_mod",
    )(cvec, w_ada, b_ada.reshape(DEPTH, 1, n))


def _rope_swap(t):
    lane = lax.broadcasted_iota(jnp.int32, t.shape, 1)
    first_half = (lane % 32) < 16
    return jnp.where(first_half, pltpu.roll(t, LANES - 16, 1), pltpu.roll(t, 16, 1))


def _pick_rows(lat_ref, ctx_ref, split_ctx):
    if not split_ctx:
        return lat_ref[...]
    return jnp.where(pl.program_id(0) >= LAT_TILES, ctx_ref[...], lat_ref[...])


def _lat_ctx_specs(width, split_ctx):
    if split_ctx:
        return [pl.BlockSpec((TM, width), lambda i, *_: (jnp.minimum(i, LAT_TILES - 1), 0)),
                pl.BlockSpec((TM, width), lambda i, *_: (jnp.maximum(i - LAT_TILES, 0), 0))]
    return [pl.BlockSpec((TM, width), lambda i, *_: (i, 0)),
            pl.BlockSpec((TM, width), lambda i, *_: (0, 0))]


def _inproj_kernel(xl_ref, xc_ref, mod_ref, g1_ref, w_ref, cs_ref, sn_ref, qg_ref, kg_ref,
                   aq_ref, ak_ref, av_ref, nq_ref, nk_ref, nv_ref,
                   rq_ref, rk_ref, rv_ref, rg_ref, *, split_ctx):
    x = _pick_rows(xl_ref, xc_ref, split_ctx)
    mod = mod_ref[0]
    sh1 = mod[:, 0:D_MODEL]
    sc1 = mod[:, D_MODEL:2 * D_MODEL]
    ms = jnp.mean(x * x, axis=-1, keepdims=True)
    h = x * lax.rsqrt(ms + EPS) * g1_ref[...]
    h = (h * (1.0 + sc1) + sh1).astype(BF16)
    cs = cs_ref[...]
    sn = sn_ref[...]

    def proj(name, hidx):
        off, _ = _SEC_OFF[name]
        c0 = off + hidx * HB
        return jnp.dot(h, w_ref[:, c0:c0 + HB], preferred_element_type=F32)

    def normed_rope(z, g):
        ss = jnp.sum(z * z, axis=-1, keepdims=True)
        zn = z * lax.rsqrt(ss * (1.0 / HEAD_DIM) + EPS) * g
        return zn * cs + _rope_swap(zn) * sn

    scale = HEAD_DIM ** -0.5
    for hh in range(N_ATT_HEADS):
        z = normed_rope(proj("aq", hh), qg_ref[...]) * (scale * LOG2E)
        aq_ref[hh * HB:(hh + 1) * HB, :] = z.T.astype(BF16)
    for hh in range(N_ATT_KV):
        z = normed_rope(proj("ak", hh), kg_ref[...])
        ak_ref[:, hh * HB:(hh + 1) * HB] = z.astype(BF16)
        av_ref[hh * HB:(hh + 1) * HB, :] = proj("av", hh).T.astype(BF16)
    for hh in range(N_NA_HEADS):
        nq_ref[:, hh * HB:(hh + 1) * HB] = (proj("nq", hh) * scale).astype(BF16)
        nk_ref[:, hh * HB:(hh + 1) * HB] = proj("nk", hh).astype(BF16)
        nv_ref[:, hh * HB:(hh + 1) * HB] = proj("nv", hh).astype(BF16)
    for hh in range(N_RET_HEADS):
        rq_ref[:, hh * HB:(hh + 1) * HB] = proj("rq", hh).astype(BF16)
        rk_ref[:, hh * HB:(hh + 1) * HB] = (proj("rk", hh) * scale).astype(BF16)
        rv_ref[:, hh * HB:(hh + 1) * HB] = proj("rv", hh).astype(BF16)
        rg_ref[:, hh * HB:(hh + 1) * HB] = proj("rg", hh).astype(BF16)


def _inproj(x_lat, x_ctx, split_ctx, mod3, g1, w_pad, cs_tab, sn_tab, qg, kg):
    names = [s[0] for s in _SECTIONS]
    widths = [_SEC_OFF[n][1] for n in names]

    def tab_map(i):
        return (jnp.where(i < LAT_TILES, i % LAT_TILES_PER_BATCH, LAT_TILES_PER_BATCH), 0)

    est = (2 * D_MODEL * NC_PAD * 2 + 2 * TM * D_MODEL * 4 + 2 * TM * NC_PAD * 2
           + 4 * TM * D_MODEL * 4)
    return pl.pallas_call(
        functools.partial(_inproj_kernel, split_ctx=split_ctx),
        out_shape=[jax.ShapeDtypeStruct((w, T_ALL) if n in _FEATURE_MAJOR else (T_ALL, w), BF16)
                   for n, w in zip(names, widths)],
        grid=(ALL_TILES,),
        in_specs=_lat_ctx_specs(D_MODEL, split_ctx) + [
            pl.BlockSpec((1, 1, 6 * D_MODEL), lambda i: (_tile_mod_row(i), 0, 0)),
            pl.BlockSpec((1, D_MODEL), lambda i: (0, 0)),
            pl.BlockSpec((D_MODEL, NC_PAD), lambda i: (0, 0)),
            pl.BlockSpec((TM, HB), tab_map),
            pl.BlockSpec((TM, HB), tab_map),
            pl.BlockSpec((1, HB), lambda i: (0, 0)),
            pl.BlockSpec((1, HB), lambda i: (0, 0)),
        ],
        out_specs=[pl.BlockSpec((w, TM), lambda i: (0, i)) if n in _FEATURE_MAJOR
                   else pl.BlockSpec((TM, w), lambda i: (i, 0)) for n, w in zip(names, widths)],
        compiler_params=pltpu.CompilerParams(
            dimension_semantics=("arbitrary",), vmem_limit_bytes=_vmem_limit(est)),
        name="norm_inproj",
    )(x_lat, x_ctx, mod3, g1, w_pad, cs_tab, sn_tab, qg, kg)


ATT_TK = 1024
ATT_NK = SEQ // ATT_TK
ATT_CK = 256


def _attn_kernel(qt_ref, k_ref, vt_ref, kc_ref, vct_ref, g_ref, o_ref, m_sc, l_sc, acc_sc):
    i = pl.program_id(2)
    j = pl.program_id(3)
    is_ctx_q = i >= LAT_TILES_PER_BATCH

    @pl.when(j == 0)
    def _():
        m_sc[...] = jnp.full(m_sc.shape, -jnp.inf, F32)
        l_sc[...] = jnp.zeros(l_sc.shape, F32)
        acc_sc[...] = jnp.zeros(acc_sc.shape, F32)

    def scores(kr, c):
        k = kr[c * ATT_CK:(c + 1) * ATT_CK, :]
        return [jnp.dot(k, qt_ref[hh * HB:(hh + 1) * HB, :], preferred_element_type=F32)
                for hh in range(ATT_GRP)]

    def step(kr, vtr, nkeys):
        nchunk = nkeys // ATT_CK
        ss = scores(kr, 0)
        for c in range(nchunk):
            cur = ss
            if c + 1 < nchunk:
                ss = scores(kr, c + 1)
            vt = vtr[0:HEAD_DIM, c * ATT_CK:(c + 1) * ATT_CK]
            ps, alphas = [], []
            for hh in range(ATT_GRP):
                m_prev = m_sc[hh]
                m_new = jnp.maximum(m_prev, jnp.max(cur[hh], axis=0, keepdims=True))
                alpha = jnp.exp2(m_prev - m_new)
                p = jnp.exp2(cur[hh] - m_new)
                l_sc[hh] = alpha * l_sc[hh] + jnp.sum(p, axis=0, keepdims=True)
                m_sc[hh] = m_new
                ps.append(p.astype(BF16))
                alphas.append(alpha)
            for hh in range(ATT_GRP):
                acc_sc[hh] = alphas[hh] * acc_sc[hh] + jnp.dot(vt, ps[hh],
                                                               preferred_element_type=F32)

    @pl.when(jnp.logical_not(is_ctx_q))
    def _():
        step(k_ref, vt_ref, ATT_TK)

    @pl.when(j == ATT_NK - 1)
    def _():
        step(kc_ref, vct_ref, CTX_LEN)
        for hh in range(ATT_GRP):
            o = acc_sc[hh] / l_sc[hh]
            ms = jnp.sum(o * o, axis=0, keepdims=True) * (1.0 / HEAD_DIM)
            g = g_ref[hh * HB:hh * HB + HEAD_DIM, :]
            y = o * lax.rsqrt(ms + EPS) * jnp.concatenate([g] * (TM // LANES), axis=1)
            y = jnp.concatenate([y, jnp.zeros_like(y)], axis=0)
            o_ref[:, hh * HB:(hh + 1) * HB] = y.T.astype(BF16)


def _gqa(aqt, ak, avt, ga_cols, want_ctx):
    nq = LAT_TILES_PER_BATCH + (1 if want_ctx else 0)

    def q_tile(b, i):
        return jnp.where(i < LAT_TILES_PER_BATCH, b * LAT_TILES_PER_BATCH + i, LAT_TILES + b)

    est = (2 * ATT_GRP * HB * TM * 2 + 4 * ATT_TK * HB * 2 + 4 * TM * HB * 2
           + ATT_GRP * TM * HB * 4 * 3 + 8 * TM * ATT_TK * 4)
    return pl.pallas_call(
        _attn_kernel,
        out_shape=jax.ShapeDtypeStruct((T_ALL, N_ATT_HEADS * HB), BF16),
        grid=(BATCH, N_ATT_KV, nq, ATT_NK),
        in_specs=[
            pl.BlockSpec((ATT_GRP * HB, TM), lambda b, c, i, j: (c, q_tile(b, i))),
            pl.BlockSpec((ATT_TK, HB), lambda b, c, i, j: (b * ATT_NK + j, c)),
            pl.BlockSpec((HB, ATT_TK), lambda b, c, i, j: (c, b * ATT_NK + j)),
            pl.BlockSpec((CTX_LEN, HB), lambda b, c, i, j: (LAT_TILES + b, c)),
            pl.BlockSpec((HB, CTX_LEN), lambda b, c, i, j: (c, LAT_TILES + b)),
            pl.BlockSpec((ATT_GRP * HB, LANES), lambda b, c, i, j: (c, 0)),
        ],
        out_specs=pl.BlockSpec((TM, ATT_GRP * HB), lambda b, c, i, j: (q_tile(b, i), c)),
        scratch_shapes=[
            pltpu.VMEM((ATT_GRP, 1, TM), F32),
            pltpu.VMEM((ATT_GRP, 1, TM), F32),
            pltpu.VMEM((ATT_GRP, HEAD_DIM, TM), F32),
        ],
        compiler_params=pltpu.CompilerParams(
            dimension_semantics=("arbitrary",) * 4, vmem_limit_bytes=_vmem_limit(est)),
        name="gqa_attn",
    )(aqt, ak, avt, ak, avt, ga_cols)


NA_BAND = NA_WIN_ROWS * GRID_W
NA_CLASSES = 8
NA_ROWS_PER_ITER = 4
_NA_CLASS_ROWS = (0, 1, 2, 3, GRID_ROWS // 2, GRID_ROWS - 3, GRID_ROWS - 2, GRID_ROWS - 1)


def _na_bias_table(rpb):
    wr, wc = NA_WIN_ROWS, NA_WIN_COLS
    r = np.asarray(_NA_CLASS_ROWS)
    ridx = np.clip(r - wr // 2, 0, GRID_ROWS - wr)[:, None] + np.arange(wr)[None, :]
    dr = ridx - r[:, None] + (wr - 1)
    col = np.arange(GRID_W)
    cstart = np.clip(col - wc // 2, 0, GRID_W - wc)
    col_ok = (col[None, :] >= cstart[:, None]) & (col[None, :] < cstart[:, None] + wc)
    dc = np.clip(col[None, :] - col[:, None] + (wc - 1), 0, 2 * wc - 2)
    bias = rpb[:, dr][..., dc]
    bias = bias.transpose(0, 1, 3, 2, 4).reshape(N_NA_HEADS, NA_CLASSES, GRID_W, NA_BAND)
    mask = np.tile(col_ok, (1, wr))
    return jnp.where(mask[None, None], bias.astype(F32), NEG_INF)


def _head_rms_gain(o, g):
    ms = jnp.sum(o * o, axis=-1, keepdims=True) * (1.0 / HEAD_DIM)
    return o * lax.rsqrt(ms + EPS) * g


def _na_kernel(q_ref, k_ref, v_ref, kc_ref, vc_ref, qc_ref, bias_ref, g_ref, o_ref, oc_ref,
               *, want_ctx):
    kc = kc_ref[...]
    vc = vc_ref[...]
    g = g_ref[0]
    half = NA_WIN_ROWS // 2
    last = GRID_ROWS - NA_WIN_ROWS

    def rows(it, carry):
        r0 = it * NA_ROWS_PER_ITER
        qrows, bands, scores = [], [], []
        for d in range(NA_ROWS_PER_ITER):
            r = r0 + d
            start = jnp.clip(r - half, 0, last)
            cls = jnp.where(r < half, r, jnp.where(r > last + half, r - last, half))
            qrow = pl.ds(pl.multiple_of(r * GRID_W, GRID_W), GRID_W)
            band = pl.ds(pl.multiple_of(start * GRID_W, GRID_W), NA_BAND)
            q = q_ref[qrow, :]
            s = lax.dot_general(q, k_ref[band, :], (((1,), (1,)), ((), ())),
                                preferred_element_type=F32)
            sc = lax.dot_general(q, kc, (((1,), (1,)), ((), ())), preferred_element_type=F32)
            qrows.append(qrow)
            bands.append(band)
            scores.append((s, sc, cls))
        probs = []
        for s, sc, cls in scores:
            bt = bias_ref[0, cls]
            s = jnp.where(bt > 0.5 * NEG_INF, s + bt, NEG_INF)
            m = jnp.maximum(jnp.max(s, axis=-1, keepdims=True),
                            jnp.max(sc, axis=-1, keepdims=True))
            p = jnp.exp(s - m)
            pc = jnp.exp(sc - m)
            l = jnp.sum(p, axis=-1, keepdims=True) + jnp.sum(pc, axis=-1, keepdims=True)
            probs.append((p.astype(BF16), pc.astype(BF16), l))
        for qrow, band, (p, pc, l) in zip(qrows, bands, probs):
            o = (jnp.dot(p, v_ref[band, :], preferred_element_type=F32)
                 + jnp.dot(pc, vc, preferred_element_type=F32)) / l
            o_ref[qrow, :] = _head_rms_gain(o, g).astype(BF16)
        return carry

    lax.fori_loop(0, GRID_ROWS // NA_ROWS_PER_ITER, rows, 0)

    if want_ctx:
        sc = lax.dot_general(qc_ref[...], kc, (((1,), (1,)), ((), ())),
                             preferred_element_type=F32)
        m = jnp.max(sc, axis=-1, keepdims=True)
        pc = jnp.exp(sc - m)
        l = jnp.sum(pc, axis=-1, keepdims=True)
        o = jnp.dot(pc.astype(BF16), vc, preferred_element_type=F32) / l
        oc_ref[...] = _head_rms_gain(o, g).astype(BF16)
    else:
        oc_ref[...] = jnp.zeros(oc_ref.shape, oc_ref.dtype)


def _neigh(nq, nk, nv, bias_tab, gn, want_ctx):
    lat = pl.BlockSpec((SEQ, HB), lambda b, h: (b, h))
    ctx = pl.BlockSpec((CTX_LEN, HB), lambda b, h: (T_LAT // CTX_LEN + b, h))
    est = 2 * (4 * SEQ * HB * 2 + 4 * CTX_LEN * HB * 2 + NA_CLASSES * GRID_W * NA_BAND * 4)
    return pl.pallas_call(
        functools.partial(_na_kernel, want_ctx=want_ctx),
        out_shape=[jax.ShapeDtypeStruct((T_LAT, N_NA_HEADS * HB), BF16),
                   jax.ShapeDtypeStruct((T_CTX, N_NA_HEADS * HB), BF16)],
        grid=(BATCH, N_NA_HEADS),
        in_specs=[lat, lat, lat, ctx, ctx, ctx,
                  pl.BlockSpec((1, NA_CLASSES, GRID_W, NA_BAND), lambda b, h: (h, 0, 0, 0)),
                  pl.BlockSpec((1, 1, HB), lambda b, h: (h, 0, 0))],
        out_specs=[pl.BlockSpec((SEQ, HB), lambda b, h: (b, h)),
                   pl.BlockSpec((CTX_LEN, HB), lambda b, h: (b, h))],
        compiler_params=pltpu.CompilerParams(
            dimension_semantics=("arbitrary", "arbitrary"), vmem_limit_bytes=_vmem_limit(est)),
        name="neigh_attn",
    )(nq, nk, nv, nk, nv, nq, bias_tab, gn)


RET_NCHUNK = SEQ // RET_CHUNK
RET_NCHUNK_CTX = CTX_LEN // RET_CHUNK
RET_CHUNKS_PER_ITER = 4


def _ret_tables(log_g2):
    lf = log_g2[0][:, None, None]
    lb = log_g2[1][:, None, None]
    pos = jnp.arange(RET_CHUNK, dtype=F32)
    i = pos[None, :, None]
    j = pos[None, None, :]
    diff = i - j
    dm = jnp.where(diff > 0, jnp.exp(lf * jnp.maximum(diff, 0.0)),
                   jnp.where(diff < 0, jnp.exp(lb * jnp.maximum(-diff, 0.0)), 2.0)) * 0.5
    fwd_lane = (jnp.arange(LANES) < HEAD_DIM)[None, None, :]
    xi = jnp.where(fwd_lane, jnp.exp(lf * (i + 1.0)), jnp.exp(lb * (RET_CHUNK - i)))
    zt = jnp.where(fwd_lane, jnp.exp(lf * (RET_CHUNK - 1.0 - i)), jnp.exp(lb * i))
    fwd_row = (jnp.arange(LANES) < HEAD_DIM)[None, :, None]
    dec = jnp.where(fwd_row, jnp.exp(lf * RET_CHUNK), jnp.exp(lb * RET_CHUNK))
    dec = jnp.broadcast_to(dec, (N_RET_HEADS, LANES, LANES))
    return dm.astype(F32), xi.astype(F32), zt.astype(F32), dec.astype(F32)


def _ret_kernel(q_ref, k_ref, v_ref, gt_ref, qc_ref, kc_ref, vc_ref, gtc_ref,
                dm_ref, xi_ref, zt_ref, dec_ref, g_ref, o_ref, oc_ref,
                u_sc, s_sc, uc_sc, sc_sc, *, want_ctx):
    dm = dm_ref[0]
    xi = xi_ref[0]
    zt = zt_ref[0]
    dec = dec_ref[0]
    dec_f = dec[0:HEAD_DIM]
    dec_b = dec[HEAD_DIM:LANES]
    g = g_ref[0]
    C = RET_CHUNK

    def chunk_rows(n):
        return pl.ds(pl.multiple_of(n * C, C), C)

    def chunk_state_update(kr, vr, usc, ns):
        kzs = [(kr[chunk_rows(n), :].astype(F32) * zt).T.astype(BF16) for n in ns]
        for n, kz in zip(ns, kzs):
            usc[n] = jnp.dot(kz, vr[chunk_rows(n), :], preferred_element_type=F32)

    def chunk_out(qr, kr, vr, gtr, ssc, outr, ns):
        qds = [qr[chunk_rows(n), :] for n in ns]
        s2s = [lax.dot_general(qd, kr[chunk_rows(n), :], (((1,), (1,)), ((), ())),
                               preferred_element_type=F32) for n, qd in zip(ns, qds)]
        outs = []
        for n, qd, s2 in zip(ns, qds, s2s):
            inner = jnp.dot((s2 * dm).astype(BF16), vr[chunk_rows(n), :],
                            preferred_element_type=F32)
            qx = (qd.astype(F32) * xi).astype(BF16)
            outs.append(inner + jnp.dot(qx, ssc[n].astype(BF16), preferred_element_type=F32))
        for n, o in zip(ns, outs):
            gate = gtr[chunk_rows(n), :].astype(F32)
            outr[chunk_rows(n), :] = (_head_rms_gain(o, g)
                                      * (gate * jax.nn.sigmoid(gate))).astype(BF16)

    def scan_states(usc, ssc, nchunk, init_f, init_b):
        def fwd(n, sf):
            ssc[n, 0:HEAD_DIM, :] = sf
            return dec_f * sf + usc[n, 0:HEAD_DIM, :]

        def bwd(t, sb):
            n = nchunk - 1 - t
            ssc[n, HEAD_DIM:LANES, :] = sb
            return dec_b * sb + usc[n, HEAD_DIM:LANES, :]

        return (lax.fori_loop(0, nchunk, fwd, init_f), lax.fori_loop(0, nchunk, bwd, init_b))

    zero = jnp.zeros((HEAD_DIM, LANES), F32)
    ctx_chunks = list(range(RET_NCHUNK_CTX))
    chunk_state_update(kc_ref, vc_ref, uc_sc, ctx_chunks)
    ctx_f, ctx_b = scan_states(uc_sc, sc_sc, RET_NCHUNK_CTX, zero, zero)
    if want_ctx:
        chunk_out(qc_ref, kc_ref, vc_ref, gtc_ref, sc_sc, oc_ref, ctx_chunks)
    else:
        oc_ref[...] = jnp.zeros(oc_ref.shape, oc_ref.dtype)

    def upd(it, carry):
        chunk_state_update(k_ref, v_ref, u_sc,
                           [it * RET_CHUNKS_PER_ITER + d for d in range(RET_CHUNKS_PER_ITER)])
        return carry

    lax.fori_loop(0, RET_NCHUNK // RET_CHUNKS_PER_ITER, upd, 0)
    scan_states(u_sc, s_sc, RET_NCHUNK, ctx_f, ctx_b)

    def out(it, carry):
        chunk_out(q_ref, k_ref, v_ref, gt_ref, s_sc, o_ref,
                  [it * RET_CHUNKS_PER_ITER + d for d in range(RET_CHUNKS_PER_ITER)])
        return carry

    lax.fori_loop(0, RET_NCHUNK // RET_CHUNKS_PER_ITER, out, 0)


def _retention(rq, rk, rv, rg, tables, gr, want_ctx):
    lat = pl.BlockSpec((SEQ, HB), lambda b, h: (b, h))
    ctx = pl.BlockSpec((CTX_LEN, HB), lambda b, h: (T_LAT // CTX_LEN + b, h))
    tab = pl.BlockSpec((1, LANES, LANES), lambda b, h: (h, 0, 0))
    est = (2 * 5 * SEQ * HB * 2 + 2 * RET_NCHUNK * LANES * LANES * 4 + 8 * LANES * LANES * 4)
    return pl.pallas_call(
        functools.partial(_ret_kernel, want_ctx=want_ctx),
        out_shape=[jax.ShapeDtypeStruct((T_LAT, N_RET_HEADS * HB), BF16),
                   jax.ShapeDtypeStruct((T_CTX, N_RET_HEADS * HB), BF16)],
        grid=(BATCH, N_RET_HEADS),
        in_specs=[lat, lat, lat, lat, ctx, ctx, ctx, ctx, tab, tab, tab, tab,
                  pl.BlockSpec((1, 1, HB), lambda b, h: (h, 0, 0))],
        out_specs=[pl.BlockSpec((SEQ, HB), lambda b, h: (b, h)),
                   pl.BlockSpec((CTX_LEN, HB), lambda b, h: (b, h))],
        scratch_shapes=[
            pltpu.VMEM((RET_NCHUNK, LANES, LANES), F32),
            pltpu.VMEM((RET_NCHUNK, LANES, LANES), F32),
            pltpu.VMEM((RET_NCHUNK_CTX, LANES, LANES), F32),
            pltpu.VMEM((RET_NCHUNK_CTX, LANES, LANES), F32),
        ],
        compiler_params=pltpu.CompilerParams(
            dimension_semantics=("arbitrary", "arbitrary"), vmem_limit_bytes=_vmem_limit(est)),
        name="retention",
    )(rq, rk, rv, rg, rq, rk, rv, rg, *tables, gr)


_YA_W = N_ATT_HEADS * HB
_YN_W = N_NA_HEADS * HB
_YR_W = N_RET_HEADS * HB


MOE_SLOT = TM
META_ROWS = 8
_META_GATE0 = 2


def _route_t(logt):
    row = lax.broadcasted_iota(jnp.int32, logt.shape, 0).astype(F32)
    p = jnp.exp(logt - jnp.max(logt, axis=0, keepdims=True))
    best = None
    for grp in range(N_EXPERT_GROUPS):
        lo = float(grp * EXPERTS_PER_GROUP)
        ing = (row >= lo) & (row < lo + EXPERTS_PER_GROUP)
        pg = jnp.where(ing, p, -1.0)
        m1 = jnp.max(pg, axis=0, keepdims=True)
        i1 = jnp.min(jnp.where(pg == m1, row, float(N_EXPERTS)), axis=0, keepdims=True)
        pg2 = jnp.where(row == i1, -1.0, pg)
        m2 = jnp.max(pg2, axis=0, keepdims=True)
        i2 = jnp.min(jnp.where(pg2 == m2, row, float(N_EXPERTS)), axis=0, keepdims=True)
        cand = (m1 + m2, m1, m2, i1, i2, jnp.zeros_like(m1) + lo)
        if best is None:
            best = cand
        else:
            better = cand[0] > best[0]
            best = tuple(jnp.where(better, c, b) for c, b in zip(cand, best))
    _, m1, m2, i1, i2, base = best
    w = m1 + m2
    gates = [jnp.where(i1 == base + e, m1 / w, jnp.where(i2 == base + e, m2 / w, 0.0))
             for e in range(EXPERTS_PER_GROUP)]
    return base * (1.0 / EXPERTS_PER_GROUP), gates


def _merge_kernel(ya_ref, ynl_ref, ync_ref, yrl_ref, yrc_ref, xl_ref, xc_ref, mod_ref, g2_ref,
                  wo_ref, wrt_ref, brt_ref,
                  xn_ref, h2_ref, metat_ref, metac_ref, ctab_ref, tot_ref, carry_sc, *, split_ctx):
    @pl.when(pl.program_id(0) == 0)
    def _():
        carry_sc[...] = jnp.zeros(carry_sc.shape, F32)

    mod = mod_ref[0]
    gt1 = mod[:, 2 * D_MODEL:3 * D_MODEL]
    sh2 = mod[:, 3 * D_MODEL:4 * D_MODEL]
    sc2 = mod[:, 4 * D_MODEL:5 * D_MODEL]
    yn = _pick_rows(ynl_ref, ync_ref, split_ctx)
    yr = _pick_rows(yrl_ref, yrc_ref, split_ctx)
    m = (jnp.dot(ya_ref[...], wo_ref[0:_YA_W, :], preferred_element_type=F32)
         + jnp.dot(yn, wo_ref[_YA_W:_YA_W + _YN_W, :], preferred_element_type=F32)
         + jnp.dot(yr, wo_ref[_YA_W + _YN_W:MIX_PAD, :], preferred_element_type=F32))
    x = _pick_rows(xl_ref, xc_ref, split_ctx) + gt1 * m
    xn_ref[...] = x
    ms = jnp.mean(x * x, axis=-1, keepdims=True)
    h2 = x * lax.rsqrt(ms + EPS) * g2_ref[...] * (1.0 + sc2) + sh2
    h2_ref[...] = h2.astype(BF16)

    ntile = TM // LANES
    logt = lax.dot_general(wrt_ref[...], h2, (((1,), (1,)), ((), ())), preferred_element_type=F32,
                           precision=lax.Precision.HIGHEST)
    gsel, gates = _route_t(logt + jnp.concatenate([brt_ref[...]] * ntile, axis=1))

    grow = lax.broadcasted_iota(jnp.int32, (META_ROWS, TM), 0).astype(F32)
    onehot = jnp.where(grow == gsel, 1.0, 0.0)
    earlier = (lax.broadcasted_iota(jnp.int32, (TM, TM), 0)
               < lax.broadcasted_iota(jnp.int32, (TM, TM), 1))
    excl = jnp.dot(onehot.astype(BF16), jnp.where(earlier, 1.0, 0.0).astype(BF16),
                   preferred_element_type=F32)
    carry = carry_sc[...]
    rank = jnp.sum(onehot * (jnp.concatenate([carry] * ntile, axis=1) + excl),
                   axis=0, keepdims=True)
    ctab_ref[0] = carry
    carry = carry + jnp.sum(onehot, axis=1, keepdims=True)
    carry_sc[...] = carry
    tot_ref[...] = carry

    metat = jnp.concatenate([gsel, rank] + gates
                            + [jnp.zeros((META_ROWS - _META_GATE0 - EXPERTS_PER_GROUP, TM), F32)],
                            axis=0)
    metat_ref[...] = metat
    metac_ref[...] = jnp.concatenate([metat, jnp.zeros((LANES - META_ROWS, TM), F32)], axis=0).T


def _merge(ya, yn_lat, yn_ctx, yr_lat, yr_ctx, x_lat, x_ctx, split_ctx, mod3, g2, wo_pad, wrt, brt,
           ntiles):
    rows = ntiles * TM
    est = (2 * MIX_PAD * D_MODEL * 2 + 4 * TM * MIX_PAD * 2 + 10 * TM * D_MODEL * 4
           + 4 * TM * TM * 4)
    const = lambda i: (0, 0)
    return pl.pallas_call(
        functools.partial(_merge_kernel, split_ctx=split_ctx),
        out_shape=[jax.ShapeDtypeStruct((rows, D_MODEL), F32),
                   jax.ShapeDtypeStruct((rows, D_MODEL), BF16),
                   jax.ShapeDtypeStruct((META_ROWS, rows), F32),
                   jax.ShapeDtypeStruct((rows, LANES), F32),
                   jax.ShapeDtypeStruct((ntiles, META_ROWS, LANES), F32),
                   jax.ShapeDtypeStruct((META_ROWS, LANES), F32)],
        grid=(ntiles,),
        in_specs=([pl.BlockSpec((TM, _YA_W), lambda i: (i, 0))]
                  + _lat_ctx_specs(_YN_W, split_ctx) + _lat_ctx_specs(_YR_W, split_ctx)
                  + _lat_ctx_specs(D_MODEL, split_ctx) + [
            pl.BlockSpec((1, 1, 6 * D_MODEL), lambda i: (_tile_mod_row(i), 0, 0)),
            pl.BlockSpec((1, D_MODEL), const),
            pl.BlockSpec((MIX_PAD, D_MODEL), const),
            pl.BlockSpec((N_EXPERTS, D_MODEL), const),
            pl.BlockSpec((N_EXPERTS, LANES), const),
        ]),
        out_specs=[pl.BlockSpec((TM, D_MODEL), lambda i: (i, 0)),
                   pl.BlockSpec((TM, D_MODEL), lambda i: (i, 0)),
                   pl.BlockSpec((META_ROWS, TM), lambda i: (0, i)),
                   pl.BlockSpec((TM, LANES), lambda i: (i, 0)),
                   pl.BlockSpec((1, META_ROWS, LANES), lambda i: (i, 0, 0)),
                   pl.BlockSpec((META_ROWS, LANES), const)],
        scratch_shapes=[pltpu.VMEM((META_ROWS, LANES), F32)],
        compiler_params=pltpu.CompilerParams(
            dimension_semantics=("arbitrary",), vmem_limit_bytes=_vmem_limit(est)),
        name="merge_outproj_router",
    )(ya, yn_lat, yn_ctx, yr_lat, yr_ctx, x_lat, x_ctx, mod3, g2, wo_pad, wrt, brt)


def _moe_plan(ctab, tot, ntiles):
    grp = N_EXPERT_GROUPS
    i32 = jnp.int32
    a = ctab[:, :grp, 0].astype(i32).T
    totg = tot[:grp, 0].astype(i32)
    b = jnp.concatenate([a[:, 1:], totg[:, None]], axis=1)
    nslot = (totg + MOE_SLOT - 1) // MOE_SLOT
    slot_end = jnp.cumsum(nslot)
    slot_base = slot_end - nslot
    total_slots = slot_end[-1]
    first_j = a // MOE_SLOT
    last_j = (jnp.maximum(b, 1) - 1) // MOE_SLOT
    npairs = jnp.where(b > a, last_j - first_j + 1, 0).reshape(-1)
    cum = jnp.cumsum(npairs)
    start = cum - npairs
    total_pairs = cum[-1]
    n_pairs_max = (grp + 1) * ntiles + grp
    pidx = jnp.arange(n_pairs_max, dtype=i32)
    p = jnp.minimum(pidx, total_pairs - 1)
    gc = jnp.sum((cum[None, :] <= p[:, None]).astype(i32), axis=1)
    slot = slot_base[gc // ntiles] + first_j.reshape(-1)[gc] + (p - start[gc])
    valid = pidx < total_pairs
    prev_slot = jnp.concatenate([jnp.full((1,), -1, i32), slot[:-1]])
    next_slot = jnp.concatenate([slot[1:], jnp.full((1,), -1, i32)])
    is_first = valid & (slot != prev_slot)
    is_last = valid & ((slot != next_slot) | (pidx == total_pairs - 1))
    flags = is_first.astype(i32) + 2 * is_last.astype(i32) + 4 * valid.astype(i32)
    n_slots_max = ntiles + grp
    sidx = jnp.arange(n_slots_max, dtype=i32)
    sgrp = jnp.minimum(jnp.sum((slot_end[None, :] <= sidx[:, None]).astype(i32), axis=1), grp - 1)
    sr0 = (sidx - slot_base[sgrp]) * MOE_SLOT
    row0 = slot_base[:, None] * MOE_SLOT + a
    row1 = slot_base[:, None] * MOE_SLOT + jnp.maximum(b, a + 1) - 1
    wb = jnp.stack([row0 // MOE_SLOT, row1 // MOE_SLOT], axis=1)
    wb = jnp.clip(wb, 0, total_slots - 1).reshape(-1).astype(i32)
    return dict(pslot=slot.astype(i32), pchunk=(gc % ntiles).astype(i32), pflag=flags,
                sgrp=sgrp, sr0=sr0.astype(i32), wb=wb, sbase=slot_base.astype(i32),
                n_pairs=n_pairs_max, n_slots=n_slots_max)


def _moe_kernel(pslot_ref, pchunk_ref, pflag_ref, sgrp_ref, sr0_ref,
                h_ref, metat_ref, metac_ref, w1_ref, w3_ref, w2_ref, o_ref, x_sc, g_sc):
    p = pl.program_id(0)
    flags = pflag_ref[p]
    slot = pslot_ref[p]

    @pl.when((flags & 1) != 0)
    def _():
        x_sc[...] = jnp.zeros(x_sc.shape, F32)
        g_sc[...] = jnp.zeros(g_sc.shape, F32)

    @pl.when((flags & 4) != 0)
    def _():
        mt = metat_ref[...].astype(jnp.int32)
        want = lax.broadcasted_iota(jnp.int32, (MOE_SLOT, TM), 0) + sr0_ref[slot]
        sel = (mt[1:2, :] == want) & (mt[0:1, :] == sgrp_ref[slot])
        pm = jnp.where(sel, 1.0, 0.0).astype(BF16)
        x_sc[...] += jnp.dot(pm, h_ref[...], preferred_element_type=F32)
        mc = metac_ref[...]
        hi = mc.astype(BF16)
        r1 = mc - hi.astype(F32)
        mid = r1.astype(BF16)
        lo = (r1 - mid.astype(F32)).astype(BF16)
        g_sc[...] += (jnp.dot(pm, hi, preferred_element_type=F32)
                      + jnp.dot(pm, mid, preferred_element_type=F32)
                      + jnp.dot(pm, lo, preferred_element_type=F32))

    @pl.when((flags & 2) != 0)
    def _():
        x = x_sc[...].astype(BF16)
        gs = g_sc[...]
        acc = jnp.zeros((MOE_SLOT, D_MODEL), F32)
        for e in range(EXPERTS_PER_GROUP):
            a = jnp.dot(x, w1_ref[0, e], preferred_element_type=F32)
            b = jnp.dot(x, w3_ref[0, e], preferred_element_type=F32)
            gate = gs[:, _META_GATE0 + e:_META_GATE0 + e + 1]
            act = (a * jax.nn.sigmoid(a)) * b * gate
            acc = acc + jnp.dot(act.astype(BF16), w2_ref[0, e], preferred_element_type=F32)
        o_ref[...] = acc.astype(BF16)


def _moe_sorted(plan, h2, metat, metac, w1g, w3g, w2g):
    est = (2 * 3 * EXPERTS_PER_GROUP * D_MODEL * D_EXPERT * 2 + 6 * TM * D_MODEL * 2
           + 2 * MOE_SLOT * D_MODEL * 4 + 6 * MOE_SLOT * D_EXPERT * 4 + 4 * TM * MOE_SLOT * 4)
    wmap = lambda p, ps, pc, pf, sg, sr: (sg[ps[p]], 0, 0, 0)
    grid_spec = pltpu.PrefetchScalarGridSpec(
        num_scalar_prefetch=5,
        grid=(plan["n_pairs"],),
        in_specs=[
            pl.BlockSpec((TM, D_MODEL), lambda p, ps, pc, pf, sg, sr: (pc[p], 0)),
            pl.BlockSpec((META_ROWS, TM), lambda p, ps, pc, pf, sg, sr: (0, pc[p])),
            pl.BlockSpec((TM, LANES), lambda p, ps, pc, pf, sg, sr: (pc[p], 0)),
            pl.BlockSpec((1, EXPERTS_PER_GROUP, D_MODEL, D_EXPERT), wmap),
            pl.BlockSpec((1, EXPERTS_PER_GROUP, D_MODEL, D_EXPERT), wmap),
            pl.BlockSpec((1, EXPERTS_PER_GROUP, D_EXPERT, D_MODEL), wmap),
        ],
        out_specs=pl.BlockSpec((MOE_SLOT, D_MODEL), lambda p, ps, pc, pf, sg, sr: (ps[p], 0)),
        scratch_shapes=[pltpu.VMEM((MOE_SLOT, D_MODEL), F32), pltpu.VMEM((MOE_SLOT, LANES), F32)],
    )
    return pl.pallas_call(
        _moe_kernel,
        out_shape=jax.ShapeDtypeStruct((plan["n_slots"] * MOE_SLOT, D_MODEL), BF16),
        grid_spec=grid_spec,
        compiler_params=pltpu.CompilerParams(
            dimension_semantics=("arbitrary",), vmem_limit_bytes=_vmem_limit(est)),
        name="moe_sorted_experts",
    )(plan["pslot"], plan["pchunk"], plan["pflag"], plan["sgrp"], plan["sr0"],
      h2, metat, metac, w1g, w3g, w2g)


_COMBINE_WINDOWS = 2 * N_EXPERT_GROUPS


def _combine_kernel(wb_ref, sb_ref, *refs, ntiles, final):
    y_refs = refs[:_COMBINE_WINDOWS]
    metac_ref, xn_ref, mod_ref, fg_ref, o_ref = refs[_COMBINE_WINDOWS:]
    i = pl.program_id(0)
    mc = metac_ref[...]
    grp = mc[:, 0:1].astype(jnp.int32)
    rank = mc[:, 1:2].astype(jnp.int32)
    col = lax.broadcasted_iota(jnp.int32, (TM, MOE_SLOT), 1)
    m = jnp.zeros((TM, D_MODEL), F32)
    for g in range(N_EXPERT_GROUPS):
        row = rank + sb_ref[g] * MOE_SLOT
        blk0 = wb_ref[(2 * g) * ntiles + i]
        for k in range(2):
            blk = wb_ref[(2 * g + k) * ntiles + i]
            sel = (grp == g) & (row == col + blk * MOE_SLOT)
            if k == 1:
                sel = jnp.logical_and(sel, blk != blk0)
            m = m + jnp.dot(jnp.where(sel, 1.0, 0.0).astype(BF16), y_refs[2 * g + k][...],
                            preferred_element_type=F32)
    gt2 = mod_ref[0][:, 5 * D_MODEL:6 * D_MODEL]
    x = xn_ref[...] + gt2 * m
    if final:
        ms = jnp.mean(x * x, axis=-1, keepdims=True)
        x = x * lax.rsqrt(ms + EPS) * fg_ref[...]
    o_ref[...] = x


def _combine(plan, ys, metac, xn, mod3, fg, ntiles, final):
    def ymap(w):
        return lambda i, wb, sb: (wb[w * ntiles + i], 0)

    est = (2 * _COMBINE_WINDOWS * MOE_SLOT * D_MODEL * 2 + 8 * TM * D_MODEL * 4
           + 4 * TM * MOE_SLOT * 4)
    grid_spec = pltpu.PrefetchScalarGridSpec(
        num_scalar_prefetch=2,
        grid=(ntiles,),
        in_specs=[pl.BlockSpec((MOE_SLOT, D_MODEL), ymap(w)) for w in range(_COMBINE_WINDOWS)] + [
            pl.BlockSpec((TM, LANES), lambda i, wb, sb: (i, 0)),
            pl.BlockSpec((TM, D_MODEL), lambda i, wb, sb: (i, 0)),
            pl.BlockSpec((1, 1, 6 * D_MODEL), lambda i, wb, sb: (_tile_mod_row(i), 0, 0)),
            pl.BlockSpec((1, D_MODEL), lambda i, wb, sb: (0, 0)),
        ],
        out_specs=pl.BlockSpec((TM, D_MODEL), lambda i, wb, sb: (i, 0)),
    )
    return pl.pallas_call(
        functools.partial(_combine_kernel, ntiles=ntiles, final=final),
        out_shape=jax.ShapeDtypeStruct((ntiles * TM, D_MODEL), F32),
        grid_spec=grid_spec,
        compiler_params=pltpu.CompilerParams(
            dimension_semantics=("arbitrary",), vmem_limit_bytes=_vmem_limit(est)),
        name="moe_combine",
    )(plan["wb"], plan["sbase"], *([ys] * _COMBINE_WINDOWS), metac, xn, mod3, fg)


def _rope_tables():
    t = np.arange(SEQ)
    nf = HEAD_DIM // 4
    inv = (np.float32(ROPE_THETA) ** (-np.arange(nf, dtype=np.float32) / np.float32(nf)))
    inv = inv.astype(np.float32)
    ang_r = (t // GRID_W).astype(np.float32)[:, None] * inv[None, :]
    ang_c = (t % GRID_W).astype(np.float32)[:, None] * inv[None, :]
    cr, sr, cc, sc = np.cos(ang_r), np.sin(ang_r), np.cos(ang_c), np.sin(ang_c)
    zeros = np.zeros((SEQ, HEAD_DIM), np.float32)
    cs = np.concatenate([cr, cr, cc, cc, zeros], axis=-1)
    sn = np.concatenate([-sr, sr, -sc, sc, zeros], axis=-1)
    ident = np.concatenate([np.ones((CTX_LEN, HEAD_DIM), np.float32),
                            np.zeros((CTX_LEN, HEAD_DIM), np.float32)], axis=-1)
    cs = np.concatenate([cs, ident], axis=0).astype(np.float32)
    sn = np.concatenate([sn, np.zeros((CTX_LEN, LANES), np.float32)], axis=0).astype(np.float32)
    return jnp.asarray(cs), jnp.asarray(sn)


def _pad_out_weight(w_out_l):
    w = w_out_l.reshape(-1, HEAD_DIM, D_MODEL)
    w = jnp.concatenate([w, jnp.zeros_like(w)], axis=1)
    return w.reshape(MIX_PAD, D_MODEL).astype(BF16)


def _pad_in_weight(w_in_l):
    parts = []
    for _, src, heads, mode in _SECTIONS:
        w = w_in_l[:, src:src + heads * HEAD_DIM].reshape(D_MODEL, heads, HEAD_DIM)
        other = w if mode == "dup" else jnp.zeros_like(w)
        parts.append(jnp.concatenate([w, other], axis=-1).reshape(D_MODEL, heads * HB))
    return jnp.concatenate(parts, axis=-1).astype(BF16)


def kernel(x, c, ctx, c_ctx, w_ada, b_ada, norm1_g, norm2_g, w_in, q_norm_g, k_norm_g, na_rpb,
           ret_decay, mix_g, w_out, w_router, b_router, w_exp1, w_exp3, w_exp2, final_g):
    cs_tab, sn_tab = _rope_tables()

    cvec = jnp.concatenate([c, c_ctx[None, :], jnp.zeros((8 - BATCH - 1, D_MODEL), F32)], axis=0)
    mod_all = _ada_mod(cvec, w_ada, b_ada)

    wrt = w_router.T
    brt = jnp.broadcast_to(b_router[:, None], (N_EXPERTS, LANES))
    zero_lane = jnp.zeros((HEAD_DIM,), F32)

    x_lat = x.reshape(T_LAT, D_MODEL)
    x_ctx = ctx.reshape(T_CTX, D_MODEL)

    for l in range(DEPTH):
        last = l == DEPTH - 1
        want_ctx = not last
        split_ctx = l == 0
        mod3 = mod_all[l].reshape(8, 1, 6 * D_MODEL)
        w_pad = _pad_in_weight(w_in[l])
        qg = jnp.concatenate([q_norm_g[l], zero_lane])[None, :]
        kg = jnp.concatenate([k_norm_g[l], zero_lane])[None, :]
        aq, ak, av, nq, nk, nv, rq, rk, rv, rg = _inproj(
            x_lat, x_ctx, split_ctx, mod3, norm1_g[l][None, :], w_pad, cs_tab, sn_tab, qg, kg)

        ga = jnp.broadcast_to(mix_g[l][:ATT_Q].reshape(N_ATT_HEADS, HEAD_DIM, 1),
                              (N_ATT_HEADS, HEAD_DIM, LANES))
        ga = jnp.concatenate([ga, jnp.zeros_like(ga)], axis=1).reshape(N_ATT_HEADS * HB, LANES)
        gn = _pad_heads(mix_g[l][ATT_Q:ATT_Q + NA_W], N_NA_HEADS)
        gr = _pad_heads(mix_g[l][ATT_Q + NA_W:], N_RET_HEADS)

        ya = _gqa(aq, ak, av, ga, want_ctx)
        yn_lat, yn_ctx = _neigh(nq, nk, nv, _na_bias_table(na_rpb[l]), gn, want_ctx)
        log_g2 = jax.nn.log_sigmoid(ret_decay[l].astype(F32))
        yr_lat, yr_ctx = _retention(rq, rk, rv, rg, _ret_tables(log_g2), gr, want_ctx)

        wo_pad = _pad_out_weight(w_out[l])
        gshape = (N_EXPERT_GROUPS, EXPERTS_PER_GROUP)
        w1g = w_exp1[l].astype(BF16).reshape(gshape + (D_MODEL, D_EXPERT))
        w3g = w_exp3[l].astype(BF16).reshape(gshape + (D_MODEL, D_EXPERT))
        w2g = w_exp2[l].astype(BF16).reshape(gshape + (D_EXPERT, D_MODEL))
        ntiles = LAT_TILES if last else ALL_TILES
        xn, h2, metat, metac, ctab, tot = _merge(
            ya, yn_lat, yn_ctx, yr_lat, yr_ctx, x_lat, x_ctx, split_ctx, mod3,
            norm2_g[l][None, :], wo_pad, wrt, brt, ntiles)
        plan = _moe_plan(ctab, tot, ntiles)
        ys = _moe_sorted(plan, h2, metat, metac, w1g, w3g, w2g)
        x_lat = _combine(plan, ys, metac, xn, mod3, final_g[None, :], ntiles, last)
        x_ctx = x_lat

    return x_lat.reshape(BATCH, SEQ, D_MODEL)
```

```python
import functools

import numpy as np
import jax
import jax.numpy as jnp
from jax import lax
from jax.experimental import pallas as pl
from jax.experimental.pallas import tpu as pltpu

D_MODEL = 1024
BATCH = 2
SEQ = 8192
DEPTH = 2
GRID_W = 64
GRID_ROWS = SEQ // GRID_W
CTX_LEN = 256
HEAD_DIM = 64
N_ATT_HEADS = 6
N_ATT_KV = 2
ATT_GRP = N_ATT_HEADS // N_ATT_KV
N_NA_HEADS = 4
N_RET_HEADS = 6
ATT_Q = N_ATT_HEADS * HEAD_DIM
ATT_KV = N_ATT_KV * HEAD_DIM
NA_W = N_NA_HEADS * HEAD_DIM
RET_W = N_RET_HEADS * HEAD_DIM
NA_WIN_ROWS = 8
NA_WIN_COLS = 16
RET_CHUNK = 128
ROPE_THETA = 10000.0
N_EXPERTS = 16
N_EXPERT_GROUPS = 4
EXPERTS_PER_GROUP = N_EXPERTS // N_EXPERT_GROUPS
D_EXPERT = 512
EPS = 1e-6
NEG_INF = -1e30

LANES = 128
VMEM_LIMIT_CAP = 56 * 1024 * 1024

T_LAT = BATCH * SEQ
T_CTX = BATCH * CTX_LEN
T_ALL = T_LAT + T_CTX
TM = 256
LAT_TILES_PER_BATCH = SEQ // TM
LAT_TILES = T_LAT // TM
CTX_TILES = T_CTX // TM
ALL_TILES = LAT_TILES + CTX_TILES
HB = LANES

F32 = jnp.float32
BF16 = jnp.bfloat16

_SECTIONS = (
    ("aq", 0, N_ATT_HEADS, "pad"),
    ("ak", ATT_Q, N_ATT_KV, "pad"),
    ("av", ATT_Q + ATT_KV, N_ATT_KV, "pad"),
    ("nq", ATT_Q + 2 * ATT_KV, N_NA_HEADS, "pad"),
    ("nk", ATT_Q + 2 * ATT_KV + NA_W, N_NA_HEADS, "pad"),
    ("nv", ATT_Q + 2 * ATT_KV + 2 * NA_W, N_NA_HEADS, "pad"),
    ("rq", ATT_Q + 2 * ATT_KV + 3 * NA_W, N_RET_HEADS, "dup"),
    ("rk", ATT_Q + 2 * ATT_KV + 3 * NA_W + RET_W, N_RET_HEADS, "dup"),
    ("rv", ATT_Q + 2 * ATT_KV + 3 * NA_W + 2 * RET_W, N_RET_HEADS, "pad"),
    ("rg", ATT_Q + 2 * ATT_KV + 3 * NA_W + 3 * RET_W, N_RET_HEADS, "pad"),
)
_SEC_OFF = {}
_off = 0
for _name, _src, _heads, _mode in _SECTIONS:
    _SEC_OFF[_name] = (_off, _heads * HB)
    _off += _heads * HB
NC_PAD = _off
_FEATURE_MAJOR = ("aq", "av")
LOG2E = 1.4426950408889634
MIX_PAD = (N_ATT_HEADS + N_NA_HEADS + N_RET_HEADS) * HB


def _vmem_limit(nbytes):
    return int(min(VMEM_LIMIT_CAP, max(16 * 1024 * 1024, 2 * nbytes)))


def _pad_heads(v, heads):
    v = v.reshape(heads, 1, HEAD_DIM).astype(F32)
    return jnp.concatenate([v, jnp.zeros_like(v)], axis=-1)


def _tile_mod_row(i):
    return jnp.where(i < LAT_TILES_PER_BATCH, 0, jnp.where(i < LAT_TILES, 1, 2))


ADA_TN = 1536


def _ada_kernel(c_ref, w_ref, b_ref, o_ref):
    c = c_ref[...]
    s = c * jax.nn.sigmoid(c)
    o_ref[0] = jnp.dot(s, w_ref[0], preferred_element_type=F32,
                       precision=lax.Precision.HIGHEST) + b_ref[0]


def _ada_mod(cvec, w_ada, b_ada):
    n = 6 * D_MODEL
    return pl.pallas_call(
        _ada_kernel,
        out_shape=jax.ShapeDtypeStruct((DEPTH, 8, n), F32),
        grid=(DEPTH, n // ADA_TN),
        in_specs=[
            pl.BlockSpec((8, D_MODEL), lambda l, j: (0, 0)),
            pl.BlockSpec((1, D_MODEL, ADA_TN), lambda l, j: (l, 0, j)),
            pl.BlockSpec((1, 1, ADA_TN), lambda l, j: (l, 0, j)),
        ],
        out_specs=pl.BlockSpec((1, 8, ADA_TN), lambda l, j: (l, 0, j)),
        compiler_params=pltpu.CompilerParams(
            dimension_semantics=("arbitrary", "arbitrary"),
            vmem_limit_bytes=_vmem_limit(2 * D_MODEL * ADA_TN * 4)),
        name="ada_mod",
    )(cvec, w_ada, b_ada.reshape(DEPTH, 1, n))


def _rope_swap(t):
    lane = lax.broadcasted_iota(jnp.int32, t.shape, 1)
    first_half = (lane % 32) < 16
    return jnp.where(first_half, pltpu.roll(t, LANES - 16, 1), pltpu.roll(t, 16, 1))


def _pick_rows(lat_ref, ctx_ref, split_ctx):
    if not split_ctx:
        return lat_ref[...]
    return jnp.where(pl.program_id(0) >= LAT_TILES, ctx_ref[...], lat_ref[...])


def _lat_ctx_specs(width, split_ctx):
    if split_ctx:
        return [pl.BlockSpec((TM, width), lambda i, *_: (jnp.minimum(i, LAT_TILES - 1), 0)),
                pl.BlockSpec((TM, width), lambda i, *_: (jnp.maximum(i - LAT_TILES, 0), 0))]
    return [pl.BlockSpec((TM, width), lambda i, *_: (i, 0)),
            pl.BlockSpec((TM, width), lambda i, *_: (0, 0))]


def _inproj_kernel(xl_ref, xc_ref, mod_ref, g1_ref, w_ref, cs_ref, sn_ref, qg_ref, kg_ref,
                   aq_ref, ak_ref, av_ref, nq_ref, nk_ref, nv_ref,
                   rq_ref, rk_ref, rv_ref, rg_ref, *, split_ctx):
    x = _pick_rows(xl_ref, xc_ref, split_ctx)
    mod = mod_ref[0]
    sh1 = mod[:, 0:D_MODEL]
    sc1 = mod[:, D_MODEL:2 * D_MODEL]
    ms = jnp.mean(x * x, axis=-1, keepdims=True)
    h = x * lax.rsqrt(ms + EPS) * g1_ref[...]
    h = (h * (1.0 + sc1) + sh1).astype(BF16)
    cs = cs_ref[...]
    sn = sn_ref[...]

    sections = {}

    def proj(name, hidx):
        if name not in sections:
            off, width = _SEC_OFF[name]
            sections[name] = jnp.dot(h, w_ref[:, off:off + width], preferred_element_type=F32)
        return sections[name][:, hidx * HB:(hidx + 1) * HB]

    def normed_rope(z, g):
        ss = jnp.sum(z * z, axis=-1, keepdims=True)
        zn = z * lax.rsqrt(ss * (1.0 / HEAD_DIM) + EPS) * g
        return zn * cs + _rope_swap(zn) * sn

    scale = HEAD_DIM ** -0.5
    for hh in range(N_ATT_HEADS):
        z = normed_rope(proj("aq", hh), qg_ref[...]) * (scale * LOG2E)
        aq_ref[hh * HB:(hh + 1) * HB, :] = z.T.astype(BF16)
    for hh in range(N_ATT_KV):
        z = normed_rope(proj("ak", hh), kg_ref[...])
        ak_ref[:, hh * HB:(hh + 1) * HB] = z.astype(BF16)
        zv = proj("av", hh)
        lane = lax.broadcasted_iota(jnp.int32, zv.shape, 1)
        av_ref[hh * HB:(hh + 1) * HB, :] = jnp.where(lane == HEAD_DIM, 1.0, zv).T.astype(BF16)
    for hh in range(N_NA_HEADS):
        nq_ref[:, hh * HB:(hh + 1) * HB] = (proj("nq", hh) * scale).astype(BF16)
        nk_ref[:, hh * HB:(hh + 1) * HB] = proj("nk", hh).astype(BF16)
        nv_ref[:, hh * HB:(hh + 1) * HB] = proj("nv", hh).astype(BF16)
    for hh in range(N_RET_HEADS):
        rq_ref[:, hh * HB:(hh + 1) * HB] = proj("rq", hh).astype(BF16)
        rk_ref[:, hh * HB:(hh + 1) * HB] = (proj("rk", hh) * scale).astype(BF16)
        rv_ref[:, hh * HB:(hh + 1) * HB] = proj("rv", hh).astype(BF16)
        rg_ref[:, hh * HB:(hh + 1) * HB] = proj("rg", hh).astype(BF16)


def _inproj(x_lat, x_ctx, split_ctx, mod3, g1, w_pad, cs_tab, sn_tab, qg, kg):
    names = [s[0] for s in _SECTIONS]
    widths = [_SEC_OFF[n][1] for n in names]

    def tab_map(i):
        return (jnp.where(i < LAT_TILES, i % LAT_TILES_PER_BATCH, LAT_TILES_PER_BATCH), 0)

    est = (2 * D_MODEL * NC_PAD * 2 + 2 * TM * D_MODEL * 4 + 2 * TM * NC_PAD * 2
           + 4 * TM * D_MODEL * 4)
    return pl.pallas_call(
        functools.partial(_inproj_kernel, split_ctx=split_ctx),
        out_shape=[jax.ShapeDtypeStruct((w, T_ALL) if n in _FEATURE_MAJOR else (T_ALL, w), BF16)
                   for n, w in zip(names, widths)],
        grid=(ALL_TILES,),
        in_specs=_lat_ctx_specs(D_MODEL, split_ctx) + [
            pl.BlockSpec((1, 1, 6 * D_MODEL), lambda i: (_tile_mod_row(i), 0, 0)),
            pl.BlockSpec((1, D_MODEL), lambda i: (0, 0)),
            pl.BlockSpec((D_MODEL, NC_PAD), lambda i: (0, 0)),
            pl.BlockSpec((TM, HB), tab_map),
            pl.BlockSpec((TM, HB), tab_map),
            pl.BlockSpec((1, HB), lambda i: (0, 0)),
            pl.BlockSpec((1, HB), lambda i: (0, 0)),
        ],
        out_specs=[pl.BlockSpec((w, TM), lambda i: (0, i)) if n in _FEATURE_MAJOR
                   else pl.BlockSpec((TM, w), lambda i: (i, 0)) for n, w in zip(names, widths)],
        compiler_params=pltpu.CompilerParams(
            dimension_semantics=("arbitrary",), vmem_limit_bytes=_vmem_limit(est)),
        name="norm_inproj",
    )(x_lat, x_ctx, mod3, g1, w_pad, cs_tab, sn_tab, qg, kg)


ATT_TK = 8192
ATT_NK = SEQ // ATT_TK
ATT_CK = 256
ATT_VROWS = HEAD_DIM + 16


def _attn_kernel(qt_ref, k_ref, vt_ref, kc_ref, vct_ref, g_ref, o_ref, m_sc, acc_sc):
    i = pl.program_id(2)
    j = pl.program_id(3)
    is_ctx_q = i >= LAT_TILES_PER_BATCH

    @pl.when(j == 0)
    def _():
        m_sc[...] = jnp.full(m_sc.shape, -jnp.inf, F32)
        acc_sc[...] = jnp.zeros(acc_sc.shape, F32)

    def scores(kr, c, ck):
        k = kr[c * ck:(c + 1) * ck, :]
        return [jnp.dot(k, qt_ref[hh * HB:(hh + 1) * HB, :], preferred_element_type=F32)
                for hh in range(ATT_GRP)]

    def step(kr, vtr, nkeys):
        ck = min(ATT_CK, nkeys)
        nchunk = nkeys // ck
        ss = scores(kr, 0, ck)
        for c in range(nchunk):
            cur = ss
            if c + 1 < nchunk:
                ss = scores(kr, c + 1, ck)
            vt = vtr[0:ATT_VROWS, c * ck:(c + 1) * ck]
            ps, alphas = [], []
            for hh in range(ATT_GRP):
                m_prev = m_sc[hh]
                m_new = jnp.maximum(m_prev, jnp.max(cur[hh], axis=0, keepdims=True))
                alpha = jnp.exp2(m_prev - m_new)
                p = jnp.exp2(cur[hh] - m_new)
                m_sc[hh] = m_new
                ps.append(p.astype(BF16))
                alphas.append(alpha)
            for hh in range(ATT_GRP):
                acc_sc[hh] = alphas[hh] * acc_sc[hh] + jnp.dot(vt, ps[hh],
                                                               preferred_element_type=F32)

    @pl.when(jnp.logical_not(is_ctx_q))
    def _():
        step(k_ref, vt_ref, ATT_TK)

    @pl.when(j == ATT_NK - 1)
    def _():
        step(kc_ref, vct_ref, CTX_LEN)
        for hh in range(ATT_GRP):
            acc = acc_sc[hh]
            o = acc[0:HEAD_DIM] / acc[HEAD_DIM:HEAD_DIM + 1]
            ms = jnp.sum(o * o, axis=0, keepdims=True) * (1.0 / HEAD_DIM)
            g = g_ref[hh * HB:hh * HB + HEAD_DIM, :]
            y = o * lax.rsqrt(ms + EPS) * jnp.concatenate([g] * (TM // LANES), axis=1)
            y = jnp.concatenate([y, jnp.zeros_like(y)], axis=0)
            o_ref[:, hh * HB:(hh + 1) * HB] = y.T.astype(BF16)


def _gqa(aqt, ak, avt, ga_cols, want_ctx):
    nq = LAT_TILES_PER_BATCH + (1 if want_ctx else 0)

    def q_tile(b, i):
        return jnp.where(i < LAT_TILES_PER_BATCH, b * LAT_TILES_PER_BATCH + i, LAT_TILES + b)

    est = (2 * ATT_GRP * HB * TM * 2 + 4 * ATT_TK * HB * 2 + 4 * TM * HB * 2
           + ATT_GRP * TM * HB * 4 * 3 + 8 * TM * ATT_TK * 4)
    return pl.pallas_call(
        _attn_kernel,
        out_shape=jax.ShapeDtypeStruct((T_ALL if want_ctx else T_LAT, N_ATT_HEADS * HB), BF16),
        grid=(BATCH, N_ATT_KV, nq, ATT_NK),
        in_specs=[
            pl.BlockSpec((ATT_GRP * HB, TM), lambda b, c, i, j: (c, q_tile(b, i))),
            pl.BlockSpec((ATT_TK, HB), lambda b, c, i, j: (b * ATT_NK + j, c)),
            pl.BlockSpec((HB, ATT_TK), lambda b, c, i, j: (c, b * ATT_NK + j)),
            pl.BlockSpec((CTX_LEN, HB), lambda b, c, i, j: (LAT_TILES + b, c)),
            pl.BlockSpec((HB, CTX_LEN), lambda b, c, i, j: (c, LAT_TILES + b)),
            pl.BlockSpec((ATT_GRP * HB, LANES), lambda b, c, i, j: (c, 0)),
        ],
        out_specs=pl.BlockSpec((TM, ATT_GRP * HB), lambda b, c, i, j: (q_tile(b, i), c)),
        scratch_shapes=[
            pltpu.VMEM((ATT_GRP, 1, TM), F32),
            pltpu.VMEM((ATT_GRP, ATT_VROWS, TM), F32),
        ],
        compiler_params=pltpu.CompilerParams(
            dimension_semantics=("arbitrary",) * 4, vmem_limit_bytes=_vmem_limit(est)),
        name="gqa_attn",
    )(aqt, ak, avt, ak, avt, ga_cols)


NA_BAND = NA_WIN_ROWS * GRID_W
NA_CLASSES = 8
NA_ROWS_PER_ITER = 4
_NA_CLASS_ROWS = (0, 1, 2, 3, GRID_ROWS // 2, GRID_ROWS - 3, GRID_ROWS - 2, GRID_ROWS - 1)


def _na_bias_table(rpb):
    wr, wc = NA_WIN_ROWS, NA_WIN_COLS
    r = np.asarray(_NA_CLASS_ROWS)
    ridx = np.clip(r - wr // 2, 0, GRID_ROWS - wr)[:, None] + np.arange(wr)[None, :]
    dr = ridx - r[:, None] + (wr - 1)
    col = np.arange(GRID_W)
    cstart = np.clip(col - wc // 2, 0, GRID_W - wc)
    col_ok = (col[None, :] >= cstart[:, None]) & (col[None, :] < cstart[:, None] + wc)
    dc = np.clip(col[None, :] - col[:, None] + (wc - 1), 0, 2 * wc - 2)
    pick_r = (dr[:, :, None] == np.arange(2 * wr - 1)[None, None, :]).astype(np.float32)
    pick_c = (dc[:, :, None] == np.arange(2 * wc - 1)[None, None, :]).astype(np.float32)
    hp = lax.Precision.HIGHEST
    by_col = jnp.einsum("hrc,qkc->hrqk", rpb.astype(F32), jnp.asarray(pick_c), precision=hp)
    bias = jnp.einsum("hrqk,cwr->hcqwk", by_col, jnp.asarray(pick_r), precision=hp)
    bias = bias.reshape(N_NA_HEADS, NA_CLASSES, GRID_W, NA_BAND)
    mask = np.tile(col_ok, (1, wr))
    return jnp.where(mask[None, None], bias, NEG_INF)


def _head_rms_gain(o, g):
    ms = jnp.sum(o * o, axis=-1, keepdims=True) * (1.0 / HEAD_DIM)
    return o * lax.rsqrt(ms + EPS) * g


def _na_kernel(q_ref, k_ref, v_ref, kc_ref, vc_ref, qc_ref, bias_ref, g_ref, o_ref, oc_ref,
               *, want_ctx):
    kc = kc_ref[...]
    vc = vc_ref[...]
    g = g_ref[0]
    half = NA_WIN_ROWS // 2
    last = GRID_ROWS - NA_WIN_ROWS

    def rows(it, carry):
        r0 = it * NA_ROWS_PER_ITER
        qrows, bands, scores = [], [], []
        for d in range(NA_ROWS_PER_ITER):
            r = r0 + d
            start = jnp.clip(r - half, 0, last)
            cls = jnp.where(r < half, r, jnp.where(r > last + half, r - last, half))
            qrow = pl.ds(pl.multiple_of(r * GRID_W, GRID_W), GRID_W)
            band = pl.ds(pl.multiple_of(start * GRID_W, GRID_W), NA_BAND)
            q = q_ref[qrow, :]
            s = lax.dot_general(q, k_ref[band, :], (((1,), (1,)), ((), ())),
                                preferred_element_type=F32)
            sc = lax.dot_general(q, kc, (((1,), (1,)), ((), ())), preferred_element_type=F32)
            qrows.append(qrow)
            bands.append(band)
            scores.append((s, sc, cls))
        probs = []
        for s, sc, cls in scores:
            bt = bias_ref[0, cls]
            s = jnp.where(bt > 0.5 * NEG_INF, s + bt, NEG_INF)
            m = jnp.maximum(jnp.max(s, axis=-1, keepdims=True),
                            jnp.max(sc, axis=-1, keepdims=True))
            p = jnp.exp(s - m)
            pc = jnp.exp(sc - m)
            l = jnp.sum(p, axis=-1, keepdims=True) + jnp.sum(pc, axis=-1, keepdims=True)
            probs.append((p.astype(BF16), pc.astype(BF16), l))
        for qrow, band, (p, pc, l) in zip(qrows, bands, probs):
            o = (jnp.dot(p, v_ref[band, :], preferred_element_type=F32)
                 + jnp.dot(pc, vc, preferred_element_type=F32)) / l
            o_ref[qrow, :] = _head_rms_gain(o, g).astype(BF16)
        return carry

    lax.fori_loop(0, GRID_ROWS // NA_ROWS_PER_ITER, rows, 0)

    if want_ctx:
        sc = lax.dot_general(qc_ref[...], kc, (((1,), (1,)), ((), ())),
                             preferred_element_type=F32)
        m = jnp.max(sc, axis=-1, keepdims=True)
        pc = jnp.exp(sc - m)
        l = jnp.sum(pc, axis=-1, keepdims=True)
        o = jnp.dot(pc.astype(BF16), vc, preferred_element_type=F32) / l
        oc_ref[...] = _head_rms_gain(o, g).astype(BF16)
    else:
        oc_ref[...] = jnp.zeros(oc_ref.shape, oc_ref.dtype)


def _neigh(nq, nk, nv, bias_tab, gn, want_ctx):
    lat = pl.BlockSpec((SEQ, HB), lambda b, h: (b, h))
    ctx = pl.BlockSpec((CTX_LEN, HB), lambda b, h: (T_LAT // CTX_LEN + b, h))
    est = 2 * (4 * SEQ * HB * 2 + 4 * CTX_LEN * HB * 2 + NA_CLASSES * GRID_W * NA_BAND * 4)
    return pl.pallas_call(
        functools.partial(_na_kernel, want_ctx=want_ctx),
        out_shape=[jax.ShapeDtypeStruct((T_LAT, N_NA_HEADS * HB), BF16),
                   jax.ShapeDtypeStruct((T_CTX, N_NA_HEADS * HB), BF16)],
        grid=(BATCH, N_NA_HEADS),
        in_specs=[lat, lat, lat, ctx, ctx, ctx,
                  pl.BlockSpec((1, NA_CLASSES, GRID_W, NA_BAND), lambda b, h: (h, 0, 0, 0)),
                  pl.BlockSpec((1, 1, HB), lambda b, h: (h, 0, 0))],
        out_specs=[pl.BlockSpec((SEQ, HB), lambda b, h: (b, h)),
                   pl.BlockSpec((CTX_LEN, HB), lambda b, h: (b, h))],
        compiler_params=pltpu.CompilerParams(
            dimension_semantics=("arbitrary", "arbitrary"), vmem_limit_bytes=_vmem_limit(est)),
        name="neigh_attn",
    )(nq, nk, nv, nk, nv, nq, bias_tab, gn)


RET_NCHUNK = SEQ // RET_CHUNK
RET_NCHUNK_CTX = CTX_LEN // RET_CHUNK
RET_CHUNKS_PER_ITER = 4


def _ret_tables(log_g2):
    lf = log_g2[0][:, None, None]
    lb = log_g2[1][:, None, None]
    pos = jnp.arange(RET_CHUNK, dtype=F32)
    i = pos[None, :, None]
    j = pos[None, None, :]
    diff = i - j
    dm = jnp.where(diff > 0, jnp.exp(lf * jnp.maximum(diff, 0.0)),
                   jnp.where(diff < 0, jnp.exp(lb * jnp.maximum(-diff, 0.0)), 2.0)) * 0.5
    fwd_lane = (jnp.arange(LANES) < HEAD_DIM)[None, None, :]
    xi = jnp.where(fwd_lane, jnp.exp(lf * (i + 1.0)), jnp.exp(lb * (RET_CHUNK - i)))
    zt = jnp.where(fwd_lane, jnp.exp(lf * (RET_CHUNK - 1.0 - i)), jnp.exp(lb * i))
    fwd_row = (jnp.arange(LANES) < HEAD_DIM)[None, :, None]
    dec = jnp.where(fwd_row, jnp.exp(lf * RET_CHUNK), jnp.exp(lb * RET_CHUNK))
    dec = jnp.broadcast_to(dec, (N_RET_HEADS, LANES, LANES))
    return dm.astype(F32), xi.astype(F32), zt.astype(F32), dec.astype(F32)


def _ret_kernel(q_ref, k_ref, v_ref, gt_ref, qc_ref, kc_ref, vc_ref, gtc_ref,
                dm_ref, xi_ref, zt_ref, dec_ref, g_ref, o_ref, oc_ref,
                u_sc, s_sc, uc_sc, sc_sc, *, want_ctx):
    dm = dm_ref[0]
    xi = xi_ref[0]
    zt = zt_ref[0]
    dec = dec_ref[0]
    dec_f = dec[0:HEAD_DIM]
    dec_b = dec[HEAD_DIM:LANES]
    g = g_ref[0]
    C = RET_CHUNK

    def chunk_rows(n):
        return pl.ds(pl.multiple_of(n * C, C), C)

    def chunk_state_update(kr, vr, usc, ns):
        kzs = [(kr[chunk_rows(n), :].astype(F32) * zt).T.astype(BF16) for n in ns]
        for n, kz in zip(ns, kzs):
            usc[n] = jnp.dot(kz, vr[chunk_rows(n), :], preferred_element_type=F32)

    def chunk_out(qr, kr, vr, gtr, ssc, outr, ns):
        qds = [qr[chunk_rows(n), :] for n in ns]
        s2s = [lax.dot_general(qd, kr[chunk_rows(n), :], (((1,), (1,)), ((), ())),
                               preferred_element_type=F32) for n, qd in zip(ns, qds)]
        outs = []
        for n, qd, s2 in zip(ns, qds, s2s):
            inner = jnp.dot((s2 * dm).astype(BF16), vr[chunk_rows(n), :],
                            preferred_element_type=F32)
            qx = (qd.astype(F32) * xi).astype(BF16)
            outs.append(inner + jnp.dot(qx, ssc[n].astype(BF16), preferred_element_type=F32))
        for n, o in zip(ns, outs):
            gate = gtr[chunk_rows(n), :].astype(F32)
            outr[chunk_rows(n), :] = (_head_rms_gain(o, g)
                                      * (gate * jax.nn.sigmoid(gate))).astype(BF16)

    def scan_states(usc, ssc, nchunk, init_f, init_b):
        def fwd(n, sf):
            ssc[n, 0:HEAD_DIM, :] = sf
            return dec_f * sf + usc[n, 0:HEAD_DIM, :]

        def bwd(t, sb):
            n = nchunk - 1 - t
            ssc[n, HEAD_DIM:LANES, :] = sb
            return dec_b * sb + usc[n, HEAD_DIM:LANES, :]

        return (lax.fori_loop(0, nchunk, fwd, init_f), lax.fori_loop(0, nchunk, bwd, init_b))

    zero = jnp.zeros((HEAD_DIM, LANES), F32)
    ctx_chunks = list(range(RET_NCHUNK_CTX))
    chunk_state_update(kc_ref, vc_ref, uc_sc, ctx_chunks)
    ctx_f, ctx_b = scan_states(uc_sc, sc_sc, RET_NCHUNK_CTX, zero, zero)
    if want_ctx:
        chunk_out(qc_ref, kc_ref, vc_ref, gtc_ref, sc_sc, oc_ref, ctx_chunks)
    else:
        oc_ref[...] = jnp.zeros(oc_ref.shape, oc_ref.dtype)

    def upd(it, carry):
        chunk_state_update(k_ref, v_ref, u_sc,
                           [it * RET_CHUNKS_PER_ITER + d for d in range(RET_CHUNKS_PER_ITER)])
        return carry

    lax.fori_loop(0, RET_NCHUNK // RET_CHUNKS_PER_ITER, upd, 0)
    scan_states(u_sc, s_sc, RET_NCHUNK, ctx_f, ctx_b)

    def out(it, carry):
        chunk_out(q_ref, k_ref, v_ref, gt_ref, s_sc, o_ref,
                  [it * RET_CHUNKS_PER_ITER + d for d in range(RET_CHUNKS_PER_ITER)])
        return carry

    lax.fori_loop(0, RET_NCHUNK // RET_CHUNKS_PER_ITER, out, 0)


def _retention(rq, rk, rv, rg, tables, gr, want_ctx):
    lat = pl.BlockSpec((SEQ, HB), lambda b, h: (b, h))
    ctx = pl.BlockSpec((CTX_LEN, HB), lambda b, h: (T_LAT // CTX_LEN + b, h))
    tab = pl.BlockSpec((1, LANES, LANES), lambda b, h: (h, 0, 0))
    est = (2 * 5 * SEQ * HB * 2 + 2 * RET_NCHUNK * LANES * LANES * 4 + 8 * LANES * LANES * 4)
    return pl.pallas_call(
        functools.partial(_ret_kernel, want_ctx=want_ctx),
        out_shape=[jax.ShapeDtypeStruct((T_LAT, N_RET_HEADS * HB), BF16),
                   jax.ShapeDtypeStruct((T_CTX, N_RET_HEADS * HB), BF16)],
        grid=(BATCH, N_RET_HEADS),
        in_specs=[lat, lat, lat, lat, ctx, ctx, ctx, ctx, tab, tab, tab, tab,
                  pl.BlockSpec((1, 1, HB), lambda b, h: (h, 0, 0))],
        out_specs=[pl.BlockSpec((SEQ, HB), lambda b, h: (b, h)),
                   pl.BlockSpec((CTX_LEN, HB), lambda b, h: (b, h))],
        scratch_shapes=[
            pltpu.VMEM((RET_NCHUNK, LANES, LANES), F32),
            pltpu.VMEM((RET_NCHUNK, LANES, LANES), F32),
            pltpu.VMEM((RET_NCHUNK_CTX, LANES, LANES), F32),
            pltpu.VMEM((RET_NCHUNK_CTX, LANES, LANES), F32),
        ],
        compiler_params=pltpu.CompilerParams(
            dimension_semantics=("arbitrary", "arbitrary"), vmem_limit_bytes=_vmem_limit(est)),
        name="retention",
    )(rq, rk, rv, rg, rq, rk, rv, rg, *tables, gr)


_YA_W = N_ATT_HEADS * HB
_YN_W = N_NA_HEADS * HB
_YR_W = N_RET_HEADS * HB


MOE_SLOT = TM
META_ROWS = 8
_META_GATE0 = 2


def _route_t(logt):
    row = lax.broadcasted_iota(jnp.int32, logt.shape, 0).astype(F32)
    p = jnp.exp(logt - jnp.max(logt, axis=0, keepdims=True))
    best = None
    for grp in range(N_EXPERT_GROUPS):
        lo = float(grp * EXPERTS_PER_GROUP)
        ing = (row >= lo) & (row < lo + EXPERTS_PER_GROUP)
        pg = jnp.where(ing, p, -1.0)
        m1 = jnp.max(pg, axis=0, keepdims=True)
        i1 = jnp.min(jnp.where(pg == m1, row, float(N_EXPERTS)), axis=0, keepdims=True)
        pg2 = jnp.where(row == i1, -1.0, pg)
        m2 = jnp.max(pg2, axis=0, keepdims=True)
        i2 = jnp.min(jnp.where(pg2 == m2, row, float(N_EXPERTS)), axis=0, keepdims=True)
        cand = (m1 + m2, m1, m2, i1, i2, jnp.zeros_like(m1) + lo)
        if best is None:
            best = cand
        else:
            better = cand[0] > best[0]
            best = tuple(jnp.where(better, c, b) for c, b in zip(cand, best))
    _, m1, m2, i1, i2, base = best
    w = m1 + m2
    gates = [jnp.where(i1 == base + e, m1 / w, jnp.where(i2 == base + e, m2 / w, 0.0))
             for e in range(EXPERTS_PER_GROUP)]
    return base * (1.0 / EXPERTS_PER_GROUP), gates


def _merge_kernel(ya_ref, ynl_ref, ync_ref, yrl_ref, yrc_ref, xl_ref, xc_ref, mod_ref, g2_ref,
                  wo_ref, wr_ref, brt_ref,
                  xn_ref, h2_ref, metat_ref, metac_ref, ctab_ref, tot_ref, carry_sc, *, split_ctx):
    @pl.when(pl.program_id(0) == 0)
    def _():
        carry_sc[...] = jnp.zeros(carry_sc.shape, F32)

    mod = mod_ref[0]
    gt1 = mod[:, 2 * D_MODEL:3 * D_MODEL]
    sh2 = mod[:, 3 * D_MODEL:4 * D_MODEL]
    sc2 = mod[:, 4 * D_MODEL:5 * D_MODEL]
    yn = _pick_rows(ynl_ref, ync_ref, split_ctx)
    yr = _pick_rows(yrl_ref, yrc_ref, split_ctx)
    m = (jnp.dot(ya_ref[...], wo_ref[0:_YA_W, :], preferred_element_type=F32)
         + jnp.dot(yn, wo_ref[_YA_W:_YA_W + _YN_W, :], preferred_element_type=F32)
         + jnp.dot(yr, wo_ref[_YA_W + _YN_W:MIX_PAD, :], preferred_element_type=F32))
    x = _pick_rows(xl_ref, xc_ref, split_ctx) + gt1 * m
    xn_ref[...] = x
    ms = jnp.mean(x * x, axis=-1, keepdims=True)
    h2 = x * lax.rsqrt(ms + EPS) * g2_ref[...] * (1.0 + sc2) + sh2
    h2_ref[...] = h2.astype(BF16)

    ntile = TM // LANES
    h_hi = h2.astype(BF16)
    h_lo = (h2 - h_hi.astype(F32)).astype(BF16)
    wr = wr_ref[...]
    w_hi = wr.astype(BF16)
    w_lo = (wr - w_hi.astype(F32)).astype(BF16)
    logits = (jnp.dot(h_hi, w_hi, preferred_element_type=F32)
              + jnp.dot(h_lo, w_hi, preferred_element_type=F32)
              + jnp.dot(h_hi, w_lo, preferred_element_type=F32))
    logt = logits.T[0:N_EXPERTS, :]
    gsel, gates = _route_t(logt + jnp.concatenate([brt_ref[...]] * ntile, axis=1))

    grow = lax.broadcasted_iota(jnp.int32, (META_ROWS, TM), 0).astype(F32)
    onehot = jnp.where(grow == gsel, 1.0, 0.0)
    earlier = (lax.broadcasted_iota(jnp.int32, (TM, TM), 0)
               < lax.broadcasted_iota(jnp.int32, (TM, TM), 1))
    excl = jnp.dot(onehot.astype(BF16), jnp.where(earlier, 1.0, 0.0).astype(BF16),
                   preferred_element_type=F32)
    carry = carry_sc[...]
    rank = jnp.sum(onehot * (jnp.concatenate([carry] * ntile, axis=1) + excl),
                   axis=0, keepdims=True)
    ctab_ref[0] = carry
    carry = carry + jnp.sum(onehot, axis=1, keepdims=True)
    carry_sc[...] = carry
    tot_ref[...] = carry

    metat = jnp.concatenate([gsel, rank] + gates
                            + [jnp.zeros((META_ROWS - _META_GATE0 - EXPERTS_PER_GROUP, TM), F32)],
                            axis=0)
    metat_ref[...] = metat
    metac_ref[...] = jnp.concatenate([metat, jnp.zeros((LANES - META_ROWS, TM), F32)], axis=0).T


def _merge(ya, yn_lat, yn_ctx, yr_lat, yr_ctx, x_lat, x_ctx, split_ctx, mod3, g2, wo_pad, wr_pad,
           brt, ntiles):
    rows = ntiles * TM
    est = (2 * MIX_PAD * D_MODEL * 2 + 4 * TM * MIX_PAD * 2 + 10 * TM * D_MODEL * 4
           + 4 * TM * TM * 4)
    const = lambda i: (0, 0)
    return pl.pallas_call(
        functools.partial(_merge_kernel, split_ctx=split_ctx),
        out_shape=[jax.ShapeDtypeStruct((rows, D_MODEL), F32),
                   jax.ShapeDtypeStruct((rows, D_MODEL), BF16),
                   jax.ShapeDtypeStruct((META_ROWS, rows), F32),
                   jax.ShapeDtypeStruct((rows, LANES), F32),
                   jax.ShapeDtypeStruct((ntiles, META_ROWS, LANES), F32),
                   jax.ShapeDtypeStruct((META_ROWS, LANES), F32)],
        grid=(ntiles,),
        in_specs=([pl.BlockSpec((TM, _YA_W), lambda i: (i, 0))]
                  + _lat_ctx_specs(_YN_W, split_ctx) + _lat_ctx_specs(_YR_W, split_ctx)
                  + _lat_ctx_specs(D_MODEL, split_ctx) + [
            pl.BlockSpec((1, 1, 6 * D_MODEL), lambda i: (_tile_mod_row(i), 0, 0)),
            pl.BlockSpec((1, D_MODEL), const),
            pl.BlockSpec((MIX_PAD, D_MODEL), const),
            pl.BlockSpec((D_MODEL, LANES), const),
            pl.BlockSpec((N_EXPERTS, LANES), const),
        ]),
        out_specs=[pl.BlockSpec((TM, D_MODEL), lambda i: (i, 0)),
                   pl.BlockSpec((TM, D_MODEL), lambda i: (i, 0)),
                   pl.BlockSpec((META_ROWS, TM), lambda i: (0, i)),
                   pl.BlockSpec((TM, LANES), lambda i: (i, 0)),
                   pl.BlockSpec((1, META_ROWS, LANES), lambda i: (i, 0, 0)),
                   pl.BlockSpec((META_ROWS, LANES), const)],
        scratch_shapes=[pltpu.VMEM((META_ROWS, LANES), F32)],
        compiler_params=pltpu.CompilerParams(
            dimension_semantics=("arbitrary",), vmem_limit_bytes=_vmem_limit(est)),
        name="merge_outproj_router",
    )(ya, yn_lat, yn_ctx, yr_lat, yr_ctx, x_lat, x_ctx, mod3, g2, wo_pad, wr_pad, brt)


def _moe_plan(ctab, tot, ntiles):
    grp = N_EXPERT_GROUPS
    i32 = jnp.int32
    a = ctab[:, :grp, 0].astype(i32).T
    totg = tot[:grp, 0].astype(i32)
    b = jnp.concatenate([a[:, 1:], totg[:, None]], axis=1)
    nslot = (totg + MOE_SLOT - 1) // MOE_SLOT
    slot_end = jnp.cumsum(nslot)
    slot_base = slot_end - nslot
    total_slots = slot_end[-1]
    first_j = a // MOE_SLOT
    last_j = (jnp.maximum(b, 1) - 1) // MOE_SLOT
    npairs = jnp.where(b > a, last_j - first_j + 1, 0).reshape(-1)
    cum = jnp.cumsum(npairs)
    start = cum - npairs
    total_pairs = cum[-1]
    n_pairs_max = (grp + 1) * ntiles + grp
    pidx = jnp.arange(n_pairs_max, dtype=i32)
    p = jnp.minimum(pidx, total_pairs - 1)
    gc = jnp.sum((cum[None, :] <= p[:, None]).astype(i32), axis=1)
    slot = slot_base[gc // ntiles] + first_j.reshape(-1)[gc] + (p - start[gc])
    valid = pidx < total_pairs
    prev_slot = jnp.concatenate([jnp.full((1,), -1, i32), slot[:-1]])
    next_slot = jnp.concatenate([slot[1:], jnp.full((1,), -1, i32)])
    is_first = valid & (slot != prev_slot)
    is_last = valid & ((slot != next_slot) | (pidx == total_pairs - 1))
    flags = is_first.astype(i32) + 2 * is_last.astype(i32) + 4 * valid.astype(i32)
    n_slots_max = ntiles + grp
    sidx = jnp.arange(n_slots_max, dtype=i32)
    sgrp = jnp.minimum(jnp.sum((slot_end[None, :] <= sidx[:, None]).astype(i32), axis=1), grp - 1)
    sr0 = (sidx - slot_base[sgrp]) * MOE_SLOT
    row0 = slot_base[:, None] * MOE_SLOT + a
    row1 = slot_base[:, None] * MOE_SLOT + jnp.maximum(b, a + 1) - 1
    wb = jnp.stack([row0 // MOE_SLOT, row1 // MOE_SLOT], axis=1)
    wb = jnp.clip(wb, 0, total_slots - 1).reshape(-1).astype(i32)
    return dict(pslot=slot.astype(i32), pchunk=(gc % ntiles).astype(i32), pflag=flags,
                sgrp=sgrp, sr0=sr0.astype(i32), wb=wb, sbase=slot_base.astype(i32),
                n_pairs=n_pairs_max, n_slots=n_slots_max)


def _moe_kernel(pslot_ref, pchunk_ref, pflag_ref, sgrp_ref, sr0_ref,
                h_ref, metat_ref, metac_ref, w1_ref, w3_ref, w2_ref, o_ref, x_sc, g_sc):
    p = pl.program_id(0)
    flags = pflag_ref[p]
    slot = pslot_ref[p]

    @pl.when((flags & 1) != 0)
    def _():
        x_sc[...] = jnp.zeros(x_sc.shape, F32)
        g_sc[...] = jnp.zeros(g_sc.shape, F32)

    @pl.when((flags & 4) != 0)
    def _():
        mt = metat_ref[...].astype(jnp.int32)
        want = lax.broadcasted_iota(jnp.int32, (MOE_SLOT, TM), 0) + sr0_ref[slot]
        sel = (mt[1:2, :] == want) & (mt[0:1, :] == sgrp_ref[slot])
        pm = jnp.where(sel, 1.0, 0.0).astype(BF16)
        x_sc[...] += jnp.dot(pm, h_ref[...], preferred_element_type=F32)
        mc = metac_ref[...]
        hi = mc.astype(BF16)
        r1 = mc - hi.astype(F32)
        mid = r1.astype(BF16)
        lo = (r1 - mid.astype(F32)).astype(BF16)
        g_sc[...] += (jnp.dot(pm, hi, preferred_element_type=F32)
                      + jnp.dot(pm, mid, preferred_element_type=F32)
                      + jnp.dot(pm, lo, preferred_element_type=F32))

    @pl.when((flags & 2) != 0)
    def _():
        x = x_sc[...].astype(BF16)
        gs = g_sc[...]
        acc = jnp.zeros((MOE_SLOT, D_MODEL), F32)
        for e in range(EXPERTS_PER_GROUP):
            a = jnp.dot(x, w1_ref[0, 0, e], preferred_element_type=F32)
            b = jnp.dot(x, w3_ref[0, 0, e], preferred_element_type=F32)
            gate = gs[:, _META_GATE0 + e:_META_GATE0 + e + 1]
            act = (a * jax.nn.sigmoid(a)) * b * gate
            acc = acc + jnp.dot(act.astype(BF16), w2_ref[0, 0, e], preferred_element_type=F32)
        o_ref[...] = acc.astype(BF16)


def _moe_sorted(plan, h2, metat, metac, w1g, w3g, w2g, layer):
    est = (2 * 3 * EXPERTS_PER_GROUP * D_MODEL * D_EXPERT * 2 + 6 * TM * D_MODEL * 2
           + 2 * MOE_SLOT * D_MODEL * 4 + 6 * MOE_SLOT * D_EXPERT * 4 + 4 * TM * MOE_SLOT * 4)
    wmap = lambda p, ps, pc, pf, sg, sr: (layer, sg[ps[p]], 0, 0, 0)
    grid_spec = pltpu.PrefetchScalarGridSpec(
        num_scalar_prefetch=5,
        grid=(plan["n_pairs"],),
        in_specs=[
            pl.BlockSpec((TM, D_MODEL), lambda p, ps, pc, pf, sg, sr: (pc[p], 0)),
            pl.BlockSpec((META_ROWS, TM), lambda p, ps, pc, pf, sg, sr: (0, pc[p])),
            pl.BlockSpec((TM, LANES), lambda p, ps, pc, pf, sg, sr: (pc[p], 0)),
            pl.BlockSpec((1, 1, EXPERTS_PER_GROUP, D_MODEL, D_EXPERT), wmap),
            pl.BlockSpec((1, 1, EXPERTS_PER_GROUP, D_MODEL, D_EXPERT), wmap),
            pl.BlockSpec((1, 1, EXPERTS_PER_GROUP, D_EXPERT, D_MODEL), wmap),
        ],
        out_specs=pl.BlockSpec((MOE_SLOT, D_MODEL), lambda p, ps, pc, pf, sg, sr: (ps[p], 0)),
        scratch_shapes=[pltpu.VMEM((MOE_SLOT, D_MODEL), F32), pltpu.VMEM((MOE_SLOT, LANES), F32)],
    )
    return pl.pallas_call(
        _moe_kernel,
        out_shape=jax.ShapeDtypeStruct((plan["n_slots"] * MOE_SLOT, D_MODEL), BF16),
        grid_spec=grid_spec,
        compiler_params=pltpu.CompilerParams(
            dimension_semantics=("arbitrary",), vmem_limit_bytes=_vmem_limit(est)),
        name="moe_sorted_experts",
    )(plan["pslot"], plan["pchunk"], plan["pflag"], plan["sgrp"], plan["sr0"],
      h2, metat, metac, w1g, w3g, w2g)


_COMBINE_WINDOWS = 2 * N_EXPERT_GROUPS


def _combine_kernel(wb_ref, sb_ref, *refs, ntiles, final):
    y_refs = refs[:_COMBINE_WINDOWS]
    metac_ref, xn_ref, mod_ref, fg_ref, o_ref = refs[_COMBINE_WINDOWS:]
    i = pl.program_id(0)
    mc = metac_ref[...]
    grp = mc[:, 0:1].astype(jnp.int32)
    rank = mc[:, 1:2].astype(jnp.int32)
    col = lax.broadcasted_iota(jnp.int32, (TM, MOE_SLOT), 1)
    m = jnp.zeros((TM, D_MODEL), F32)
    for g in range(N_EXPERT_GROUPS):
        row = rank + sb_ref[g] * MOE_SLOT
        blk0 = wb_ref[(2 * g) * ntiles + i]
        for k in range(2):
            blk = wb_ref[(2 * g + k) * ntiles + i]
            sel = (grp == g) & (row == col + blk * MOE_SLOT)
            if k == 1:
                sel = jnp.logical_and(sel, blk != blk0)
            m = m + jnp.dot(jnp.where(sel, 1.0, 0.0).astype(BF16), y_refs[2 * g + k][...],
                            preferred_element_type=F32)
    gt2 = mod_ref[0][:, 5 * D_MODEL:6 * D_MODEL]
    x = xn_ref[...] + gt2 * m
    if final:
        ms = jnp.mean(x * x, axis=-1, keepdims=True)
        x = x * lax.rsqrt(ms + EPS) * fg_ref[...]
    o_ref[...] = x


def _combine(plan, ys, metac, xn, mod3, fg, ntiles, final):
    def ymap(w):
        return lambda i, wb, sb: (wb[w * ntiles + i], 0)

    est = (2 * _COMBINE_WINDOWS * MOE_SLOT * D_MODEL * 2 + 8 * TM * D_MODEL * 4
           + 4 * TM * MOE_SLOT * 4)
    grid_spec = pltpu.PrefetchScalarGridSpec(
        num_scalar_prefetch=2,
        grid=(ntiles,),
        in_specs=[pl.BlockSpec((MOE_SLOT, D_MODEL), ymap(w)) for w in range(_COMBINE_WINDOWS)] + [
            pl.BlockSpec((TM, LANES), lambda i, wb, sb: (i, 0)),
            pl.BlockSpec((TM, D_MODEL), lambda i, wb, sb: (i, 0)),
            pl.BlockSpec((1, 1, 6 * D_MODEL), lambda i, wb, sb: (_tile_mod_row(i), 0, 0)),
            pl.BlockSpec((1, D_MODEL), lambda i, wb, sb: (0, 0)),
        ],
        out_specs=pl.BlockSpec((TM, D_MODEL), lambda i, wb, sb: (i, 0)),
    )
    return pl.pallas_call(
        functools.partial(_combine_kernel, ntiles=ntiles, final=final),
        out_shape=jax.ShapeDtypeStruct((ntiles * TM, D_MODEL), F32),
        grid_spec=grid_spec,
        compiler_params=pltpu.CompilerParams(
            dimension_semantics=("arbitrary",), vmem_limit_bytes=_vmem_limit(est)),
        name="moe_combine",
    )(plan["wb"], plan["sbase"], *([ys] * _COMBINE_WINDOWS), metac, xn, mod3, fg)


def _rope_tables():
    t = np.arange(SEQ)
    nf = HEAD_DIM // 4
    inv = (np.float32(ROPE_THETA) ** (-np.arange(nf, dtype=np.float32) / np.float32(nf)))
    inv = inv.astype(np.float32)
    ang_r = (t // GRID_W).astype(np.float32)[:, None] * inv[None, :]
    ang_c = (t % GRID_W).astype(np.float32)[:, None] * inv[None, :]
    cr, sr, cc, sc = np.cos(ang_r), np.sin(ang_r), np.cos(ang_c), np.sin(ang_c)
    zeros = np.zeros((SEQ, HEAD_DIM), np.float32)
    cs = np.concatenate([cr, cr, cc, cc, zeros], axis=-1)
    sn = np.concatenate([-sr, sr, -sc, sc, zeros], axis=-1)
    ident = np.concatenate([np.ones((CTX_LEN, HEAD_DIM), np.float32),
                            np.zeros((CTX_LEN, HEAD_DIM), np.float32)], axis=-1)
    cs = np.concatenate([cs, ident], axis=0).astype(np.float32)
    sn = np.concatenate([sn, np.zeros((CTX_LEN, LANES), np.float32)], axis=0).astype(np.float32)
    return jnp.asarray(cs), jnp.asarray(sn)


def _pad_out_weight(w_out_l):
    w = w_out_l.reshape(-1, HEAD_DIM, D_MODEL)
    w = jnp.concatenate([w, jnp.zeros_like(w)], axis=1)
    return w.reshape(MIX_PAD, D_MODEL).astype(BF16)


def _pad_in_weight(w_in_l):
    parts = []
    for _, src, heads, mode in _SECTIONS:
        w = w_in_l[:, src:src + heads * HEAD_DIM].reshape(D_MODEL, heads, HEAD_DIM)
        other = w if mode == "dup" else jnp.zeros_like(w)
        parts.append(jnp.concatenate([w, other], axis=-1).reshape(D_MODEL, heads * HB))
    return jnp.concatenate(parts, axis=-1).astype(BF16)


def kernel(x, c, ctx, c_ctx, w_ada, b_ada, norm1_g, norm2_g, w_in, q_norm_g, k_norm_g, na_rpb,
           ret_decay, mix_g, w_out, w_router, b_router, w_exp1, w_exp3, w_exp2, final_g):
    cs_tab, sn_tab = _rope_tables()

    cvec = jnp.concatenate([c, c_ctx[None, :], jnp.zeros((8 - BATCH - 1, D_MODEL), F32)], axis=0)
    mod_all = _ada_mod(cvec, w_ada, b_ada)

    wr_pad = jnp.concatenate([w_router, jnp.zeros((D_MODEL, LANES - N_EXPERTS), F32)], axis=1)
    brt = jnp.broadcast_to(b_router[:, None], (N_EXPERTS, LANES))
    zero_lane = jnp.zeros((HEAD_DIM,), F32)

    x_lat = x.reshape(T_LAT, D_MODEL)
    x_ctx = ctx.reshape(T_CTX, D_MODEL)

    gshape = (DEPTH, N_EXPERT_GROUPS, EXPERTS_PER_GROUP)
    w1g = w_exp1.astype(BF16).reshape(gshape + (D_MODEL, D_EXPERT))
    w3g = w_exp3.astype(BF16).reshape(gshape + (D_MODEL, D_EXPERT))
    w2g = w_exp2.astype(BF16).reshape(gshape + (D_EXPERT, D_MODEL))

    for l in range(DEPTH):
        last = l == DEPTH - 1
        want_ctx = not last
        split_ctx = l == 0
        mod3 = mod_all[l].reshape(8, 1, 6 * D_MODEL)
        w_pad = _pad_in_weight(w_in[l])
        qg = jnp.concatenate([q_norm_g[l], zero_lane])[None, :]
        kg = jnp.concatenate([k_norm_g[l], zero_lane])[None, :]
        aq, ak, av, nq, nk, nv, rq, rk, rv, rg = _inproj(
            x_lat, x_ctx, split_ctx, mod3, norm1_g[l][None, :], w_pad, cs_tab, sn_tab, qg, kg)

        ga = jnp.broadcast_to(mix_g[l][:ATT_Q].reshape(N_ATT_HEADS, HEAD_DIM, 1),
                              (N_ATT_HEADS, HEAD_DIM, LANES))
        ga = jnp.concatenate([ga, jnp.zeros_like(ga)], axis=1).reshape(N_ATT_HEADS * HB, LANES)
        gn = _pad_heads(mix_g[l][ATT_Q:ATT_Q + NA_W], N_NA_HEADS)
        gr = _pad_heads(mix_g[l][ATT_Q + NA_W:], N_RET_HEADS)

        ya = _gqa(aq, ak, av, ga, want_ctx)
        yn_lat, yn_ctx = _neigh(nq, nk, nv, _na_bias_table(na_rpb[l]), gn, want_ctx)
        log_g2 = jax.nn.log_sigmoid(ret_decay[l].astype(F32))
        yr_lat, yr_ctx = _retention(rq, rk, rv, rg, _ret_tables(log_g2), gr, want_ctx)

        wo_pad = _pad_out_weight(w_out[l])
        ntiles = LAT_TILES if last else ALL_TILES
        xn, h2, metat, metac, ctab, tot = _merge(
            ya, yn_lat, yn_ctx, yr_lat, yr_ctx, x_lat, x_ctx, split_ctx, mod3,
            norm2_g[l][None, :], wo_pad, wr_pad, brt, ntiles)
        plan = _moe_plan(ctab, tot, ntiles)
        ys = _moe_sorted(plan, h2, metat, metac, w1g, w3g, w2g, l)
        x_lat = _combine(plan, ys, metac, xn, mod3, final_g[None, :], ntiles, last)
        x_ctx = x_lat

    return x_lat.reshape(BATCH, SEQ, D_MODEL)
```

```python
import functools

import numpy as np
import jax
import jax.numpy as jnp
from jax import lax
from jax.experimental import pallas as pl
from jax.experimental.pallas import tpu as pltpu

D_MODEL = 1024
BATCH = 2
SEQ = 8192
DEPTH = 2
GRID_W = 64
GRID_ROWS = SEQ // GRID_W
CTX_LEN = 256
HEAD_DIM = 64
N_ATT_HEADS = 6
N_ATT_KV = 2
ATT_GRP = N_ATT_HEADS // N_ATT_KV
N_NA_HEADS = 4
N_RET_HEADS = 6
ATT_Q = N_ATT_HEADS * HEAD_DIM
ATT_KV = N_ATT_KV * HEAD_DIM
NA_W = N_NA_HEADS * HEAD_DIM
RET_W = N_RET_HEADS * HEAD_DIM
NA_WIN_ROWS = 8
NA_WIN_COLS = 16
RET_CHUNK = 128
ROPE_THETA = 10000.0
N_EXPERTS = 16
N_EXPERT_GROUPS = 4
EXPERTS_PER_GROUP = N_EXPERTS // N_EXPERT_GROUPS
D_EXPERT = 512
EPS = 1e-6
NEG_INF = -1e30

LANES = 128
VMEM_LIMIT_CAP = 56 * 1024 * 1024

T_LAT = BATCH * SEQ
T_CTX = BATCH * CTX_LEN
T_ALL = T_LAT + T_CTX
TM = 256
LAT_TILES_PER_BATCH = SEQ // TM
LAT_TILES = T_LAT // TM
CTX_TILES = T_CTX // TM
ALL_TILES = LAT_TILES + CTX_TILES
TP = TM
MOE_CHUNK = 2 * TM
P_LAT_TILES_PER_BATCH = SEQ // TP
P_LAT_TILES = T_LAT // TP
P_ALL_TILES = T_ALL // TP
HB = LANES

F32 = jnp.float32
BF16 = jnp.bfloat16

_SECTIONS = (
    ("aq", 0, N_ATT_HEADS, "pad"),
    ("ak", ATT_Q, N_ATT_KV, "pad"),
    ("av", ATT_Q + ATT_KV, N_ATT_KV, "pad"),
    ("nq", ATT_Q + 2 * ATT_KV, N_NA_HEADS, "pad"),
    ("nk", ATT_Q + 2 * ATT_KV + NA_W, N_NA_HEADS, "pad"),
    ("nv", ATT_Q + 2 * ATT_KV + 2 * NA_W, N_NA_HEADS, "pad"),
    ("rq", ATT_Q + 2 * ATT_KV + 3 * NA_W, N_RET_HEADS, "dup"),
    ("rk", ATT_Q + 2 * ATT_KV + 3 * NA_W + RET_W, N_RET_HEADS, "dup"),
    ("rv", ATT_Q + 2 * ATT_KV + 3 * NA_W + 2 * RET_W, N_RET_HEADS, "pad"),
    ("rg", ATT_Q + 2 * ATT_KV + 3 * NA_W + 3 * RET_W, N_RET_HEADS, "pad"),
)
_SEC_OFF = {}
_off = 0
for _name, _src, _heads, _mode in _SECTIONS:
    _SEC_OFF[_name] = (_off, _heads * HB)
    _off += _heads * HB
NC_PAD = _off
_FEATURE_MAJOR = ("aq", "av")
LOG2E = 1.4426950408889634
MIX_PAD = (N_ATT_HEADS + N_NA_HEADS + N_RET_HEADS) * HB


def _vmem_limit(nbytes):
    return int(min(VMEM_LIMIT_CAP, max(16 * 1024 * 1024, 2 * nbytes)))


def _pad_heads(v, heads):
    v = v.reshape(heads, 1, HEAD_DIM).astype(F32)
    return jnp.concatenate([v, jnp.zeros_like(v)], axis=-1)


def _tile_mod_row(i, per_batch=LAT_TILES_PER_BATCH):
    return jnp.where(i < per_batch, 0, jnp.where(i < BATCH * per_batch, 1, 2))


def _ptile_mod_row(i):
    return _tile_mod_row(i, P_LAT_TILES_PER_BATCH)


ADA_TN = 1536


def _ada_kernel(c_ref, w_ref, b_ref, o_ref):
    c = c_ref[...]
    s = c * jax.nn.sigmoid(c)
    o_ref[0] = jnp.dot(s, w_ref[0], preferred_element_type=F32,
                       precision=lax.Precision.HIGHEST) + b_ref[0]


def _ada_mod(cvec, w_ada, b_ada):
    n = 6 * D_MODEL
    return pl.pallas_call(
        _ada_kernel,
        out_shape=jax.ShapeDtypeStruct((DEPTH, 8, n), F32),
        grid=(DEPTH, n // ADA_TN),
        in_specs=[
            pl.BlockSpec((8, D_MODEL), lambda l, j: (0, 0)),
            pl.BlockSpec((1, D_MODEL, ADA_TN), lambda l, j: (l, 0, j)),
            pl.BlockSpec((1, 1, ADA_TN), lambda l, j: (l, 0, j)),
        ],
        out_specs=pl.BlockSpec((1, 8, ADA_TN), lambda l, j: (l, 0, j)),
        compiler_params=pltpu.CompilerParams(
            dimension_semantics=("arbitrary", "arbitrary"),
            vmem_limit_bytes=_vmem_limit(2 * D_MODEL * ADA_TN * 4)),
        name="ada_mod",
    )(cvec, w_ada, b_ada.reshape(DEPTH, 1, n))


def _rope_swap(t):
    lane = lax.broadcasted_iota(jnp.int32, t.shape, 1)
    first_half = (lane % 32) < 16
    return jnp.where(first_half, pltpu.roll(t, LANES - 16, 1), pltpu.roll(t, 16, 1))


def _pick_rows(lat_ref, ctx_ref, split_ctx):
    if not split_ctx:
        return lat_ref[...]
    return jnp.where(pl.program_id(0) >= P_LAT_TILES, ctx_ref[...], lat_ref[...])


def _lat_ctx_specs(width, split_ctx):
    if split_ctx:
        return [pl.BlockSpec((TP, width), lambda i, *_: (jnp.minimum(i, P_LAT_TILES - 1), 0)),
                pl.BlockSpec((TP, width), lambda i, *_: (jnp.maximum(i - P_LAT_TILES, 0), 0))]
    return [pl.BlockSpec((TP, width), lambda i, *_: (i, 0)),
            pl.BlockSpec((TP, width), lambda i, *_: (0, 0))]


def _inproj_kernel(xl_ref, xc_ref, mod_ref, g1_ref, w_ref, cs_ref, sn_ref, qg_ref, kg_ref,
                   aq_ref, ak_ref, av_ref, nq_ref, nk_ref, nv_ref,
                   rq_ref, rk_ref, rv_ref, rg_ref, *, split_ctx):
    x = _pick_rows(xl_ref, xc_ref, split_ctx)
    mod = mod_ref[0]
    sh1 = mod[:, 0:D_MODEL]
    sc1 = mod[:, D_MODEL:2 * D_MODEL]
    ms = jnp.mean(x * x, axis=-1, keepdims=True)
    h = x * lax.rsqrt(ms + EPS) * g1_ref[...]
    h = (h * (1.0 + sc1) + sh1).astype(BF16)
    cs = cs_ref[...]
    sn = sn_ref[...]

    sections = {}

    def proj(name, hidx):
        if name not in sections:
            off, width = _SEC_OFF[name]
            sections[name] = jnp.dot(h, w_ref[:, off:off + width], preferred_element_type=F32)
        return sections[name][:, hidx * HB:(hidx + 1) * HB]

    def normed_rope(z, g):
        ss = jnp.sum(z * z, axis=-1, keepdims=True)
        zn = z * lax.rsqrt(ss * (1.0 / HEAD_DIM) + EPS) * g
        return zn * cs + _rope_swap(zn) * sn

    scale = HEAD_DIM ** -0.5
    for hh in range(N_ATT_HEADS):
        z = normed_rope(proj("aq", hh), qg_ref[...]) * (scale * LOG2E)
        aq_ref[hh * HB:(hh + 1) * HB, :] = z.T.astype(BF16)
    for hh in range(N_ATT_KV):
        z = normed_rope(proj("ak", hh), kg_ref[...])
        ak_ref[:, hh * HB:(hh + 1) * HB] = z.astype(BF16)
        zv = proj("av", hh)
        lane = lax.broadcasted_iota(jnp.int32, zv.shape, 1)
        av_ref[hh * HB:(hh + 1) * HB, :] = jnp.where(lane == HEAD_DIM, 1.0, zv).T.astype(BF16)
    for hh in range(N_NA_HEADS):
        nq_ref[:, hh * HB:(hh + 1) * HB] = (proj("nq", hh) * scale).astype(BF16)
        nk_ref[:, hh * HB:(hh + 1) * HB] = proj("nk", hh).astype(BF16)
        nv_ref[:, hh * HB:(hh + 1) * HB] = proj("nv", hh).astype(BF16)
    for hh in range(N_RET_HEADS):
        rq_ref[:, hh * HB:(hh + 1) * HB] = proj("rq", hh).astype(BF16)
        rk_ref[:, hh * HB:(hh + 1) * HB] = (proj("rk", hh) * scale).astype(BF16)
        rv_ref[:, hh * HB:(hh + 1) * HB] = proj("rv", hh).astype(BF16)
        rg_ref[:, hh * HB:(hh + 1) * HB] = proj("rg", hh).astype(BF16)


def _inproj(x_lat, x_ctx, split_ctx, mod3, g1, w_pad, cs_tab, sn_tab, qg, kg):
    names = [s[0] for s in _SECTIONS]
    widths = [_SEC_OFF[n][1] for n in names]

    def tab_map(i):
        return (jnp.where(i < P_LAT_TILES, i % P_LAT_TILES_PER_BATCH, P_LAT_TILES_PER_BATCH), 0)

    est = (D_MODEL * NC_PAD * 2 + 4 * TP * D_MODEL * 4 + 2 * TP * NC_PAD * 2
           + 6 * TP * D_MODEL * 4)
    return pl.pallas_call(
        functools.partial(_inproj_kernel, split_ctx=split_ctx),
        out_shape=[jax.ShapeDtypeStruct((w, T_ALL) if n in _FEATURE_MAJOR else (T_ALL, w), BF16)
                   for n, w in zip(names, widths)],
        grid=(P_ALL_TILES,),
        in_specs=_lat_ctx_specs(D_MODEL, split_ctx) + [
            pl.BlockSpec((1, 1, 6 * D_MODEL), lambda i: (_ptile_mod_row(i), 0, 0)),
            pl.BlockSpec((1, D_MODEL), lambda i: (0, 0)),
            pl.BlockSpec((D_MODEL, NC_PAD), lambda i: (0, 0), pipeline_mode=pl.Buffered(1)),
            pl.BlockSpec((TP, HB), tab_map),
            pl.BlockSpec((TP, HB), tab_map),
            pl.BlockSpec((1, HB), lambda i: (0, 0)),
            pl.BlockSpec((1, HB), lambda i: (0, 0)),
        ],
        out_specs=[pl.BlockSpec((w, TP), lambda i: (0, i)) if n in _FEATURE_MAJOR
                   else pl.BlockSpec((TP, w), lambda i: (i, 0)) for n, w in zip(names, widths)],
        compiler_params=pltpu.CompilerParams(
            dimension_semantics=("arbitrary",), vmem_limit_bytes=_vmem_limit(est)),
        name="norm_inproj",
    )(x_lat, x_ctx, mod3, g1, w_pad, cs_tab, sn_tab, qg, kg)


ATT_TK = 8192
ATT_NK = SEQ // ATT_TK
ATT_CK = 256
ATT_VROWS = HEAD_DIM + 16


def _attn_kernel(qt_ref, k_ref, vt_ref, kc_ref, vct_ref, g_ref, o_ref, m_sc, acc_sc):
    i = pl.program_id(2)
    j = pl.program_id(3)
    is_ctx_q = i >= LAT_TILES_PER_BATCH

    @pl.when(j == 0)
    def _():
        m_sc[...] = jnp.full(m_sc.shape, -jnp.inf, F32)
        acc_sc[...] = jnp.zeros(acc_sc.shape, F32)

    def scores(kr, c, ck):
        k = kr[c * ck:(c + 1) * ck, :]
        return [jnp.dot(k, qt_ref[hh * HB:(hh + 1) * HB, :], preferred_element_type=F32)
                for hh in range(ATT_GRP)]

    def step(kr, vtr, nkeys):
        ck = min(ATT_CK, nkeys)
        nchunk = nkeys // ck
        ss = scores(kr, 0, ck)
        for c in range(nchunk):
            cur = ss
            if c + 1 < nchunk:
                ss = scores(kr, c + 1, ck)
            vt = vtr[0:ATT_VROWS, c * ck:(c + 1) * ck]
            ps, alphas = [], []
            for hh in range(ATT_GRP):
                m_prev = m_sc[hh]
                m_new = jnp.maximum(m_prev, jnp.max(cur[hh], axis=0, keepdims=True))
                alpha = jnp.exp2(m_prev - m_new)
                p = jnp.exp2(cur[hh] - m_new)
                m_sc[hh] = m_new
                ps.append(p.astype(BF16))
                alphas.append(alpha)
            for hh in range(ATT_GRP):
                acc_sc[hh] = alphas[hh] * acc_sc[hh] + jnp.dot(vt, ps[hh],
                                                               preferred_element_type=F32)

    @pl.when(jnp.logical_not(is_ctx_q))
    def _():
        step(k_ref, vt_ref, ATT_TK)

    @pl.when(j == ATT_NK - 1)
    def _():
        step(kc_ref, vct_ref, CTX_LEN)
        for hh in range(ATT_GRP):
            acc = acc_sc[hh]
            o = acc[0:HEAD_DIM] / acc[HEAD_DIM:HEAD_DIM + 1]
            ms = jnp.sum(o * o, axis=0, keepdims=True) * (1.0 / HEAD_DIM)
            g = g_ref[hh * HB:hh * HB + HEAD_DIM, :]
            y = o * lax.rsqrt(ms + EPS) * jnp.concatenate([g] * (TM // LANES), axis=1)
            y = jnp.concatenate([y, jnp.zeros_like(y)], axis=0)
            o_ref[:, hh * HB:(hh + 1) * HB] = y.T.astype(BF16)


def _gqa(aqt, ak, avt, ga_cols, want_ctx):
    nq = LAT_TILES_PER_BATCH + (1 if want_ctx else 0)

    def q_tile(b, i):
        return jnp.where(i < LAT_TILES_PER_BATCH, b * LAT_TILES_PER_BATCH + i, LAT_TILES + b)

    est = (2 * ATT_GRP * HB * TM * 2 + 4 * ATT_TK * HB * 2 + 4 * TM * HB * 2
           + ATT_GRP * TM * HB * 4 * 3 + 8 * TM * ATT_TK * 4)
    return pl.pallas_call(
        _attn_kernel,
        out_shape=jax.ShapeDtypeStruct((T_ALL if want_ctx else T_LAT, N_ATT_HEADS * HB), BF16),
        grid=(BATCH, N_ATT_KV, nq, ATT_NK),
        in_specs=[
            pl.BlockSpec((ATT_GRP * HB, TM), lambda b, c, i, j: (c, q_tile(b, i))),
            pl.BlockSpec((ATT_TK, HB), lambda b, c, i, j: (b * ATT_NK + j, c)),
            pl.BlockSpec((HB, ATT_TK), lambda b, c, i, j: (c, b * ATT_NK + j)),
            pl.BlockSpec((CTX_LEN, HB), lambda b, c, i, j: (LAT_TILES + b, c)),
            pl.BlockSpec((HB, CTX_LEN), lambda b, c, i, j: (c, LAT_TILES + b)),
            pl.BlockSpec((ATT_GRP * HB, LANES), lambda b, c, i, j: (c, 0)),
        ],
        out_specs=pl.BlockSpec((TM, ATT_GRP * HB), lambda b, c, i, j: (q_tile(b, i), c)),
        scratch_shapes=[
            pltpu.VMEM((ATT_GRP, 1, TM), F32),
            pltpu.VMEM((ATT_GRP, ATT_VROWS, TM), F32),
        ],
        compiler_params=pltpu.CompilerParams(
            dimension_semantics=("arbitrary",) * 4, vmem_limit_bytes=_vmem_limit(est)),
        name="gqa_attn",
    )(aqt, ak, avt, ak, avt, ga_cols)


NA_BAND = NA_WIN_ROWS * GRID_W
NA_CLASSES = 8
NA_ROWS_PER_ITER = 8
_NA_CLASS_ROWS = (0, 1, 2, 3, GRID_ROWS // 2, GRID_ROWS - 3, GRID_ROWS - 2, GRID_ROWS - 1)


def _na_bias_table(rpb):
    wr, wc = NA_WIN_ROWS, NA_WIN_COLS
    r = np.asarray(_NA_CLASS_ROWS)
    ridx = np.clip(r - wr // 2, 0, GRID_ROWS - wr)[:, None] + np.arange(wr)[None, :]
    dr = ridx - r[:, None] + (wr - 1)
    col = np.arange(GRID_W)
    cstart = np.clip(col - wc // 2, 0, GRID_W - wc)
    col_ok = (col[None, :] >= cstart[:, None]) & (col[None, :] < cstart[:, None] + wc)
    dc = np.clip(col[None, :] - col[:, None] + (wc - 1), 0, 2 * wc - 2)
    pick_r = (dr[:, :, None] == np.arange(2 * wr - 1)[None, None, :]).astype(np.float32)
    pick_c = (dc[:, :, None] == np.arange(2 * wc - 1)[None, None, :]).astype(np.float32)
    hp = lax.Precision.HIGHEST
    by_col = jnp.einsum("hrc,qkc->hrqk", rpb.astype(F32), jnp.asarray(pick_c), precision=hp)
    bias = jnp.einsum("hrqk,cwr->hcqwk", by_col, jnp.asarray(pick_r), precision=hp)
    bias = bias.reshape(N_NA_HEADS, NA_CLASSES, GRID_W, NA_BAND)
    mask = np.tile(col_ok, (1, wr))
    return jnp.where(mask[None, None], bias, NEG_INF)


def _head_rms_gain(o, g):
    ms = jnp.sum(o * o, axis=-1, keepdims=True) * (1.0 / HEAD_DIM)
    return o * lax.rsqrt(ms + EPS) * g


def _na_kernel(q_ref, k_ref, v_ref, kc_ref, vc_ref, qc_ref, bias_ref, g_ref, o_ref, oc_ref,
               *, want_ctx):
    kc = kc_ref[...]
    vc = vc_ref[...]
    g = g_ref[0]
    half = NA_WIN_ROWS // 2
    last = GRID_ROWS - NA_WIN_ROWS

    def rows(it, carry):
        r0 = it * NA_ROWS_PER_ITER
        qrows, bands, scores = [], [], []
        for d in range(NA_ROWS_PER_ITER):
            r = r0 + d
            start = jnp.clip(r - half, 0, last)
            cls = jnp.where(r < half, r, jnp.where(r > last + half, r - last, half))
            qrow = pl.ds(pl.multiple_of(r * GRID_W, GRID_W), GRID_W)
            band = pl.ds(pl.multiple_of(start * GRID_W, GRID_W), NA_BAND)
            q = q_ref[qrow, :]
            s = lax.dot_general(q, k_ref[band, :], (((1,), (1,)), ((), ())),
                                preferred_element_type=F32)
            sc = lax.dot_general(q, kc, (((1,), (1,)), ((), ())), preferred_element_type=F32)
            qrows.append(qrow)
            bands.append(band)
            scores.append((s, sc, cls))
        probs = []
        for s, sc, cls in scores:
            bt = bias_ref[0, cls]
            s = jnp.where(bt > 0.5 * NEG_INF, s + bt, NEG_INF)
            m = jnp.maximum(jnp.max(s, axis=-1, keepdims=True),
                            jnp.max(sc, axis=-1, keepdims=True))
            p = jnp.exp(s - m)
            pc = jnp.exp(sc - m)
            l = jnp.sum(p, axis=-1, keepdims=True) + jnp.sum(pc, axis=-1, keepdims=True)
            probs.append((p.astype(BF16), pc.astype(BF16), l))
        for qrow, band, (p, pc, l) in zip(qrows, bands, probs):
            o = (jnp.dot(p, v_ref[band, :], preferred_element_type=F32)
                 + jnp.dot(pc, vc, preferred_element_type=F32)) / l
            o_ref[qrow, :] = _head_rms_gain(o, g).astype(BF16)
        return carry

    lax.fori_loop(0, GRID_ROWS // NA_ROWS_PER_ITER, rows, 0)

    if want_ctx:
        sc = lax.dot_general(qc_ref[...], kc, (((1,), (1,)), ((), ())),
                             preferred_element_type=F32)
        m = jnp.max(sc, axis=-1, keepdims=True)
        pc = jnp.exp(sc - m)
        l = jnp.sum(pc, axis=-1, keepdims=True)
        o = jnp.dot(pc.astype(BF16), vc, preferred_element_type=F32) / l
        oc_ref[...] = _head_rms_gain(o, g).astype(BF16)
    else:
        oc_ref[...] = jnp.zeros(oc_ref.shape, oc_ref.dtype)


def _neigh(nq, nk, nv, bias_tab, gn, want_ctx):
    lat = pl.BlockSpec((SEQ, HB), lambda b, h: (b, h))
    ctx = pl.BlockSpec((CTX_LEN, HB), lambda b, h: (T_LAT // CTX_LEN + b, h))
    est = 2 * (4 * SEQ * HB * 2 + 4 * CTX_LEN * HB * 2 + NA_CLASSES * GRID_W * NA_BAND * 4)
    return pl.pallas_call(
        functools.partial(_na_kernel, want_ctx=want_ctx),
        out_shape=[jax.ShapeDtypeStruct((T_LAT, N_NA_HEADS * HB), BF16),
                   jax.ShapeDtypeStruct((T_CTX, N_NA_HEADS * HB), BF16)],
        grid=(BATCH, N_NA_HEADS),
        in_specs=[lat, lat, lat, ctx, ctx, ctx,
                  pl.BlockSpec((1, NA_CLASSES, GRID_W, NA_BAND), lambda b, h: (h, 0, 0, 0)),
                  pl.BlockSpec((1, 1, HB), lambda b, h: (h, 0, 0))],
        out_specs=[pl.BlockSpec((SEQ, HB), lambda b, h: (b, h)),
                   pl.BlockSpec((CTX_LEN, HB), lambda b, h: (b, h))],
        compiler_params=pltpu.CompilerParams(
            dimension_semantics=("arbitrary", "arbitrary"), vmem_limit_bytes=_vmem_limit(est)),
        name="neigh_attn",
    )(nq, nk, nv, nk, nv, nq, bias_tab, gn)


RET_NCHUNK = SEQ // RET_CHUNK
RET_NCHUNK_CTX = CTX_LEN // RET_CHUNK
RET_CHUNKS_PER_ITER = 8


def _ret_tables(log_g2):
    lf = log_g2[0][:, None, None]
    lb = log_g2[1][:, None, None]
    pos = jnp.arange(RET_CHUNK, dtype=F32)
    i = pos[None, :, None]
    j = pos[None, None, :]
    diff = i - j
    dm = jnp.where(diff > 0, jnp.exp(lf * jnp.maximum(diff, 0.0)),
                   jnp.where(diff < 0, jnp.exp(lb * jnp.maximum(-diff, 0.0)), 2.0)) * 0.5
    fwd_lane = (jnp.arange(LANES) < HEAD_DIM)[None, None, :]
    xi = jnp.where(fwd_lane, jnp.exp(lf * (i + 1.0)), jnp.exp(lb * (RET_CHUNK - i)))
    zt = jnp.where(fwd_lane, jnp.exp(lf * (RET_CHUNK - 1.0 - i)), jnp.exp(lb * i))
    fwd_row = (jnp.arange(LANES) < HEAD_DIM)[None, :, None]
    dec = jnp.where(fwd_row, jnp.exp(lf * RET_CHUNK), jnp.exp(lb * RET_CHUNK))
    dec = jnp.broadcast_to(dec, (N_RET_HEADS, LANES, LANES))
    return dm.astype(F32), xi.astype(F32), zt.astype(F32), dec.astype(F32)


def _ret_kernel(q_ref, k_ref, v_ref, gt_ref, qc_ref, kc_ref, vc_ref, gtc_ref,
                dm_ref, xi_ref, zt_ref, dec_ref, g_ref, o_ref, oc_ref,
                u_sc, s_sc, uc_sc, sc_sc, *, want_ctx):
    dm = dm_ref[0]
    xi = xi_ref[0]
    zt = zt_ref[0]
    dec = dec_ref[0]
    dec_f = dec[0:HEAD_DIM]
    dec_b = dec[HEAD_DIM:LANES]
    g = g_ref[0]
    C = RET_CHUNK

    def chunk_rows(n):
        return pl.ds(pl.multiple_of(n * C, C), C)

    def chunk_state_update(kr, vr, usc, ns):
        kzs = [(kr[chunk_rows(n), :].astype(F32) * zt).T.astype(BF16) for n in ns]
        for n, kz in zip(ns, kzs):
            usc[n] = jnp.dot(kz, vr[chunk_rows(n), :], preferred_element_type=F32)

    def chunk_out(qr, kr, vr, gtr, ssc, outr, ns):
        qds = [qr[chunk_rows(n), :] for n in ns]
        s2s = [lax.dot_general(qd, kr[chunk_rows(n), :], (((1,), (1,)), ((), ())),
                               preferred_element_type=F32) for n, qd in zip(ns, qds)]
        outs = []
        for n, qd, s2 in zip(ns, qds, s2s):
            inner = jnp.dot((s2 * dm).astype(BF16), vr[chunk_rows(n), :],
                            preferred_element_type=F32)
            qx = (qd.astype(F32) * xi).astype(BF16)
            outs.append(inner + jnp.dot(qx, ssc[n].astype(BF16), preferred_element_type=F32))
        for n, o in zip(ns, outs):
            gate = gtr[chunk_rows(n), :].astype(F32)
            outr[chunk_rows(n), :] = (_head_rms_gain(o, g)
                                      * (gate * jax.nn.sigmoid(gate))).astype(BF16)

    def scan_states(usc, ssc, nchunk, init_f, init_b):
        def fwd(n, sf):
            ssc[n, 0:HEAD_DIM, :] = sf
            return dec_f * sf + usc[n, 0:HEAD_DIM, :]

        def bwd(t, sb):
            n = nchunk - 1 - t
            ssc[n, HEAD_DIM:LANES, :] = sb
            return dec_b * sb + usc[n, HEAD_DIM:LANES, :]

        return (lax.fori_loop(0, nchunk, fwd, init_f), lax.fori_loop(0, nchunk, bwd, init_b))

    zero = jnp.zeros((HEAD_DIM, LANES), F32)
    ctx_chunks = list(range(RET_NCHUNK_CTX))
    chunk_state_update(kc_ref, vc_ref, uc_sc, ctx_chunks)
    ctx_f, ctx_b = scan_states(uc_sc, sc_sc, RET_NCHUNK_CTX, zero, zero)
    if want_ctx:
        chunk_out(qc_ref, kc_ref, vc_ref, gtc_ref, sc_sc, oc_ref, ctx_chunks)
    else:
        oc_ref[...] = jnp.zeros(oc_ref.shape, oc_ref.dtype)

    def upd(it, carry):
        chunk_state_update(k_ref, v_ref, u_sc,
                           [it * RET_CHUNKS_PER_ITER + d for d in range(RET_CHUNKS_PER_ITER)])
        return carry

    lax.fori_loop(0, RET_NCHUNK // RET_CHUNKS_PER_ITER, upd, 0)
    scan_states(u_sc, s_sc, RET_NCHUNK, ctx_f, ctx_b)

    def out(it, carry):
        chunk_out(q_ref, k_ref, v_ref, gt_ref, s_sc, o_ref,
                  [it * RET_CHUNKS_PER_ITER + d for d in range(RET_CHUNKS_PER_ITER)])
        return carry

    lax.fori_loop(0, RET_NCHUNK // RET_CHUNKS_PER_ITER, out, 0)


def _retention(rq, rk, rv, rg, tables, gr, want_ctx):
    lat = pl.BlockSpec((SEQ, HB), lambda b, h: (b, h))
    ctx = pl.BlockSpec((CTX_LEN, HB), lambda b, h: (T_LAT // CTX_LEN + b, h))
    tab = pl.BlockSpec((1, LANES, LANES), lambda b, h: (h, 0, 0))
    est = (2 * 5 * SEQ * HB * 2 + 2 * RET_NCHUNK * LANES * LANES * 4 + 8 * LANES * LANES * 4)
    return pl.pallas_call(
        functools.partial(_ret_kernel, want_ctx=want_ctx),
        out_shape=[jax.ShapeDtypeStruct((T_LAT, N_RET_HEADS * HB), BF16),
                   jax.ShapeDtypeStruct((T_CTX, N_RET_HEADS * HB), BF16)],
        grid=(BATCH, N_RET_HEADS),
        in_specs=[lat, lat, lat, lat, ctx, ctx, ctx, ctx, tab, tab, tab, tab,
                  pl.BlockSpec((1, 1, HB), lambda b, h: (h, 0, 0))],
        out_specs=[pl.BlockSpec((SEQ, HB), lambda b, h: (b, h)),
                   pl.BlockSpec((CTX_LEN, HB), lambda b, h: (b, h))],
        scratch_shapes=[
            pltpu.VMEM((RET_NCHUNK, LANES, LANES), F32),
            pltpu.VMEM((RET_NCHUNK, LANES, LANES), F32),
            pltpu.VMEM((RET_NCHUNK_CTX, LANES, LANES), F32),
            pltpu.VMEM((RET_NCHUNK_CTX, LANES, LANES), F32),
        ],
        compiler_params=pltpu.CompilerParams(
            dimension_semantics=("arbitrary", "arbitrary"), vmem_limit_bytes=_vmem_limit(est)),
        name="retention",
    )(rq, rk, rv, rg, rq, rk, rv, rg, *tables, gr)


_YA_W = N_ATT_HEADS * HB
_YN_W = N_NA_HEADS * HB
_YR_W = N_RET_HEADS * HB


MOE_SLOT = TM
META_ROWS = 8
_META_GATE0 = 2


def _route_t(logt):
    row = lax.broadcasted_iota(jnp.int32, logt.shape, 0).astype(F32)
    p = jnp.exp(logt - jnp.max(logt, axis=0, keepdims=True))
    best = None
    for grp in range(N_EXPERT_GROUPS):
        lo = float(grp * EXPERTS_PER_GROUP)
        ing = (row >= lo) & (row < lo + EXPERTS_PER_GROUP)
        pg = jnp.where(ing, p, -1.0)
        m1 = jnp.max(pg, axis=0, keepdims=True)
        i1 = jnp.min(jnp.where(pg == m1, row, float(N_EXPERTS)), axis=0, keepdims=True)
        pg2 = jnp.where(row == i1, -1.0, pg)
        m2 = jnp.max(pg2, axis=0, keepdims=True)
        i2 = jnp.min(jnp.where(pg2 == m2, row, float(N_EXPERTS)), axis=0, keepdims=True)
        cand = (m1 + m2, m1, m2, i1, i2, jnp.zeros_like(m1) + lo)
        if best is None:
            best = cand
        else:
            better = cand[0] > best[0]
            best = tuple(jnp.where(better, c, b) for c, b in zip(cand, best))
    _, m1, m2, i1, i2, base = best
    w = m1 + m2
    gates = [jnp.where(i1 == base + e, m1 / w, jnp.where(i2 == base + e, m2 / w, 0.0))
             for e in range(EXPERTS_PER_GROUP)]
    return base * (1.0 / EXPERTS_PER_GROUP), gates


def _merge_kernel(ya_ref, ynl_ref, ync_ref, yrl_ref, yrc_ref, xl_ref, xc_ref, mod_ref, g2_ref,
                  wo_ref, wr_ref, brt_ref,
                  xn_ref, h2_ref, metat_ref, metac_ref, ctab_ref, tot_ref, carry_sc, *, split_ctx):
    @pl.when(pl.program_id(0) == 0)
    def _():
        carry_sc[...] = jnp.zeros(carry_sc.shape, F32)

    mod = mod_ref[0]
    gt1 = mod[:, 2 * D_MODEL:3 * D_MODEL]
    sh2 = mod[:, 3 * D_MODEL:4 * D_MODEL]
    sc2 = mod[:, 4 * D_MODEL:5 * D_MODEL]
    yn = _pick_rows(ynl_ref, ync_ref, split_ctx)
    yr = _pick_rows(yrl_ref, yrc_ref, split_ctx)
    m = (jnp.dot(ya_ref[...], wo_ref[0:_YA_W, :], preferred_element_type=F32)
         + jnp.dot(yn, wo_ref[_YA_W:_YA_W + _YN_W, :], preferred_element_type=F32)
         + jnp.dot(yr, wo_ref[_YA_W + _YN_W:MIX_PAD, :], preferred_element_type=F32))
    x = _pick_rows(xl_ref, xc_ref, split_ctx) + gt1 * m
    xn_ref[...] = x
    ms = jnp.mean(x * x, axis=-1, keepdims=True)
    h2 = x * lax.rsqrt(ms + EPS) * g2_ref[...] * (1.0 + sc2) + sh2
    h2_ref[...] = h2.astype(BF16)

    ntile = TP // LANES
    h_hi = h2.astype(BF16)
    h_lo = (h2 - h_hi.astype(F32)).astype(BF16)
    wr = wr_ref[...]
    w_hi = wr.astype(BF16)
    w_lo = (wr - w_hi.astype(F32)).astype(BF16)
    logits = (jnp.dot(h_hi, w_hi, preferred_element_type=F32)
              + jnp.dot(h_lo, w_hi, preferred_element_type=F32)
              + jnp.dot(h_hi, w_lo, preferred_element_type=F32))
    logt = logits.T[0:N_EXPERTS, :]
    gsel, gates = _route_t(logt + jnp.concatenate([brt_ref[...]] * ntile, axis=1))

    grow = lax.broadcasted_iota(jnp.int32, (META_ROWS, TP), 0).astype(F32)
    onehot = jnp.where(grow == gsel, 1.0, 0.0)
    earlier = (lax.broadcasted_iota(jnp.int32, (TP, TP), 0)
               < lax.broadcasted_iota(jnp.int32, (TP, TP), 1))
    excl = jnp.dot(onehot.astype(BF16), jnp.where(earlier, 1.0, 0.0).astype(BF16),
                   preferred_element_type=F32)
    carry = carry_sc[...]
    rank = jnp.sum(onehot * (jnp.concatenate([carry] * ntile, axis=1) + excl),
                   axis=0, keepdims=True)
    for sub in range(TP // TM):
        ctab_ref[sub] = carry
        carry = carry + jnp.sum(onehot[:, sub * TM:(sub + 1) * TM], axis=1, keepdims=True)
    carry_sc[...] = carry
    tot_ref[...] = carry

    metat = jnp.concatenate([gsel, rank] + gates
                            + [jnp.zeros((META_ROWS - _META_GATE0 - EXPERTS_PER_GROUP, TP), F32)],
                            axis=0)
    metat_ref[...] = metat
    metac_ref[...] = jnp.concatenate([metat, jnp.zeros((LANES - META_ROWS, TP), F32)], axis=0).T


def _merge(ya, yn_lat, yn_ctx, yr_lat, yr_ctx, x_lat, x_ctx, split_ctx, mod3, g2, wo_pad, wr_pad,
           brt, ntiles):
    rows = ntiles * TP
    est = (2 * MIX_PAD * D_MODEL * 2 + 4 * TP * MIX_PAD * 2 + 10 * TP * D_MODEL * 4
           + 4 * TP * TP * 4)
    const = lambda i: (0, 0)
    return pl.pallas_call(
        functools.partial(_merge_kernel, split_ctx=split_ctx),
        out_shape=[jax.ShapeDtypeStruct((rows, D_MODEL), F32),
                   jax.ShapeDtypeStruct((rows, D_MODEL), BF16),
                   jax.ShapeDtypeStruct((META_ROWS, rows), F32),
                   jax.ShapeDtypeStruct((rows, LANES), F32),
                   jax.ShapeDtypeStruct((ntiles * (TP // TM), META_ROWS, LANES), F32),
                   jax.ShapeDtypeStruct((META_ROWS, LANES), F32)],
        grid=(ntiles,),
        in_specs=([pl.BlockSpec((TP, _YA_W), lambda i: (i, 0))]
                  + _lat_ctx_specs(_YN_W, split_ctx) + _lat_ctx_specs(_YR_W, split_ctx)
                  + _lat_ctx_specs(D_MODEL, split_ctx) + [
            pl.BlockSpec((1, 1, 6 * D_MODEL), lambda i: (_ptile_mod_row(i), 0, 0)),
            pl.BlockSpec((1, D_MODEL), const),
            pl.BlockSpec((MIX_PAD, D_MODEL), const),
            pl.BlockSpec((D_MODEL, LANES), const),
            pl.BlockSpec((N_EXPERTS, LANES), const),
        ]),
        out_specs=[pl.BlockSpec((TP, D_MODEL), lambda i: (i, 0)),
                   pl.BlockSpec((TP, D_MODEL), lambda i: (i, 0)),
                   pl.BlockSpec((META_ROWS, TP), lambda i: (0, i)),
                   pl.BlockSpec((TP, LANES), lambda i: (i, 0)),
                   pl.BlockSpec((TP // TM, META_ROWS, LANES), lambda i: (i, 0, 0)),
                   pl.BlockSpec((META_ROWS, LANES), const)],
        scratch_shapes=[pltpu.VMEM((META_ROWS, LANES), F32)],
        compiler_params=pltpu.CompilerParams(
            dimension_semantics=("arbitrary",), vmem_limit_bytes=_vmem_limit(est)),
        name="merge_outproj_router",
    )(ya, yn_lat, yn_ctx, yr_lat, yr_ctx, x_lat, x_ctx, mod3, g2, wo_pad, wr_pad, brt)


def _moe_plan(ctab, tot, ntiles):
    grp = N_EXPERT_GROUPS
    i32 = jnp.int32
    a = ctab[:, :grp, 0].astype(i32).T
    totg = tot[:grp, 0].astype(i32)
    b = jnp.concatenate([a[:, 1:], totg[:, None]], axis=1)
    nslot = (totg + MOE_SLOT - 1) // MOE_SLOT
    slot_end = jnp.cumsum(nslot)
    slot_base = slot_end - nslot
    total_slots = slot_end[-1]
    nchunk = ntiles // (MOE_CHUNK // TM)
    ac = a[:, ::MOE_CHUNK // TM]
    bc = jnp.concatenate([ac[:, 1:], totg[:, None]], axis=1)
    first_j = ac // MOE_SLOT
    last_j = (jnp.maximum(bc, 1) - 1) // MOE_SLOT
    npairs = jnp.where(bc > ac, last_j - first_j + 1, 0).reshape(-1)
    cum = jnp.cumsum(npairs)
    start = cum - npairs
    total_pairs = cum[-1]
    n_pairs_max = grp * nchunk + ntiles + grp
    pidx = jnp.arange(n_pairs_max, dtype=i32)
    p = jnp.minimum(pidx, total_pairs - 1)
    gc = jnp.sum((cum[None, :] <= p[:, None]).astype(i32), axis=1)
    slot = slot_base[gc // nchunk] + first_j.reshape(-1)[gc] + (p - start[gc])
    valid = pidx < total_pairs
    prev_slot = jnp.concatenate([jnp.full((1,), -1, i32), slot[:-1]])
    next_slot = jnp.concatenate([slot[1:], jnp.full((1,), -1, i32)])
    is_first = valid & (slot != prev_slot)
    is_last = valid & ((slot != next_slot) | (pidx == total_pairs - 1))
    flags = is_first.astype(i32) + 2 * is_last.astype(i32) + 4 * valid.astype(i32)
    n_slots_max = ntiles + grp
    sidx = jnp.arange(n_slots_max, dtype=i32)
    sgrp = jnp.minimum(jnp.sum((slot_end[None, :] <= sidx[:, None]).astype(i32), axis=1), grp - 1)
    sr0 = (sidx - slot_base[sgrp]) * MOE_SLOT
    row0 = slot_base[:, None] * MOE_SLOT + a
    row1 = slot_base[:, None] * MOE_SLOT + jnp.maximum(b, a + 1) - 1
    wb = jnp.stack([row0 // MOE_SLOT, row1 // MOE_SLOT], axis=1)
    wb = jnp.clip(wb, 0, total_slots - 1).reshape(-1).astype(i32)
    return dict(pslot=slot.astype(i32), pchunk=(gc % nchunk).astype(i32), pflag=flags,
                sgrp=sgrp, sr0=sr0.astype(i32), wb=wb, sbase=slot_base.astype(i32),
                n_pairs=n_pairs_max, n_slots=n_slots_max)


def _moe_kernel(pslot_ref, pchunk_ref, pflag_ref, sgrp_ref, sr0_ref,
                h_ref, metat_ref, metac_ref, w1_ref, w3_ref, w2_ref, o_ref, x_sc, g_sc):
    p = pl.program_id(0)
    flags = pflag_ref[p]
    slot = pslot_ref[p]

    @pl.when((flags & 1) != 0)
    def _():
        x_sc[...] = jnp.zeros(x_sc.shape, F32)
        g_sc[...] = jnp.zeros(g_sc.shape, F32)

    @pl.when((flags & 4) != 0)
    def _():
        mt = metat_ref[...].astype(jnp.int32)
        want = lax.broadcasted_iota(jnp.int32, (MOE_SLOT, MOE_CHUNK), 0) + sr0_ref[slot]
        sel = (mt[1:2, :] == want) & (mt[0:1, :] == sgrp_ref[slot])
        pm = jnp.where(sel, 1.0, 0.0).astype(BF16)
        x_sc[...] += jnp.dot(pm, h_ref[...], preferred_element_type=F32)
        mc = metac_ref[...]
        hi = mc.astype(BF16)
        r1 = mc - hi.astype(F32)
        mid = r1.astype(BF16)
        lo = (r1 - mid.astype(F32)).astype(BF16)
        g_sc[...] += (jnp.dot(pm, hi, preferred_element_type=F32)
                      + jnp.dot(pm, mid, preferred_element_type=F32)
                      + jnp.dot(pm, lo, preferred_element_type=F32))

    @pl.when((flags & 2) != 0)
    def _():
        x = x_sc[...].astype(BF16)
        gs = g_sc[...]
        acc = jnp.zeros((MOE_SLOT, D_MODEL), F32)
        for e in range(EXPERTS_PER_GROUP):
            a = jnp.dot(x, w1_ref[0, 0, e], preferred_element_type=F32)
            b = jnp.dot(x, w3_ref[0, 0, e], preferred_element_type=F32)
            gate = gs[:, _META_GATE0 + e:_META_GATE0 + e + 1]
            act = (a * jax.nn.sigmoid(a)) * b * gate
            acc = acc + jnp.dot(act.astype(BF16), w2_ref[0, 0, e], preferred_element_type=F32)
        o_ref[...] = acc.astype(BF16)


def _moe_sorted(plan, h2, metat, metac, w1g, w3g, w2g, layer):
    est = (2 * 3 * EXPERTS_PER_GROUP * D_MODEL * D_EXPERT * 2 + 6 * MOE_CHUNK * D_MODEL * 2
           + 2 * MOE_SLOT * D_MODEL * 4 + 6 * MOE_SLOT * D_EXPERT * 4
           + 4 * MOE_CHUNK * MOE_SLOT * 4)
    wmap = lambda p, ps, pc, pf, sg, sr: (layer, sg[ps[p]], 0, 0, 0)
    grid_spec = pltpu.PrefetchScalarGridSpec(
        num_scalar_prefetch=5,
        grid=(plan["n_pairs"],),
        in_specs=[
            pl.BlockSpec((MOE_CHUNK, D_MODEL), lambda p, ps, pc, pf, sg, sr: (pc[p], 0)),
            pl.BlockSpec((META_ROWS, MOE_CHUNK), lambda p, ps, pc, pf, sg, sr: (0, pc[p])),
            pl.BlockSpec((MOE_CHUNK, LANES), lambda p, ps, pc, pf, sg, sr: (pc[p], 0)),
            pl.BlockSpec((1, 1, EXPERTS_PER_GROUP, D_MODEL, D_EXPERT), wmap),
            pl.BlockSpec((1, 1, EXPERTS_PER_GROUP, D_MODEL, D_EXPERT), wmap),
            pl.BlockSpec((1, 1, EXPERTS_PER_GROUP, D_EXPERT, D_MODEL), wmap),
        ],
        out_specs=pl.BlockSpec((MOE_SLOT, D_MODEL), lambda p, ps, pc, pf, sg, sr: (ps[p], 0)),
        scratch_shapes=[pltpu.VMEM((MOE_SLOT, D_MODEL), F32), pltpu.VMEM((MOE_SLOT, LANES), F32)],
    )
    return pl.pallas_call(
        _moe_kernel,
        out_shape=jax.ShapeDtypeStruct((plan["n_slots"] * MOE_SLOT, D_MODEL), BF16),
        grid_spec=grid_spec,
        compiler_params=pltpu.CompilerParams(
            dimension_semantics=("arbitrary",), vmem_limit_bytes=_vmem_limit(est)),
        name="moe_sorted_experts",
    )(plan["pslot"], plan["pchunk"], plan["pflag"], plan["sgrp"], plan["sr0"],
      h2, metat, metac, w1g, w3g, w2g)


_COMBINE_WINDOWS = 2 * N_EXPERT_GROUPS


def _combine_kernel(wb_ref, sb_ref, *refs, ntiles, final):
    y_refs = refs[:_COMBINE_WINDOWS]
    metac_ref, xn_ref, mod_ref, fg_ref, o_ref = refs[_COMBINE_WINDOWS:]
    i = pl.program_id(0)
    mc = metac_ref[...]
    grp = mc[:, 0:1].astype(jnp.int32)
    rank = mc[:, 1:2].astype(jnp.int32)
    col = lax.broadcasted_iota(jnp.int32, (TM, MOE_SLOT), 1)
    m = jnp.zeros((TM, D_MODEL), F32)
    for g in range(N_EXPERT_GROUPS):
        row = rank + sb_ref[g] * MOE_SLOT
        blk0 = wb_ref[(2 * g) * ntiles + i]
        for k in range(2):
            blk = wb_ref[(2 * g + k) * ntiles + i]
            sel = (grp == g) & (row == col + blk * MOE_SLOT)
            if k == 1:
                sel = jnp.logical_and(sel, blk != blk0)
            m = m + jnp.dot(jnp.where(sel, 1.0, 0.0).astype(BF16), y_refs[2 * g + k][...],
                            preferred_element_type=F32)
    gt2 = mod_ref[0][:, 5 * D_MODEL:6 * D_MODEL]
    x = xn_ref[...] + gt2 * m
    if final:
        ms = jnp.mean(x * x, axis=-1, keepdims=True)
        x = x * lax.rsqrt(ms + EPS) * fg_ref[...]
    o_ref[...] = x


def _combine(plan, ys, metac, xn, mod3, fg, ntiles, final):
    def ymap(w):
        return lambda i, wb, sb: (wb[w * ntiles + i], 0)

    est = (2 * _COMBINE_WINDOWS * MOE_SLOT * D_MODEL * 2 + 8 * TM * D_MODEL * 4
           + 4 * TM * MOE_SLOT * 4)
    grid_spec = pltpu.PrefetchScalarGridSpec(
        num_scalar_prefetch=2,
        grid=(ntiles,),
        in_specs=[pl.BlockSpec((MOE_SLOT, D_MODEL), ymap(w)) for w in range(_COMBINE_WINDOWS)] + [
            pl.BlockSpec((TM, LANES), lambda i, wb, sb: (i, 0)),
            pl.BlockSpec((TM, D_MODEL), lambda i, wb, sb: (i, 0)),
            pl.BlockSpec((1, 1, 6 * D_MODEL), lambda i, wb, sb: (_tile_mod_row(i), 0, 0)),
            pl.BlockSpec((1, D_MODEL), lambda i, wb, sb: (0, 0)),
        ],
        out_specs=pl.BlockSpec((TM, D_MODEL), lambda i, wb, sb: (i, 0)),
    )
    return pl.pallas_call(
        functools.partial(_combine_kernel, ntiles=ntiles, final=final),
        out_shape=jax.ShapeDtypeStruct((ntiles * TM, D_MODEL), F32),
        grid_spec=grid_spec,
        compiler_params=pltpu.CompilerParams(
            dimension_semantics=("arbitrary",), vmem_limit_bytes=_vmem_limit(est)),
        name="moe_combine",
    )(plan["wb"], plan["sbase"], *([ys] * _COMBINE_WINDOWS), metac, xn, mod3, fg)


def _rope_tables():
    t = np.arange(SEQ)
    nf = HEAD_DIM // 4
    inv = (np.float32(ROPE_THETA) ** (-np.arange(nf, dtype=np.float32) / np.float32(nf)))
    inv = inv.astype(np.float32)
    ang_r = (t // GRID_W).astype(np.float32)[:, None] * inv[None, :]
    ang_c = (t % GRID_W).astype(np.float32)[:, None] * inv[None, :]
    cr, sr, cc, sc = np.cos(ang_r), np.sin(ang_r), np.cos(ang_c), np.sin(ang_c)
    zeros = np.zeros((SEQ, HEAD_DIM), np.float32)
    cs = np.concatenate([cr, cr, cc, cc, zeros], axis=-1)
    sn = np.concatenate([-sr, sr, -sc, sc, zeros], axis=-1)
    ident = np.concatenate([np.ones((T_CTX, HEAD_DIM), np.float32),
                            np.zeros((T_CTX, HEAD_DIM), np.float32)], axis=-1)
    cs = np.concatenate([cs, ident], axis=0).astype(np.float32)
    sn = np.concatenate([sn, np.zeros((T_CTX, LANES), np.float32)], axis=0).astype(np.float32)
    return jnp.asarray(cs), jnp.asarray(sn)


def _pad_out_weight(w_out_l):
    w = w_out_l.reshape(-1, HEAD_DIM, D_MODEL)
    w = jnp.concatenate([w, jnp.zeros_like(w)], axis=1)
    return w.reshape(MIX_PAD, D_MODEL).astype(BF16)


def _pad_in_weight(w_in_l):
    parts = []
    for _, src, heads, mode in _SECTIONS:
        w = w_in_l[:, src:src + heads * HEAD_DIM].reshape(D_MODEL, heads, HEAD_DIM)
        other = w if mode == "dup" else jnp.zeros_like(w)
        parts.append(jnp.concatenate([w, other], axis=-1).reshape(D_MODEL, heads * HB))
    return jnp.concatenate(parts, axis=-1).astype(BF16)


def kernel(x, c, ctx, c_ctx, w_ada, b_ada, norm1_g, norm2_g, w_in, q_norm_g, k_norm_g, na_rpb,
           ret_decay, mix_g, w_out, w_router, b_router, w_exp1, w_exp3, w_exp2, final_g):
    cs_tab, sn_tab = _rope_tables()

    cvec = jnp.concatenate([c, c_ctx[None, :], jnp.zeros((8 - BATCH - 1, D_MODEL), F32)], axis=0)
    mod_all = _ada_mod(cvec, w_ada, b_ada)

    wr_pad = jnp.concatenate([w_router, jnp.zeros((D_MODEL, LANES - N_EXPERTS), F32)], axis=1)
    brt = jnp.broadcast_to(b_router[:, None], (N_EXPERTS, LANES))
    zero_lane = jnp.zeros((HEAD_DIM,), F32)

    x_lat = x.reshape(T_LAT, D_MODEL)
    x_ctx = ctx.reshape(T_CTX, D_MODEL)

    gshape = (DEPTH, N_EXPERT_GROUPS, EXPERTS_PER_GROUP)
    w1g = w_exp1.astype(BF16).reshape(gshape + (D_MODEL, D_EXPERT))
    w3g = w_exp3.astype(BF16).reshape(gshape + (D_MODEL, D_EXPERT))
    w2g = w_exp2.astype(BF16).reshape(gshape + (D_EXPERT, D_MODEL))

    for l in range(DEPTH):
        last = l == DEPTH - 1
        want_ctx = not last
        split_ctx = l == 0
        mod3 = mod_all[l].reshape(8, 1, 6 * D_MODEL)
        w_pad = _pad_in_weight(w_in[l])
        qg = jnp.concatenate([q_norm_g[l], zero_lane])[None, :]
        kg = jnp.concatenate([k_norm_g[l], zero_lane])[None, :]
        aq, ak, av, nq, nk, nv, rq, rk, rv, rg = _inproj(
            x_lat, x_ctx, split_ctx, mod3, norm1_g[l][None, :], w_pad, cs_tab, sn_tab, qg, kg)

        ga = jnp.broadcast_to(mix_g[l][:ATT_Q].reshape(N_ATT_HEADS, HEAD_DIM, 1),
                              (N_ATT_HEADS, HEAD_DIM, LANES))
        ga = jnp.concatenate([ga, jnp.zeros_like(ga)], axis=1).reshape(N_ATT_HEADS * HB, LANES)
        gn = _pad_heads(mix_g[l][ATT_Q:ATT_Q + NA_W], N_NA_HEADS)
        gr = _pad_heads(mix_g[l][ATT_Q + NA_W:], N_RET_HEADS)

        ya = _gqa(aq, ak, av, ga, want_ctx)
        yn_lat, yn_ctx = _neigh(nq, nk, nv, _na_bias_table(na_rpb[l]), gn, want_ctx)
        log_g2 = jax.nn.log_sigmoid(ret_decay[l].astype(F32))
        yr_lat, yr_ctx = _retention(rq, rk, rv, rg, _ret_tables(log_g2), gr, want_ctx)

        wo_pad = _pad_out_weight(w_out[l])
        ntiles = LAT_TILES if last else ALL_TILES
        xn, h2, metat, metac, ctab, tot = _merge(
            ya, yn_lat, yn_ctx, yr_lat, yr_ctx, x_lat, x_ctx, split_ctx, mod3,
            norm2_g[l][None, :], wo_pad, wr_pad, brt, ntiles // (TP // TM))
        plan = _moe_plan(ctab, tot, ntiles)
        ys = _moe_sorted(plan, h2, metat, metac, w1g, w3g, w2g, l)
        x_lat = _combine(plan, ys, metac, xn, mod3, final_g[None, :], ntiles, last)
        x_ctx = x_lat

    return x_lat.reshape(BATCH, SEQ, D_MODEL)
```

```python
import functools

import numpy as np
import jax
import jax.numpy as jnp
from jax import lax
from jax.experimental import pallas as pl
from jax.experimental.pallas import tpu as pltpu

D_MODEL = 1024
BATCH = 2
SEQ = 8192
DEPTH = 2
GRID_W = 64
GRID_ROWS = SEQ // GRID_W
CTX_LEN = 256
HEAD_DIM = 64
N_ATT_HEADS = 6
N_ATT_KV = 2
ATT_GRP = N_ATT_HEADS // N_ATT_KV
N_NA_HEADS = 4
N_RET_HEADS = 6
ATT_Q = N_ATT_HEADS * HEAD_DIM
ATT_KV = N_ATT_KV * HEAD_DIM
NA_W = N_NA_HEADS * HEAD_DIM
RET_W = N_RET_HEADS * HEAD_DIM
NA_WIN_ROWS = 8
NA_WIN_COLS = 16
RET_CHUNK = 128
ROPE_THETA = 10000.0
N_EXPERTS = 16
N_EXPERT_GROUPS = 4
EXPERTS_PER_GROUP = N_EXPERTS // N_EXPERT_GROUPS
D_EXPERT = 512
EPS = 1e-6
NEG_INF = -1e30

LANES = 128
VMEM_LIMIT_CAP = 56 * 1024 * 1024

T_LAT = BATCH * SEQ
T_CTX = BATCH * CTX_LEN
T_ALL = T_LAT + T_CTX
TM = 256
LAT_TILES_PER_BATCH = SEQ // TM
LAT_TILES = T_LAT // TM
CTX_TILES = T_CTX // TM
ALL_TILES = LAT_TILES + CTX_TILES
TP = TM
MOE_CHUNK = 2 * TM
P_LAT_TILES_PER_BATCH = SEQ // TP
P_LAT_TILES = T_LAT // TP
P_ALL_TILES = T_ALL // TP
HB = LANES

F32 = jnp.float32
BF16 = jnp.bfloat16

_SECTIONS = (
    ("aq", 0, N_ATT_HEADS, "pad"),
    ("ak", ATT_Q, N_ATT_KV, "pad"),
    ("av", ATT_Q + ATT_KV, N_ATT_KV, "pad"),
    ("nq", ATT_Q + 2 * ATT_KV, N_NA_HEADS, "par"),
    ("nk", ATT_Q + 2 * ATT_KV + NA_W, N_NA_HEADS, "dense"),
    ("nv", ATT_Q + 2 * ATT_KV + 2 * NA_W, N_NA_HEADS, "dense"),
    ("rq", ATT_Q + 2 * ATT_KV + 3 * NA_W, N_RET_HEADS, "dup"),
    ("rk", ATT_Q + 2 * ATT_KV + 3 * NA_W + RET_W, N_RET_HEADS, "dup"),
    ("rv", ATT_Q + 2 * ATT_KV + 3 * NA_W + 2 * RET_W, N_RET_HEADS, "dense"),
    ("rg", ATT_Q + 2 * ATT_KV + 3 * NA_W + 3 * RET_W, N_RET_HEADS, "dense"),
)
_SEC_OFF = {}
_off = 0
for _name, _src, _heads, _mode in _SECTIONS:
    _width = _heads * (HEAD_DIM if _mode == "dense" else HB)
    _SEC_OFF[_name] = (_off, _width)
    _off += _width
NC_PAD = _off
_FEATURE_MAJOR = ("aq", "av")
LOG2E = 1.4426950408889634
_YA_W = N_ATT_HEADS * HB
_YN_W = NA_W
_YR_W = RET_W
MIX_PAD = _YA_W + _YN_W + _YR_W


def _vmem_limit(nbytes):
    return int(min(VMEM_LIMIT_CAP, max(16 * 1024 * 1024, 2 * nbytes)))


def _pair_gains(v, heads):
    v = v.reshape(heads, 1, HEAD_DIM).astype(F32)
    even = (np.arange(heads) % 2 == 0)[:, None, None]
    zero = jnp.zeros_like(v)
    return jnp.concatenate([jnp.where(even, v, zero), jnp.where(even, zero, v)], axis=-1)


def _tile_mod_row(i, per_batch=LAT_TILES_PER_BATCH):
    return jnp.where(i < per_batch, 0, jnp.where(i < BATCH * per_batch, 1, 2))


def _ptile_mod_row(i):
    return _tile_mod_row(i, P_LAT_TILES_PER_BATCH)


ADA_TN = 1536


def _ada_kernel(c_ref, w_ref, b_ref, o_ref):
    c = c_ref[...]
    s = c * jax.nn.sigmoid(c)
    o_ref[0] = jnp.dot(s, w_ref[0], preferred_element_type=F32,
                       precision=lax.Precision.HIGHEST) + b_ref[0]


def _ada_mod(cvec, w_ada, b_ada):
    n = 6 * D_MODEL
    return pl.pallas_call(
        _ada_kernel,
        out_shape=jax.ShapeDtypeStruct((DEPTH, 8, n), F32),
        grid=(DEPTH, n // ADA_TN),
        in_specs=[
            pl.BlockSpec((8, D_MODEL), lambda l, j: (0, 0)),
            pl.BlockSpec((1, D_MODEL, ADA_TN), lambda l, j: (l, 0, j)),
            pl.BlockSpec((1, 1, ADA_TN), lambda l, j: (l, 0, j)),
        ],
        out_specs=pl.BlockSpec((1, 8, ADA_TN), lambda l, j: (l, 0, j)),
        compiler_params=pltpu.CompilerParams(
            dimension_semantics=("arbitrary", "arbitrary"),
            vmem_limit_bytes=_vmem_limit(2 * D_MODEL * ADA_TN * 4)),
        name="ada_mod",
    )(cvec, w_ada, b_ada.reshape(DEPTH, 1, n))


def _rope_swap(t):
    lane = lax.broadcasted_iota(jnp.int32, t.shape, 1)
    first_half = (lane % 32) < 16
    return jnp.where(first_half, pltpu.roll(t, LANES - 16, 1), pltpu.roll(t, 16, 1))


def _pick_rows(lat_ref, ctx_ref, split_ctx):
    if not split_ctx:
        return lat_ref[...]
    return jnp.where(pl.program_id(0) >= P_LAT_TILES, ctx_ref[...], lat_ref[...])


def _lat_ctx_specs(width, split_ctx):
    if split_ctx:
        return [pl.BlockSpec((TP, width), lambda i, *_: (jnp.minimum(i, P_LAT_TILES - 1), 0)),
                pl.BlockSpec((TP, width), lambda i, *_: (jnp.maximum(i - P_LAT_TILES, 0), 0))]
    return [pl.BlockSpec((TP, width), lambda i, *_: (i, 0)),
            pl.BlockSpec((TP, width), lambda i, *_: (0, 0))]


def _inproj_kernel(xl_ref, xc_ref, mod_ref, g1_ref, w_ref, cs_ref, sn_ref, qg_ref, kg_ref,
                   aq_ref, ak_ref, av_ref, nq_ref, nk_ref, nv_ref,
                   rq_ref, rk_ref, rv_ref, rg_ref, *, split_ctx):
    x = _pick_rows(xl_ref, xc_ref, split_ctx)
    mod = mod_ref[0]
    sh1 = mod[:, 0:D_MODEL]
    sc1 = mod[:, D_MODEL:2 * D_MODEL]
    ms = jnp.mean(x * x, axis=-1, keepdims=True)
    h = x * lax.rsqrt(ms + EPS) * g1_ref[...]
    h = (h * (1.0 + sc1) + sh1).astype(BF16)
    cs = cs_ref[...]
    sn = sn_ref[...]

    sections = {}

    def proj(name, hidx):
        if name not in sections:
            off, width = _SEC_OFF[name]
            sections[name] = jnp.dot(h, w_ref[:, off:off + width], preferred_element_type=F32)
        return sections[name][:, hidx * HB:(hidx + 1) * HB]

    def normed_rope(z, g):
        ss = jnp.sum(z * z, axis=-1, keepdims=True)
        zn = z * lax.rsqrt(ss * (1.0 / HEAD_DIM) + EPS) * g
        return zn * cs + _rope_swap(zn) * sn

    scale = HEAD_DIM ** -0.5
    for hh in range(N_ATT_HEADS):
        z = normed_rope(proj("aq", hh), qg_ref[...]) * (scale * LOG2E)
        aq_ref[hh * HB:(hh + 1) * HB, :] = z.T.astype(BF16)
    for hh in range(N_ATT_KV):
        z = normed_rope(proj("ak", hh), kg_ref[...])
        ak_ref[:, hh * HB:(hh + 1) * HB] = z.astype(BF16)
        zv = proj("av", hh)
        lane = lax.broadcasted_iota(jnp.int32, zv.shape, 1)
        av_ref[hh * HB:(hh + 1) * HB, :] = jnp.where(lane == HEAD_DIM, 1.0, zv).T.astype(BF16)
    def whole(name):
        off, width = _SEC_OFF[name]
        return jnp.dot(h, w_ref[:, off:off + width], preferred_element_type=F32)

    nq_ref[...] = (whole("nq") * scale).astype(BF16)
    nk_ref[...] = whole("nk").astype(BF16)
    nv_ref[...] = whole("nv").astype(BF16)
    rq_ref[...] = whole("rq").astype(BF16)
    rk_ref[...] = (whole("rk") * scale).astype(BF16)
    rv_ref[...] = whole("rv").astype(BF16)
    rg_ref[...] = whole("rg").astype(BF16)


def _inproj(x_lat, x_ctx, split_ctx, mod3, g1, w_pad, cs_tab, sn_tab, qg, kg):
    names = [s[0] for s in _SECTIONS]
    widths = [_SEC_OFF[n][1] for n in names]

    def tab_map(i):
        return (jnp.where(i < P_LAT_TILES, i % P_LAT_TILES_PER_BATCH, P_LAT_TILES_PER_BATCH), 0)

    est = (D_MODEL * NC_PAD * 2 + 4 * TP * D_MODEL * 4 + 2 * TP * NC_PAD * 2
           + 6 * TP * D_MODEL * 4)
    return pl.pallas_call(
        functools.partial(_inproj_kernel, split_ctx=split_ctx),
        out_shape=[jax.ShapeDtypeStruct((w, T_ALL) if n in _FEATURE_MAJOR else (T_ALL, w), BF16)
                   for n, w in zip(names, widths)],
        grid=(P_ALL_TILES,),
        in_specs=_lat_ctx_specs(D_MODEL, split_ctx) + [
            pl.BlockSpec((1, 1, 6 * D_MODEL), lambda i: (_ptile_mod_row(i), 0, 0)),
            pl.BlockSpec((1, D_MODEL), lambda i: (0, 0)),
            pl.BlockSpec((D_MODEL, NC_PAD), lambda i: (0, 0), pipeline_mode=pl.Buffered(1)),
            pl.BlockSpec((TP, HB), tab_map),
            pl.BlockSpec((TP, HB), tab_map),
            pl.BlockSpec((1, HB), lambda i: (0, 0)),
            pl.BlockSpec((1, HB), lambda i: (0, 0)),
        ],
        out_specs=[pl.BlockSpec((w, TP), lambda i: (0, i)) if n in _FEATURE_MAJOR
                   else pl.BlockSpec((TP, w), lambda i: (i, 0)) for n, w in zip(names, widths)],
        compiler_params=pltpu.CompilerParams(
            dimension_semantics=("arbitrary",), vmem_limit_bytes=_vmem_limit(est)),
        name="norm_inproj",
    )(x_lat, x_ctx, mod3, g1, w_pad, cs_tab, sn_tab, qg, kg)


ATT_TK = 8192
ATT_NK = SEQ // ATT_TK
ATT_CK = 256
ATT_VROWS = HEAD_DIM + 16


def _attn_kernel(qt_ref, k_ref, vt_ref, kc_ref, vct_ref, g_ref, o_ref, m_sc, acc_sc):
    i = pl.program_id(2)
    j = pl.program_id(3)
    is_ctx_q = i >= LAT_TILES_PER_BATCH

    @pl.when(j == 0)
    def _():
        m_sc[...] = jnp.full(m_sc.shape, -jnp.inf, F32)
        acc_sc[...] = jnp.zeros(acc_sc.shape, F32)

    def scores(kr, c, ck):
        k = kr[c * ck:(c + 1) * ck, :]
        return [jnp.dot(k, qt_ref[hh * HB:(hh + 1) * HB, :], preferred_element_type=F32)
                for hh in range(ATT_GRP)]

    def step(kr, vtr, nkeys):
        ck = min(ATT_CK, nkeys)
        nchunk = nkeys // ck
        ss = scores(kr, 0, ck)
        for c in range(nchunk):
            cur = ss
            if c + 1 < nchunk:
                ss = scores(kr, c + 1, ck)
            vt = vtr[0:ATT_VROWS, c * ck:(c + 1) * ck]
            ps, alphas = [], []
            for hh in range(ATT_GRP):
                m_prev = m_sc[hh]
                m_new = jnp.maximum(m_prev, jnp.max(cur[hh], axis=0, keepdims=True))
                alpha = jnp.exp2(m_prev - m_new)
                p = jnp.exp2(cur[hh] - m_new)
                m_sc[hh] = m_new
                ps.append(p.astype(BF16))
                alphas.append(alpha)
            for hh in range(ATT_GRP):
                acc_sc[hh] = alphas[hh] * acc_sc[hh] + jnp.dot(vt, ps[hh],
                                                               preferred_element_type=F32)

    @pl.when(jnp.logical_not(is_ctx_q))
    def _():
        step(k_ref, vt_ref, ATT_TK)

    @pl.when(j == ATT_NK - 1)
    def _():
        step(kc_ref, vct_ref, CTX_LEN)
        for hh in range(ATT_GRP):
            acc = acc_sc[hh]
            o = acc[0:HEAD_DIM] / acc[HEAD_DIM:HEAD_DIM + 1]
            ms = jnp.sum(o * o, axis=0, keepdims=True) * (1.0 / HEAD_DIM)
            g = g_ref[hh * HB:hh * HB + HEAD_DIM, :]
            y = o * lax.rsqrt(ms + EPS) * jnp.concatenate([g] * (TM // LANES), axis=1)
            y = jnp.concatenate([y, jnp.zeros_like(y)], axis=0)
            o_ref[:, hh * HB:(hh + 1) * HB] = y.T.astype(BF16)


def _gqa(aqt, ak, avt, ga_cols, want_ctx):
    nq = LAT_TILES_PER_BATCH + (1 if want_ctx else 0)

    def q_tile(b, i):
        return jnp.where(i < LAT_TILES_PER_BATCH, b * LAT_TILES_PER_BATCH + i, LAT_TILES + b)

    est = (2 * ATT_GRP * HB * TM * 2 + 4 * ATT_TK * HB * 2 + 4 * TM * HB * 2
           + ATT_GRP * TM * HB * 4 * 3 + 8 * TM * ATT_TK * 4)
    return pl.pallas_call(
        _attn_kernel,
        out_shape=jax.ShapeDtypeStruct((T_ALL if want_ctx else T_LAT, N_ATT_HEADS * HB), BF16),
        grid=(BATCH, N_ATT_KV, nq, ATT_NK),
        in_specs=[
            pl.BlockSpec((ATT_GRP * HB, TM), lambda b, c, i, j: (c, q_tile(b, i))),
            pl.BlockSpec((ATT_TK, HB), lambda b, c, i, j: (b * ATT_NK + j, c)),
            pl.BlockSpec((HB, ATT_TK), lambda b, c, i, j: (c, b * ATT_NK + j)),
            pl.BlockSpec((CTX_LEN, HB), lambda b, c, i, j: (LAT_TILES + b, c)),
            pl.BlockSpec((HB, CTX_LEN), lambda b, c, i, j: (c, LAT_TILES + b)),
            pl.BlockSpec((ATT_GRP * HB, LANES), lambda b, c, i, j: (c, 0)),
        ],
        out_specs=pl.BlockSpec((TM, ATT_GRP * HB), lambda b, c, i, j: (q_tile(b, i), c)),
        scratch_shapes=[
            pltpu.VMEM((ATT_GRP, 1, TM), F32),
            pltpu.VMEM((ATT_GRP, ATT_VROWS, TM), F32),
        ],
        compiler_params=pltpu.CompilerParams(
            dimension_semantics=("arbitrary",) * 4, vmem_limit_bytes=_vmem_limit(est)),
        name="gqa_attn",
    )(aqt, ak, avt, ak, avt, ga_cols)


NA_BAND = NA_WIN_ROWS * GRID_W
NA_CLASSES = 8
NA_ROWS_PER_ITER = 8
_NA_CLASS_ROWS = (0, 1, 2, 3, GRID_ROWS // 2, GRID_ROWS - 3, GRID_ROWS - 2, GRID_ROWS - 1)


def _na_bias_table(rpb):
    wr, wc = NA_WIN_ROWS, NA_WIN_COLS
    r = np.asarray(_NA_CLASS_ROWS)
    ridx = np.clip(r - wr // 2, 0, GRID_ROWS - wr)[:, None] + np.arange(wr)[None, :]
    dr = ridx - r[:, None] + (wr - 1)
    col = np.arange(GRID_W)
    cstart = np.clip(col - wc // 2, 0, GRID_W - wc)
    col_ok = (col[None, :] >= cstart[:, None]) & (col[None, :] < cstart[:, None] + wc)
    dc = np.clip(col[None, :] - col[:, None] + (wc - 1), 0, 2 * wc - 2)
    pick_r = (dr[:, :, None] == np.arange(2 * wr - 1)[None, None, :]).astype(np.float32)
    pick_c = (dc[:, :, None] == np.arange(2 * wc - 1)[None, None, :]).astype(np.float32)
    hp = lax.Precision.HIGHEST
    by_col = jnp.einsum("hrc,qkc->hrqk", rpb.astype(F32), jnp.asarray(pick_c), precision=hp)
    bias = jnp.einsum("hrqk,cwr->hcqwk", by_col, jnp.asarray(pick_r), precision=hp)
    bias = bias.reshape(N_NA_HEADS, NA_CLASSES, GRID_W, NA_BAND)
    mask = np.tile(col_ok, (1, wr))
    return jnp.where(mask[None, None], bias, NEG_INF)


def _head_rms_gain(o, g):
    ms = jnp.sum(o * o, axis=-1, keepdims=True) * (1.0 / HEAD_DIM)
    return o * lax.rsqrt(ms + EPS) * g


def _pair_store(ref, rows, o, g, odd, post=None):
    lane = lax.broadcasted_iota(jnp.int32, o.shape, 1)
    mine = (lane >= HEAD_DIM) if odd else (lane < HEAD_DIM)
    y = _head_rms_gain(jnp.where(mine, o, 0.0), g)
    if post is not None:
        y = y * post
    if odd:
        y = y + ref[rows, :].astype(F32)
    ref[rows, :] = y.astype(BF16)


def _for_head_parity(body):
    is_odd = pl.program_id(1) % 2 == 1
    pl.when(jnp.logical_not(is_odd))(functools.partial(body, False))
    pl.when(is_odd)(functools.partial(body, True))


def _na_kernel(q_ref, k_ref, v_ref, kc_ref, vc_ref, qc_ref, bias_ref, g_ref, o_ref, oc_ref,
               *, want_ctx):
    _for_head_parity(functools.partial(
        _na_body, q_ref, k_ref, v_ref, kc_ref, vc_ref, qc_ref, bias_ref, g_ref, o_ref, oc_ref,
        want_ctx))


def _na_body(q_ref, k_ref, v_ref, kc_ref, vc_ref, qc_ref, bias_ref, g_ref, o_ref, oc_ref,
             want_ctx, odd):
    kc = kc_ref[...]
    vc = vc_ref[...]
    g = g_ref[0]
    half = NA_WIN_ROWS // 2
    last = GRID_ROWS - NA_WIN_ROWS

    def rows(it, carry):
        r0 = it * NA_ROWS_PER_ITER
        qrows, bands, scores = [], [], []
        for d in range(NA_ROWS_PER_ITER):
            r = r0 + d
            start = jnp.clip(r - half, 0, last)
            cls = jnp.where(r < half, r, jnp.where(r > last + half, r - last, half))
            qrow = pl.ds(pl.multiple_of(r * GRID_W, GRID_W), GRID_W)
            band = pl.ds(pl.multiple_of(start * GRID_W, GRID_W), NA_BAND)
            q = q_ref[qrow, :]
            s = lax.dot_general(q, k_ref[band, :], (((1,), (1,)), ((), ())),
                                preferred_element_type=F32)
            sc = lax.dot_general(q, kc, (((1,), (1,)), ((), ())), preferred_element_type=F32)
            qrows.append(qrow)
            bands.append(band)
            scores.append((s, sc, cls))
        probs = []
        for s, sc, cls in scores:
            bt = bias_ref[0, cls]
            s = jnp.where(bt > 0.5 * NEG_INF, s + bt, NEG_INF)
            m = jnp.maximum(jnp.max(s, axis=-1, keepdims=True),
                            jnp.max(sc, axis=-1, keepdims=True))
            p = jnp.exp(s - m)
            pc = jnp.exp(sc - m)
            l = jnp.sum(p, axis=-1, keepdims=True) + jnp.sum(pc, axis=-1, keepdims=True)
            probs.append((p.astype(BF16), pc.astype(BF16), l))
        for qrow, band, (p, pc, l) in zip(qrows, bands, probs):
            o = (jnp.dot(p, v_ref[band, :], preferred_element_type=F32)
                 + jnp.dot(pc, vc, preferred_element_type=F32)) / l
            _pair_store(o_ref, qrow, o, g, odd)
        return carry

    lax.fori_loop(0, GRID_ROWS // NA_ROWS_PER_ITER, rows, 0)

    if want_ctx:
        sc = lax.dot_general(qc_ref[...], kc, (((1,), (1,)), ((), ())),
                             preferred_element_type=F32)
        m = jnp.max(sc, axis=-1, keepdims=True)
        pc = jnp.exp(sc - m)
        l = jnp.sum(pc, axis=-1, keepdims=True)
        o = jnp.dot(pc.astype(BF16), vc, preferred_element_type=F32) / l
        _pair_store(oc_ref, slice(None), o, g, odd)
    elif not odd:
        oc_ref[...] = jnp.zeros(oc_ref.shape, oc_ref.dtype)


def _pair_specs(rows, first_block=0):
    return (pl.BlockSpec((rows, HB), lambda b, h: (first_block + b, h)),
            pl.BlockSpec((rows, HB), lambda b, h: (first_block + b, h // 2)))


def _neigh(nq, nk, nv, bias_tab, gn, want_ctx):
    lat, lat_pair = _pair_specs(SEQ)
    ctx, ctx_pair = _pair_specs(CTX_LEN, T_LAT // CTX_LEN)
    est = 2 * (4 * SEQ * HB * 2 + 4 * CTX_LEN * HB * 2 + NA_CLASSES * GRID_W * NA_BAND * 4)
    return pl.pallas_call(
        functools.partial(_na_kernel, want_ctx=want_ctx),
        out_shape=[jax.ShapeDtypeStruct((T_LAT, _YN_W), BF16),
                   jax.ShapeDtypeStruct((T_CTX, _YN_W), BF16)],
        grid=(BATCH, N_NA_HEADS),
        in_specs=[lat, lat_pair, lat_pair, ctx_pair, ctx_pair, ctx,
                  pl.BlockSpec((1, NA_CLASSES, GRID_W, NA_BAND), lambda b, h: (h, 0, 0, 0)),
                  pl.BlockSpec((1, 1, HB), lambda b, h: (h, 0, 0))],
        out_specs=[_pair_specs(SEQ)[1], _pair_specs(CTX_LEN)[1]],
        compiler_params=pltpu.CompilerParams(
            dimension_semantics=("arbitrary", "arbitrary"), vmem_limit_bytes=_vmem_limit(est)),
        name="neigh_attn",
    )(nq, nk, nv, nk, nv, nq, bias_tab, gn)


RET_NCHUNK = SEQ // RET_CHUNK
RET_NCHUNK_CTX = CTX_LEN // RET_CHUNK
RET_CHUNKS_PER_ITER = 8


def _ret_tables(log_g2):
    lf = log_g2[0][:, None, None]
    lb = log_g2[1][:, None, None]
    pos = jnp.arange(RET_CHUNK, dtype=F32)
    i = pos[None, :, None]
    j = pos[None, None, :]
    diff = i - j
    dm = jnp.where(diff > 0, jnp.exp(lf * jnp.maximum(diff, 0.0)),
                   jnp.where(diff < 0, jnp.exp(lb * jnp.maximum(-diff, 0.0)), 2.0)) * 0.5
    fwd_lane = (jnp.arange(LANES) < HEAD_DIM)[None, None, :]
    xi = jnp.where(fwd_lane, jnp.exp(lf * (i + 1.0)), jnp.exp(lb * (RET_CHUNK - i)))
    zt = jnp.where(fwd_lane, jnp.exp(lf * (RET_CHUNK - 1.0 - i)), jnp.exp(lb * i))
    fwd_row = (jnp.arange(LANES) < HEAD_DIM)[None, :, None]
    dec = jnp.where(fwd_row, jnp.exp(lf * RET_CHUNK), jnp.exp(lb * RET_CHUNK))
    dec = jnp.broadcast_to(dec, (N_RET_HEADS, LANES, LANES))
    return dm.astype(F32), xi.astype(F32), zt.astype(F32), dec.astype(F32)


def _ret_kernel(*refs, want_ctx):
    _for_head_parity(functools.partial(_ret_body, *refs, want_ctx))


def _ret_body(q_ref, k_ref, v_ref, gt_ref, qc_ref, kc_ref, vc_ref, gtc_ref,
              dm_ref, xi_ref, zt_ref, dec_ref, g_ref, o_ref, oc_ref,
              u_sc, s_sc, uc_sc, sc_sc, want_ctx, odd):
    dm = dm_ref[0]
    xi = xi_ref[0]
    zt = zt_ref[0]
    dec = dec_ref[0]
    dec_f = dec[0:HEAD_DIM]
    dec_b = dec[HEAD_DIM:LANES]
    g = g_ref[0]
    C = RET_CHUNK

    def chunk_rows(n):
        return pl.ds(pl.multiple_of(n * C, C), C)

    def chunk_state_update(kr, vr, usc, ns):
        kzs = [(kr[chunk_rows(n), :].astype(F32) * zt).T.astype(BF16) for n in ns]
        for n, kz in zip(ns, kzs):
            usc[n] = jnp.dot(kz, vr[chunk_rows(n), :], preferred_element_type=F32)

    def chunk_out(qr, kr, vr, gtr, ssc, outr, ns):
        qds = [qr[chunk_rows(n), :] for n in ns]
        s2s = [lax.dot_general(qd, kr[chunk_rows(n), :], (((1,), (1,)), ((), ())),
                               preferred_element_type=F32) for n, qd in zip(ns, qds)]
        outs = []
        for n, qd, s2 in zip(ns, qds, s2s):
            inner = jnp.dot((s2 * dm).astype(BF16), vr[chunk_rows(n), :],
                            preferred_element_type=F32)
            qx = (qd.astype(F32) * xi).astype(BF16)
            outs.append(inner + jnp.dot(qx, ssc[n].astype(BF16), preferred_element_type=F32))
        for n, o in zip(ns, outs):
            gate = gtr[chunk_rows(n), :].astype(F32)
            _pair_store(outr, chunk_rows(n), o, g, odd, post=gate * jax.nn.sigmoid(gate))

    def scan_states(usc, ssc, nchunk, init_f, init_b):
        def fwd(n, sf):
            ssc[n, 0:HEAD_DIM, :] = sf
            return dec_f * sf + usc[n, 0:HEAD_DIM, :]

        def bwd(t, sb):
            n = nchunk - 1 - t
            ssc[n, HEAD_DIM:LANES, :] = sb
            return dec_b * sb + usc[n, HEAD_DIM:LANES, :]

        return (lax.fori_loop(0, nchunk, fwd, init_f), lax.fori_loop(0, nchunk, bwd, init_b))

    zero = jnp.zeros((HEAD_DIM, LANES), F32)
    ctx_chunks = list(range(RET_NCHUNK_CTX))
    chunk_state_update(kc_ref, vc_ref, uc_sc, ctx_chunks)
    ctx_f, ctx_b = scan_states(uc_sc, sc_sc, RET_NCHUNK_CTX, zero, zero)
    if want_ctx:
        chunk_out(qc_ref, kc_ref, vc_ref, gtc_ref, sc_sc, oc_ref, ctx_chunks)
    elif not odd:
        oc_ref[...] = jnp.zeros(oc_ref.shape, oc_ref.dtype)

    def upd(it, carry):
        chunk_state_update(k_ref, v_ref, u_sc,
                           [it * RET_CHUNKS_PER_ITER + d for d in range(RET_CHUNKS_PER_ITER)])
        return carry

    lax.fori_loop(0, RET_NCHUNK // RET_CHUNKS_PER_ITER, upd, 0)
    scan_states(u_sc, s_sc, RET_NCHUNK, ctx_f, ctx_b)

    def out(it, carry):
        chunk_out(q_ref, k_ref, v_ref, gt_ref, s_sc, o_ref,
                  [it * RET_CHUNKS_PER_ITER + d for d in range(RET_CHUNKS_PER_ITER)])
        return carry

    lax.fori_loop(0, RET_NCHUNK // RET_CHUNKS_PER_ITER, out, 0)


def _retention(rq, rk, rv, rg, tables, gr, want_ctx):
    lat, lat_pair = _pair_specs(SEQ)
    ctx, ctx_pair = _pair_specs(CTX_LEN, T_LAT // CTX_LEN)
    tab = pl.BlockSpec((1, LANES, LANES), lambda b, h: (h, 0, 0))
    est = (2 * 5 * SEQ * HB * 2 + 2 * RET_NCHUNK * LANES * LANES * 4 + 8 * LANES * LANES * 4)
    return pl.pallas_call(
        functools.partial(_ret_kernel, want_ctx=want_ctx),
        out_shape=[jax.ShapeDtypeStruct((T_LAT, _YR_W), BF16),
                   jax.ShapeDtypeStruct((T_CTX, _YR_W), BF16)],
        grid=(BATCH, N_RET_HEADS),
        in_specs=[lat, lat, lat_pair, lat_pair, ctx, ctx, ctx_pair, ctx_pair, tab, tab, tab, tab,
                  pl.BlockSpec((1, 1, HB), lambda b, h: (h, 0, 0))],
        out_specs=[_pair_specs(SEQ)[1], _pair_specs(CTX_LEN)[1]],
        scratch_shapes=[
            pltpu.VMEM((RET_NCHUNK, LANES, LANES), F32),
            pltpu.VMEM((RET_NCHUNK, LANES, LANES), F32),
            pltpu.VMEM((RET_NCHUNK_CTX, LANES, LANES), F32),
            pltpu.VMEM((RET_NCHUNK_CTX, LANES, LANES), F32),
        ],
        compiler_params=pltpu.CompilerParams(
            dimension_semantics=("arbitrary", "arbitrary"), vmem_limit_bytes=_vmem_limit(est)),
        name="retention",
    )(rq, rk, rv, rg, rq, rk, rv, rg, *tables, gr)


MOE_SLOT = TM
META_ROWS = 8
_META_GATE0 = 2


def _route_t(logt):
    row = lax.broadcasted_iota(jnp.int32, logt.shape, 0).astype(F32)
    p = jnp.exp(logt - jnp.max(logt, axis=0, keepdims=True))
    best = None
    for grp in range(N_EXPERT_GROUPS):
        lo = float(grp * EXPERTS_PER_GROUP)
        ing = (row >= lo) & (row < lo + EXPERTS_PER_GROUP)
        pg = jnp.where(ing, p, -1.0)
        m1 = jnp.max(pg, axis=0, keepdims=True)
        i1 = jnp.min(jnp.where(pg == m1, row, float(N_EXPERTS)), axis=0, keepdims=True)
        pg2 = jnp.where(row == i1, -1.0, pg)
        m2 = jnp.max(pg2, axis=0, keepdims=True)
        i2 = jnp.min(jnp.where(pg2 == m2, row, float(N_EXPERTS)), axis=0, keepdims=True)
        cand = (m1 + m2, m1, m2, i1, i2, jnp.zeros_like(m1) + lo)
        if best is None:
            best = cand
        else:
            better = cand[0] > best[0]
            best = tuple(jnp.where(better, c, b) for c, b in zip(cand, best))
    _, m1, m2, i1, i2, base = best
    w = m1 + m2
    gates = [jnp.where(i1 == base + e, m1 / w, jnp.where(i2 == base + e, m2 / w, 0.0))
             for e in range(EXPERTS_PER_GROUP)]
    return base * (1.0 / EXPERTS_PER_GROUP), gates


def _merge_kernel(ya_ref, ynl_ref, ync_ref, yrl_ref, yrc_ref, xl_ref, xc_ref, mod_ref, g2_ref,
                  wo_ref, wr_ref, brt_ref,
                  xn_ref, h2_ref, metat_ref, metac_ref, ctab_ref, tot_ref, carry_sc, *, split_ctx):
    @pl.when(pl.program_id(0) == 0)
    def _():
        carry_sc[...] = jnp.zeros(carry_sc.shape, F32)

    mod = mod_ref[0]
    gt1 = mod[:, 2 * D_MODEL:3 * D_MODEL]
    sh2 = mod[:, 3 * D_MODEL:4 * D_MODEL]
    sc2 = mod[:, 4 * D_MODEL:5 * D_MODEL]
    yn = _pick_rows(ynl_ref, ync_ref, split_ctx)
    yr = _pick_rows(yrl_ref, yrc_ref, split_ctx)
    m = (jnp.dot(ya_ref[...], wo_ref[0:_YA_W, :], preferred_element_type=F32)
         + jnp.dot(yn, wo_ref[_YA_W:_YA_W + _YN_W, :], preferred_element_type=F32)
         + jnp.dot(yr, wo_ref[_YA_W + _YN_W:MIX_PAD, :], preferred_element_type=F32))
    x = _pick_rows(xl_ref, xc_ref, split_ctx) + gt1 * m
    xn_ref[...] = x
    ms = jnp.mean(x * x, axis=-1, keepdims=True)
    h2 = x * lax.rsqrt(ms + EPS) * g2_ref[...] * (1.0 + sc2) + sh2
    h2_ref[...] = h2.astype(BF16)

    ntile = TP // LANES
    h_hi = h2.astype(BF16)
    h_lo = (h2 - h_hi.astype(F32)).astype(BF16)
    wr = wr_ref[...]
    w_hi = wr.astype(BF16)
    w_lo = (wr - w_hi.astype(F32)).astype(BF16)
    logits = (jnp.dot(h_hi, w_hi, preferred_element_type=F32)
              + jnp.dot(h_lo, w_hi, preferred_element_type=F32)
              + jnp.dot(h_hi, w_lo, preferred_element_type=F32))
    logt = logits.T[0:N_EXPERTS, :]
    gsel, gates = _route_t(logt + jnp.concatenate([brt_ref[...]] * ntile, axis=1))

    grow = lax.broadcasted_iota(jnp.int32, (META_ROWS, TP), 0).astype(F32)
    onehot = jnp.where(grow == gsel, 1.0, 0.0)
    earlier = (lax.broadcasted_iota(jnp.int32, (TP, TP), 0)
               < lax.broadcasted_iota(jnp.int32, (TP, TP), 1))
    excl = jnp.dot(onehot.astype(BF16), jnp.where(earlier, 1.0, 0.0).astype(BF16),
                   preferred_element_type=F32)
    carry = carry_sc[...]
    rank = jnp.sum(onehot * (jnp.concatenate([carry] * ntile, axis=1) + excl),
                   axis=0, keepdims=True)
    for sub in range(TP // TM):
        ctab_ref[sub] = carry
        carry = carry + jnp.sum(onehot[:, sub * TM:(sub + 1) * TM], axis=1, keepdims=True)
    carry_sc[...] = carry
    tot_ref[...] = carry

    metat = jnp.concatenate([gsel, rank] + gates
                            + [jnp.zeros((META_ROWS - _META_GATE0 - EXPERTS_PER_GROUP, TP), F32)],
                            axis=0)
    metat_ref[...] = metat
    metac_ref[...] = jnp.concatenate([metat, jnp.zeros((LANES - META_ROWS, TP), F32)], axis=0).T


def _merge(ya, yn_lat, yn_ctx, yr_lat, yr_ctx, x_lat, x_ctx, split_ctx, mod3, g2, wo_pad, wr_pad,
           brt, ntiles):
    rows = ntiles * TP
    est = (2 * MIX_PAD * D_MODEL * 2 + 4 * TP * MIX_PAD * 2 + 10 * TP * D_MODEL * 4
           + 4 * TP * TP * 4)
    const = lambda i: (0, 0)
    return pl.pallas_call(
        functools.partial(_merge_kernel, split_ctx=split_ctx),
        out_shape=[jax.ShapeDtypeStruct((rows, D_MODEL), F32),
                   jax.ShapeDtypeStruct((rows, D_MODEL), BF16),
                   jax.ShapeDtypeStruct((META_ROWS, rows), F32),
                   jax.ShapeDtypeStruct((rows, LANES), F32),
                   jax.ShapeDtypeStruct((ntiles * (TP // TM), META_ROWS, LANES), F32),
                   jax.ShapeDtypeStruct((META_ROWS, LANES), F32)],
        grid=(ntiles,),
        in_specs=([pl.BlockSpec((TP, _YA_W), lambda i: (i, 0))]
                  + _lat_ctx_specs(_YN_W, split_ctx) + _lat_ctx_specs(_YR_W, split_ctx)
                  + _lat_ctx_specs(D_MODEL, split_ctx) + [
            pl.BlockSpec((1, 1, 6 * D_MODEL), lambda i: (_ptile_mod_row(i), 0, 0)),
            pl.BlockSpec((1, D_MODEL), const),
            pl.BlockSpec((MIX_PAD, D_MODEL), const),
            pl.BlockSpec((D_MODEL, LANES), const),
            pl.BlockSpec((N_EXPERTS, LANES), const),
        ]),
        out_specs=[pl.BlockSpec((TP, D_MODEL), lambda i: (i, 0)),
                   pl.BlockSpec((TP, D_MODEL), lambda i: (i, 0)),
                   pl.BlockSpec((META_ROWS, TP), lambda i: (0, i)),
                   pl.BlockSpec((TP, LANES), lambda i: (i, 0)),
                   pl.BlockSpec((TP // TM, META_ROWS, LANES), lambda i: (i, 0, 0)),
                   pl.BlockSpec((META_ROWS, LANES), const)],
        scratch_shapes=[pltpu.VMEM((META_ROWS, LANES), F32)],
        compiler_params=pltpu.CompilerParams(
            dimension_semantics=("arbitrary",), vmem_limit_bytes=_vmem_limit(est)),
        name="merge_outproj_router",
    )(ya, yn_lat, yn_ctx, yr_lat, yr_ctx, x_lat, x_ctx, mod3, g2, wo_pad, wr_pad, brt)


def _moe_plan(ctab, tot, ntiles):
    grp = N_EXPERT_GROUPS
    i32 = jnp.int32
    a = ctab[:, :grp, 0].astype(i32).T
    totg = tot[:grp, 0].astype(i32)
    b = jnp.concatenate([a[:, 1:], totg[:, None]], axis=1)
    nslot = (totg + MOE_SLOT - 1) // MOE_SLOT
    slot_end = jnp.cumsum(nslot)
    slot_base = slot_end - nslot
    total_slots = slot_end[-1]
    nchunk = ntiles // (MOE_CHUNK // TM)
    ac = a[:, ::MOE_CHUNK // TM]
    bc = jnp.concatenate([ac[:, 1:], totg[:, None]], axis=1)
    first_j = ac // MOE_SLOT
    last_j = (jnp.maximum(bc, 1) - 1) // MOE_SLOT
    npairs = jnp.where(bc > ac, last_j - first_j + 1, 0).reshape(-1)
    cum = jnp.cumsum(npairs)
    start = cum - npairs
    total_pairs = cum[-1]
    n_pairs_max = grp * nchunk + ntiles + grp
    pidx = jnp.arange(n_pairs_max, dtype=i32)
    p = jnp.minimum(pidx, total_pairs - 1)
    gc = jnp.sum((cum[None, :] <= p[:, None]).astype(i32), axis=1)
    slot = slot_base[gc // nchunk] + first_j.reshape(-1)[gc] + (p - start[gc])
    valid = pidx < total_pairs
    prev_slot = jnp.concatenate([jnp.full((1,), -1, i32), slot[:-1]])
    next_slot = jnp.concatenate([slot[1:], jnp.full((1,), -1, i32)])
    is_first = valid & (slot != prev_slot)
    is_last = valid & ((slot != next_slot) | (pidx == total_pairs - 1))
    flags = is_first.astype(i32) + 2 * is_last.astype(i32) + 4 * valid.astype(i32)
    n_slots_max = ntiles + grp
    sidx = jnp.arange(n_slots_max, dtype=i32)
    sgrp = jnp.minimum(jnp.sum((slot_end[None, :] <= sidx[:, None]).astype(i32), axis=1), grp - 1)
    sr0 = (sidx - slot_base[sgrp]) * MOE_SLOT
    row0 = slot_base[:, None] * MOE_SLOT + a
    row1 = slot_base[:, None] * MOE_SLOT + jnp.maximum(b, a + 1) - 1
    wb = jnp.stack([row0 // MOE_SLOT, row1 // MOE_SLOT], axis=1)
    wb = jnp.clip(wb, 0, total_slots - 1).reshape(-1).astype(i32)
    return dict(pslot=slot.astype(i32), pchunk=(gc % nchunk).astype(i32), pflag=flags,
                sgrp=sgrp, sr0=sr0.astype(i32), wb=wb, sbase=slot_base.astype(i32),
                n_pairs=n_pairs_max, n_slots=n_slots_max)


def _moe_kernel(pslot_ref, pchunk_ref, pflag_ref, sgrp_ref, sr0_ref,
                h_ref, metat_ref, metac_ref, w1_ref, w3_ref, w2_ref, o_ref, x_sc, g_sc):
    p = pl.program_id(0)
    flags = pflag_ref[p]
    slot = pslot_ref[p]

    @pl.when((flags & 1) != 0)
    def _():
        x_sc[...] = jnp.zeros(x_sc.shape, F32)
        g_sc[...] = jnp.zeros(g_sc.shape, F32)

    @pl.when((flags & 4) != 0)
    def _():
        mt = metat_ref[...].astype(jnp.int32)
        want = lax.broadcasted_iota(jnp.int32, (MOE_SLOT, MOE_CHUNK), 0) + sr0_ref[slot]
        sel = (mt[1:2, :] == want) & (mt[0:1, :] == sgrp_ref[slot])
        pm = jnp.where(sel, 1.0, 0.0).astype(BF16)
        x_sc[...] += jnp.dot(pm, h_ref[...], preferred_element_type=F32)
        mc = metac_ref[...]
        hi = mc.astype(BF16)
        r1 = mc - hi.astype(F32)
        mid = r1.astype(BF16)
        lo = (r1 - mid.astype(F32)).astype(BF16)
        g_sc[...] += (jnp.dot(pm, hi, preferred_element_type=F32)
                      + jnp.dot(pm, mid, preferred_element_type=F32)
                      + jnp.dot(pm, lo, preferred_element_type=F32))

    @pl.when((flags & 2) != 0)
    def _():
        x = x_sc[...].astype(BF16)
        gs = g_sc[...]
        acc = jnp.zeros((MOE_SLOT, D_MODEL), F32)
        for e in range(EXPERTS_PER_GROUP):
            a = jnp.dot(x, w1_ref[0, 0, e], preferred_element_type=F32)
            b = jnp.dot(x, w3_ref[0, 0, e], preferred_element_type=F32)
            gate = gs[:, _META_GATE0 + e:_META_GATE0 + e + 1]
            act = (a * jax.nn.sigmoid(a)) * b * gate
            acc = acc + jnp.dot(act.astype(BF16), w2_ref[0, 0, e], preferred_element_type=F32)
        o_ref[...] = acc.astype(BF16)


def _moe_sorted(plan, h2, metat, metac, w1g, w3g, w2g, layer):
    est = (2 * 3 * EXPERTS_PER_GROUP * D_MODEL * D_EXPERT * 2 + 6 * MOE_CHUNK * D_MODEL * 2
           + 2 * MOE_SLOT * D_MODEL * 4 + 6 * MOE_SLOT * D_EXPERT * 4
           + 4 * MOE_CHUNK * MOE_SLOT * 4)
    wmap = lambda p, ps, pc, pf, sg, sr: (layer, sg[ps[p]], 0, 0, 0)
    grid_spec = pltpu.PrefetchScalarGridSpec(
        num_scalar_prefetch=5,
        grid=(plan["n_pairs"],),
        in_specs=[
            pl.BlockSpec((MOE_CHUNK, D_MODEL), lambda p, ps, pc, pf, sg, sr: (pc[p], 0)),
            pl.BlockSpec((META_ROWS, MOE_CHUNK), lambda p, ps, pc, pf, sg, sr: (0, pc[p])),
            pl.BlockSpec((MOE_CHUNK, LANES), lambda p, ps, pc, pf, sg, sr: (pc[p], 0)),
            pl.BlockSpec((1, 1, EXPERTS_PER_GROUP, D_MODEL, D_EXPERT), wmap),
            pl.BlockSpec((1, 1, EXPERTS_PER_GROUP, D_MODEL, D_EXPERT), wmap),
            pl.BlockSpec((1, 1, EXPERTS_PER_GROUP, D_EXPERT, D_MODEL), wmap),
        ],
        out_specs=pl.BlockSpec((MOE_SLOT, D_MODEL), lambda p, ps, pc, pf, sg, sr: (ps[p], 0)),
        scratch_shapes=[pltpu.VMEM((MOE_SLOT, D_MODEL), F32), pltpu.VMEM((MOE_SLOT, LANES), F32)],
    )
    return pl.pallas_call(
        _moe_kernel,
        out_shape=jax.ShapeDtypeStruct((plan["n_slots"] * MOE_SLOT, D_MODEL), BF16),
        grid_spec=grid_spec,
        compiler_params=pltpu.CompilerParams(
            dimension_semantics=("arbitrary",), vmem_limit_bytes=_vmem_limit(est)),
        name="moe_sorted_experts",
    )(plan["pslot"], plan["pchunk"], plan["pflag"], plan["sgrp"], plan["sr0"],
      h2, metat, metac, w1g, w3g, w2g)


_COMBINE_WINDOWS = 2 * N_EXPERT_GROUPS


def _combine_kernel(wb_ref, sb_ref, *refs, ntiles, final):
    y_refs = refs[:_COMBINE_WINDOWS]
    metac_ref, xn_ref, mod_ref, fg_ref, o_ref = refs[_COMBINE_WINDOWS:]
    i = pl.program_id(0)
    mc = metac_ref[...]
    grp = mc[:, 0:1].astype(jnp.int32)
    rank = mc[:, 1:2].astype(jnp.int32)
    col = lax.broadcasted_iota(jnp.int32, (TM, MOE_SLOT), 1)
    m = jnp.zeros((TM, D_MODEL), F32)
    for g in range(N_EXPERT_GROUPS):
        row = rank + sb_ref[g] * MOE_SLOT
        blk0 = wb_ref[(2 * g) * ntiles + i]
        for k in range(2):
            blk = wb_ref[(2 * g + k) * ntiles + i]
            sel = (grp == g) & (row == col + blk * MOE_SLOT)
            if k == 1:
                sel = jnp.logical_and(sel, blk != blk0)
            m = m + jnp.dot(jnp.where(sel, 1.0, 0.0).astype(BF16), y_refs[2 * g + k][...],
                            preferred_element_type=F32)
    gt2 = mod_ref[0][:, 5 * D_MODEL:6 * D_MODEL]
    x = xn_ref[...] + gt2 * m
    if final:
        ms = jnp.mean(x * x, axis=-1, keepdims=True)
        x = x * lax.rsqrt(ms + EPS) * fg_ref[...]
    o_ref[...] = x


def _combine(plan, ys, metac, xn, mod3, fg, ntiles, final):
    def ymap(w):
        return lambda i, wb, sb: (wb[w * ntiles + i], 0)

    est = (2 * _COMBINE_WINDOWS * MOE_SLOT * D_MODEL * 2 + 8 * TM * D_MODEL * 4
           + 4 * TM * MOE_SLOT * 4)
    grid_spec = pltpu.PrefetchScalarGridSpec(
        num_scalar_prefetch=2,
        grid=(ntiles,),
        in_specs=[pl.BlockSpec((MOE_SLOT, D_MODEL), ymap(w)) for w in range(_COMBINE_WINDOWS)] + [
            pl.BlockSpec((TM, LANES), lambda i, wb, sb: (i, 0)),
            pl.BlockSpec((TM, D_MODEL), lambda i, wb, sb: (i, 0)),
            pl.BlockSpec((1, 1, 6 * D_MODEL), lambda i, wb, sb: (_tile_mod_row(i), 0, 0)),
            pl.BlockSpec((1, D_MODEL), lambda i, wb, sb: (0, 0)),
        ],
        out_specs=pl.BlockSpec((TM, D_MODEL), lambda i, wb, sb: (i, 0)),
    )
    return pl.pallas_call(
        functools.partial(_combine_kernel, ntiles=ntiles, final=final),
        out_shape=jax.ShapeDtypeStruct((ntiles * TM, D_MODEL), F32),
        grid_spec=grid_spec,
        compiler_params=pltpu.CompilerParams(
            dimension_semantics=("arbitrary",), vmem_limit_bytes=_vmem_limit(est)),
        name="moe_combine",
    )(plan["wb"], plan["sbase"], *([ys] * _COMBINE_WINDOWS), metac, xn, mod3, fg)


def _rope_tables():
    t = np.arange(SEQ)
    nf = HEAD_DIM // 4
    inv = (np.float32(ROPE_THETA) ** (-np.arange(nf, dtype=np.float32) / np.float32(nf)))
    inv = inv.astype(np.float32)
    ang_r = (t // GRID_W).astype(np.float32)[:, None] * inv[None, :]
    ang_c = (t % GRID_W).astype(np.float32)[:, None] * inv[None, :]
    cr, sr, cc, sc = np.cos(ang_r), np.sin(ang_r), np.cos(ang_c), np.sin(ang_c)
    zeros = np.zeros((SEQ, HEAD_DIM), np.float32)
    cs = np.concatenate([cr, cr, cc, cc, zeros], axis=-1)
    sn = np.concatenate([-sr, sr, -sc, sc, zeros], axis=-1)
    ident = np.concatenate([np.ones((T_CTX, HEAD_DIM), np.float32),
                            np.zeros((T_CTX, HEAD_DIM), np.float32)], axis=-1)
    cs = np.concatenate([cs, ident], axis=0).astype(np.float32)
    sn = np.concatenate([sn, np.zeros((T_CTX, LANES), np.float32)], axis=0).astype(np.float32)
    return jnp.asarray(cs), jnp.asarray(sn)


def _pad_out_weight(w_out_l):
    w = w_out_l[:ATT_Q].reshape(N_ATT_HEADS, HEAD_DIM, D_MODEL)
    w = jnp.concatenate([w, jnp.zeros_like(w)], axis=1).reshape(_YA_W, D_MODEL)
    return jnp.concatenate([w, w_out_l[ATT_Q:]], axis=0).astype(BF16)


def _pad_in_weight(w_in_l):
    parts = []
    for _, src, heads, mode in _SECTIONS:
        w = w_in_l[:, src:src + heads * HEAD_DIM]
        if mode != "dense":
            w = w.reshape(D_MODEL, heads, HEAD_DIM)
            zero = jnp.zeros_like(w)
            if mode == "par":
                even = (np.arange(heads) % 2 == 0)[None, :, None]
                halves = [jnp.where(even, w, zero), jnp.where(even, zero, w)]
            else:
                halves = [w, w if mode == "dup" else zero]
            w = jnp.concatenate(halves, axis=-1).reshape(D_MODEL, heads * HB)
        parts.append(w)
    return jnp.concatenate(parts, axis=-1).astype(BF16)


def kernel(x, c, ctx, c_ctx, w_ada, b_ada, norm1_g, norm2_g, w_in, q_norm_g, k_norm_g, na_rpb,
           ret_decay, mix_g, w_out, w_router, b_router, w_exp1, w_exp3, w_exp2, final_g):
    cs_tab, sn_tab = _rope_tables()

    cvec = jnp.concatenate([c, c_ctx[None, :], jnp.zeros((8 - BATCH - 1, D_MODEL), F32)], axis=0)
    mod_all = _ada_mod(cvec, w_ada, b_ada)

    wr_pad = jnp.concatenate([w_router, jnp.zeros((D_MODEL, LANES - N_EXPERTS), F32)], axis=1)
    brt = jnp.broadcast_to(b_router[:, None], (N_EXPERTS, LANES))
    zero_lane = jnp.zeros((HEAD_DIM,), F32)

    x_lat = x.reshape(T_LAT, D_MODEL)
    x_ctx = ctx.reshape(T_CTX, D_MODEL)

    gshape = (DEPTH, N_EXPERT_GROUPS, EXPERTS_PER_GROUP)
    w1g = w_exp1.astype(BF16).reshape(gshape + (D_MODEL, D_EXPERT))
    w3g = w_exp3.astype(BF16).reshape(gshape + (D_MODEL, D_EXPERT))
    w2g = w_exp2.astype(BF16).reshape(gshape + (D_EXPERT, D_MODEL))

    for l in range(DEPTH):
        last = l == DEPTH - 1
        want_ctx = not last
        split_ctx = l == 0
        mod3 = mod_all[l].reshape(8, 1, 6 * D_MODEL)
        w_pad = _pad_in_weight(w_in[l])
        qg = jnp.concatenate([q_norm_g[l], zero_lane])[None, :]
        kg = jnp.concatenate([k_norm_g[l], zero_lane])[None, :]
        aq, ak, av, nq, nk, nv, rq, rk, rv, rg = _inproj(
            x_lat, x_ctx, split_ctx, mod3, norm1_g[l][None, :], w_pad, cs_tab, sn_tab, qg, kg)

        ga = jnp.broadcast_to(mix_g[l][:ATT_Q].reshape(N_ATT_HEADS, HEAD_DIM, 1),
                              (N_ATT_HEADS, HEAD_DIM, LANES))
        ga = jnp.concatenate([ga, jnp.zeros_like(ga)], axis=1).reshape(N_ATT_HEADS * HB, LANES)
        gn = _pair_gains(mix_g[l][ATT_Q:ATT_Q + NA_W], N_NA_HEADS)
        gr = _pair_gains(mix_g[l][ATT_Q + NA_W:], N_RET_HEADS)

        ya = _gqa(aq, ak, av, ga, want_ctx)
        yn_lat, yn_ctx = _neigh(nq, nk, nv, _na_bias_table(na_rpb[l]), gn, want_ctx)
        log_g2 = jax.nn.log_sigmoid(ret_decay[l].astype(F32))
        yr_lat, yr_ctx = _retention(rq, rk, rv, rg, _ret_tables(log_g2), gr, want_ctx)

        wo_pad = _pad_out_weight(w_out[l])
        ntiles = LAT_TILES if last else ALL_TILES
        xn, h2, metat, metac, ctab, tot = _merge(
            ya, yn_lat, yn_ctx, yr_lat, yr_ctx, x_lat, x_ctx, split_ctx, mod3,
            norm2_g[l][None, :], wo_pad, wr_pad, brt, ntiles // (TP // TM))
        plan = _moe_plan(ctab, tot, ntiles)
        ys = _moe_sorted(plan, h2, metat, metac, w1g, w3g, w2g, l)
        x_lat = _combine(plan, ys, metac, xn, mod3, final_g[None, :], ntiles, last)
        x_ctx = x_lat

    return x_lat.reshape(BATCH, SEQ, D_MODEL)
```

```python
import functools

import numpy as np
import jax
import jax.numpy as jnp
from jax import lax
from jax.experimental import pallas as pl
from jax.experimental.pallas import tpu as pltpu

D_MODEL = 1024
BATCH = 2
SEQ = 8192
DEPTH = 2
GRID_W = 64
GRID_ROWS = SEQ // GRID_W
CTX_LEN = 256
HEAD_DIM = 64
N_ATT_HEADS = 6
N_ATT_KV = 2
ATT_GRP = N_ATT_HEADS // N_ATT_KV
N_NA_HEADS = 4
N_RET_HEADS = 6
ATT_Q = N_ATT_HEADS * HEAD_DIM
ATT_KV = N_ATT_KV * HEAD_DIM
NA_W = N_NA_HEADS * HEAD_DIM
RET_W = N_RET_HEADS * HEAD_DIM
NA_WIN_ROWS = 8
NA_WIN_COLS = 16
RET_CHUNK = 128
ROPE_THETA = 10000.0
N_EXPERTS = 16
N_EXPERT_GROUPS = 4
EXPERTS_PER_GROUP = N_EXPERTS // N_EXPERT_GROUPS
D_EXPERT = 512
EPS = 1e-6
NEG_INF = -1e30

LANES = 128
VMEM_LIMIT_CAP = 56 * 1024 * 1024

T_LAT = BATCH * SEQ
T_CTX = BATCH * CTX_LEN
T_ALL = T_LAT + T_CTX
TM = 256
LAT_TILES_PER_BATCH = SEQ // TM
LAT_TILES = T_LAT // TM
CTX_TILES = T_CTX // TM
ALL_TILES = LAT_TILES + CTX_TILES
TP = TM
MOE_CHUNK = 2 * TM
P_LAT_TILES_PER_BATCH = SEQ // TP
P_LAT_TILES = T_LAT // TP
P_ALL_TILES = T_ALL // TP
HB = LANES

F32 = jnp.float32
BF16 = jnp.bfloat16

_SECTIONS = (
    ("aq", 0, N_ATT_HEADS, "pad"),
    ("ak", ATT_Q, N_ATT_KV, "pad"),
    ("av", ATT_Q + ATT_KV, N_ATT_KV, "pad"),
    ("nq", ATT_Q + 2 * ATT_KV, N_NA_HEADS, "par"),
    ("nk", ATT_Q + 2 * ATT_KV + NA_W, N_NA_HEADS, "dense"),
    ("nv", ATT_Q + 2 * ATT_KV + 2 * NA_W, N_NA_HEADS, "dense"),
    ("rq", ATT_Q + 2 * ATT_KV + 3 * NA_W, N_RET_HEADS, "dup"),
    ("rk", ATT_Q + 2 * ATT_KV + 3 * NA_W + RET_W, N_RET_HEADS, "dup"),
    ("rv", ATT_Q + 2 * ATT_KV + 3 * NA_W + 2 * RET_W, N_RET_HEADS, "dense"),
    ("rg", ATT_Q + 2 * ATT_KV + 3 * NA_W + 3 * RET_W, N_RET_HEADS, "dense"),
)
_SEC_OFF = {}
_off = 0
for _name, _src, _heads, _mode in _SECTIONS:
    _width = _heads * (HEAD_DIM if _mode == "dense" else HB)
    _SEC_OFF[_name] = (_off, _width)
    _off += _width
NC_PAD = _off
_FEATURE_MAJOR = ("aq", "av")
LOG2E = 1.4426950408889634
_YA_W = N_ATT_HEADS * HB
_YN_W = NA_W
_YR_W = RET_W
MIX_PAD = _YA_W + _YN_W + _YR_W


def _vmem_limit(nbytes):
    return int(min(VMEM_LIMIT_CAP, max(16 * 1024 * 1024, 2 * nbytes)))


def _pair_gains(v, heads):
    v = v.reshape(heads, 1, HEAD_DIM).astype(F32)
    even = (np.arange(heads) % 2 == 0)[:, None, None]
    zero = jnp.zeros_like(v)
    return jnp.concatenate([jnp.where(even, v, zero), jnp.where(even, zero, v)], axis=-1)


def _tile_mod_row(i, per_batch=LAT_TILES_PER_BATCH):
    return jnp.where(i < per_batch, 0, jnp.where(i < BATCH * per_batch, 1, 2))


def _ptile_mod_row(i):
    return _tile_mod_row(i, P_LAT_TILES_PER_BATCH)


ADA_TN = 1536


def _ada_kernel(c_ref, w_ref, b_ref, o_ref):
    c = c_ref[...]
    s = c * jax.nn.sigmoid(c)
    o_ref[0] = jnp.dot(s, w_ref[0], preferred_element_type=F32,
                       precision=lax.Precision.HIGHEST) + b_ref[0]


def _ada_mod(cvec, w_ada, b_ada):
    n = 6 * D_MODEL
    return pl.pallas_call(
        _ada_kernel,
        out_shape=jax.ShapeDtypeStruct((DEPTH, 8, n), F32),
        grid=(DEPTH, n // ADA_TN),
        in_specs=[
            pl.BlockSpec((8, D_MODEL), lambda l, j: (0, 0)),
            pl.BlockSpec((1, D_MODEL, ADA_TN), lambda l, j: (l, 0, j)),
            pl.BlockSpec((1, 1, ADA_TN), lambda l, j: (l, 0, j)),
        ],
        out_specs=pl.BlockSpec((1, 8, ADA_TN), lambda l, j: (l, 0, j)),
        compiler_params=pltpu.CompilerParams(
            dimension_semantics=("arbitrary", "arbitrary"),
            vmem_limit_bytes=_vmem_limit(2 * D_MODEL * ADA_TN * 4)),
        name="ada_mod",
    )(cvec, w_ada, b_ada.reshape(DEPTH, 1, n))


def _rope_swap(t):
    lane = lax.broadcasted_iota(jnp.int32, t.shape, 1)
    first_half = (lane % 32) < 16
    return jnp.where(first_half, pltpu.roll(t, LANES - 16, 1), pltpu.roll(t, 16, 1))


def _pick_rows(lat_ref, ctx_ref, split_ctx):
    if not split_ctx:
        return lat_ref[...]
    return jnp.where(pl.program_id(0) >= P_LAT_TILES, ctx_ref[...], lat_ref[...])


def _lat_ctx_specs(width, split_ctx):
    if split_ctx:
        return [pl.BlockSpec((TP, width), lambda i, *_: (jnp.minimum(i, P_LAT_TILES - 1), 0)),
                pl.BlockSpec((TP, width), lambda i, *_: (jnp.maximum(i - P_LAT_TILES, 0), 0))]
    return [pl.BlockSpec((TP, width), lambda i, *_: (i, 0)),
            pl.BlockSpec((TP, width), lambda i, *_: (0, 0))]


def _inproj_kernel(xl_ref, xc_ref, mod_ref, g1_ref, w_ref, cs_ref, sn_ref, qg_ref, kg_ref,
                   aq_ref, ak_ref, av_ref, nq_ref, nk_ref, nv_ref,
                   rq_ref, rk_ref, rv_ref, rg_ref, *, split_ctx):
    x = _pick_rows(xl_ref, xc_ref, split_ctx)
    mod = mod_ref[0]
    sh1 = mod[:, 0:D_MODEL]
    sc1 = mod[:, D_MODEL:2 * D_MODEL]
    ms = jnp.mean(x * x, axis=-1, keepdims=True)
    h = x * lax.rsqrt(ms + EPS) * g1_ref[...]
    h = (h * (1.0 + sc1) + sh1).astype(BF16)
    cs = cs_ref[...]
    sn = sn_ref[...]

    sections = {}

    def proj(name, hidx):
        if name not in sections:
            off, width = _SEC_OFF[name]
            sections[name] = jnp.dot(h, w_ref[:, off:off + width], preferred_element_type=F32)
        return sections[name][:, hidx * HB:(hidx + 1) * HB]

    def normed_rope(z, g):
        ss = jnp.sum(z * z, axis=-1, keepdims=True)
        zn = z * lax.rsqrt(ss * (1.0 / HEAD_DIM) + EPS) * g
        return zn * cs + _rope_swap(zn) * sn

    scale = HEAD_DIM ** -0.5
    for hh in range(N_ATT_HEADS):
        z = normed_rope(proj("aq", hh), qg_ref[...]) * (scale * LOG2E)
        aq_ref[hh * HB:(hh + 1) * HB, :] = z.T.astype(BF16)
    for hh in range(N_ATT_KV):
        z = normed_rope(proj("ak", hh), kg_ref[...])
        ak_ref[:, hh * HB:(hh + 1) * HB] = z.astype(BF16)
        zv = proj("av", hh)
        lane = lax.broadcasted_iota(jnp.int32, zv.shape, 1)
        av_ref[hh * HB:(hh + 1) * HB, :] = jnp.where(lane == HEAD_DIM, 1.0, zv).T.astype(BF16)
    def whole(name):
        off, width = _SEC_OFF[name]
        return jnp.dot(h, w_ref[:, off:off + width], preferred_element_type=F32)

    nq_ref[...] = (whole("nq") * scale).astype(BF16)
    nk_ref[...] = whole("nk").astype(BF16)
    nv_ref[...] = whole("nv").astype(BF16)
    rq_ref[...] = whole("rq").astype(BF16)
    rk_ref[...] = (whole("rk") * scale).astype(BF16)
    rv_ref[...] = whole("rv").astype(BF16)
    rg_ref[...] = whole("rg").astype(BF16)


def _inproj(x_lat, x_ctx, split_ctx, mod3, g1, w_pad, cs_tab, sn_tab, qg, kg):
    names = [s[0] for s in _SECTIONS]
    widths = [_SEC_OFF[n][1] for n in names]

    def tab_map(i):
        return (jnp.where(i < P_LAT_TILES, i % P_LAT_TILES_PER_BATCH, P_LAT_TILES_PER_BATCH), 0)

    est = (D_MODEL * NC_PAD * 2 + 4 * TP * D_MODEL * 4 + 2 * TP * NC_PAD * 2
           + 6 * TP * D_MODEL * 4)
    return pl.pallas_call(
        functools.partial(_inproj_kernel, split_ctx=split_ctx),
        out_shape=[jax.ShapeDtypeStruct((w, T_ALL) if n in _FEATURE_MAJOR else (T_ALL, w), BF16)
                   for n, w in zip(names, widths)],
        grid=(P_ALL_TILES,),
        in_specs=_lat_ctx_specs(D_MODEL, split_ctx) + [
            pl.BlockSpec((1, 1, 6 * D_MODEL), lambda i: (_ptile_mod_row(i), 0, 0)),
            pl.BlockSpec((1, D_MODEL), lambda i: (0, 0)),
            pl.BlockSpec((D_MODEL, NC_PAD), lambda i: (0, 0), pipeline_mode=pl.Buffered(1)),
            pl.BlockSpec((TP, HB), tab_map),
            pl.BlockSpec((TP, HB), tab_map),
            pl.BlockSpec((1, HB), lambda i: (0, 0)),
            pl.BlockSpec((1, HB), lambda i: (0, 0)),
        ],
        out_specs=[pl.BlockSpec((w, TP), lambda i: (0, i)) if n in _FEATURE_MAJOR
                   else pl.BlockSpec((TP, w), lambda i: (i, 0)) for n, w in zip(names, widths)],
        compiler_params=pltpu.CompilerParams(
            dimension_semantics=("arbitrary",), vmem_limit_bytes=_vmem_limit(est)),
        name="norm_inproj",
    )(x_lat, x_ctx, mod3, g1, w_pad, cs_tab, sn_tab, qg, kg)


ATT_TK = 8192
ATT_NK = SEQ // ATT_TK
ATT_CK = 256
ATT_VROWS = HEAD_DIM + 16


def _attn_kernel(qt_ref, k_ref, vt_ref, kc_ref, vct_ref, g_ref, o_ref, m_sc, acc_sc):
    i = pl.program_id(2)
    j = pl.program_id(3)
    is_ctx_q = i >= LAT_TILES_PER_BATCH

    @pl.when(j == 0)
    def _():
        m_sc[...] = jnp.full(m_sc.shape, -jnp.inf, F32)
        acc_sc[...] = jnp.zeros(acc_sc.shape, F32)

    def scores(kr, c, ck):
        k = kr[c * ck:(c + 1) * ck, :]
        return [jnp.dot(k, qt_ref[hh * HB:(hh + 1) * HB, :], preferred_element_type=F32)
                for hh in range(ATT_GRP)]

    def step(kr, vtr, nkeys):
        ck = min(ATT_CK, nkeys)
        nchunk = nkeys // ck
        ss = scores(kr, 0, ck)
        for c in range(nchunk):
            cur = ss
            if c + 1 < nchunk:
                ss = scores(kr, c + 1, ck)
            vt = vtr[0:ATT_VROWS, c * ck:(c + 1) * ck]
            ps, alphas = [], []
            for hh in range(ATT_GRP):
                m_prev = m_sc[hh]
                m_new = jnp.maximum(m_prev, jnp.max(cur[hh], axis=0, keepdims=True))
                alpha = jnp.exp2(m_prev - m_new)
                p = jnp.exp2(cur[hh] - m_new)
                m_sc[hh] = m_new
                ps.append(p.astype(BF16))
                alphas.append(alpha)
            for hh in range(ATT_GRP):
                acc_sc[hh] = alphas[hh] * acc_sc[hh] + jnp.dot(vt, ps[hh],
                                                               preferred_element_type=F32)

    @pl.when(jnp.logical_not(is_ctx_q))
    def _():
        step(k_ref, vt_ref, ATT_TK)

    @pl.when(j == ATT_NK - 1)
    def _():
        step(kc_ref, vct_ref, CTX_LEN)
        for hh in range(ATT_GRP):
            acc = acc_sc[hh]
            o = acc[0:HEAD_DIM] / acc[HEAD_DIM:HEAD_DIM + 1]
            ms = jnp.sum(o * o, axis=0, keepdims=True) * (1.0 / HEAD_DIM)
            g = g_ref[hh * HB:hh * HB + HEAD_DIM, :]
            y = o * lax.rsqrt(ms + EPS) * jnp.concatenate([g] * (TM // LANES), axis=1)
            y = jnp.concatenate([y, jnp.zeros_like(y)], axis=0)
            o_ref[:, hh * HB:(hh + 1) * HB] = y.T.astype(BF16)


def _gqa(aqt, ak, avt, ga_cols, want_ctx):
    nq = LAT_TILES_PER_BATCH + (1 if want_ctx else 0)

    def q_tile(b, i):
        return jnp.where(i < LAT_TILES_PER_BATCH, b * LAT_TILES_PER_BATCH + i, LAT_TILES + b)

    est = (2 * ATT_GRP * HB * TM * 2 + 4 * ATT_TK * HB * 2 + 4 * TM * HB * 2
           + ATT_GRP * TM * HB * 4 * 3 + 8 * TM * ATT_TK * 4)
    return pl.pallas_call(
        _attn_kernel,
        out_shape=jax.ShapeDtypeStruct((T_ALL if want_ctx else T_LAT, N_ATT_HEADS * HB), BF16),
        grid=(BATCH, N_ATT_KV, nq, ATT_NK),
        in_specs=[
            pl.BlockSpec((ATT_GRP * HB, TM), lambda b, c, i, j: (c, q_tile(b, i))),
            pl.BlockSpec((ATT_TK, HB), lambda b, c, i, j: (b * ATT_NK + j, c)),
            pl.BlockSpec((HB, ATT_TK), lambda b, c, i, j: (c, b * ATT_NK + j)),
            pl.BlockSpec((CTX_LEN, HB), lambda b, c, i, j: (LAT_TILES + b, c)),
            pl.BlockSpec((HB, CTX_LEN), lambda b, c, i, j: (c, LAT_TILES + b)),
            pl.BlockSpec((ATT_GRP * HB, LANES), lambda b, c, i, j: (c, 0)),
        ],
        out_specs=pl.BlockSpec((TM, ATT_GRP * HB), lambda b, c, i, j: (q_tile(b, i), c)),
        scratch_shapes=[
            pltpu.VMEM((ATT_GRP, 1, TM), F32),
            pltpu.VMEM((ATT_GRP, ATT_VROWS, TM), F32),
        ],
        compiler_params=pltpu.CompilerParams(
            dimension_semantics=("arbitrary",) * 4, vmem_limit_bytes=_vmem_limit(est)),
        name="gqa_attn",
    )(aqt, ak, avt, ak, avt, ga_cols)


NA_BAND = NA_WIN_ROWS * GRID_W
NA_CLASSES = 8
NA_ROWS_PER_ITER = 16
_NA_CLASS_ROWS = (0, 1, 2, 3, GRID_ROWS // 2, GRID_ROWS - 3, GRID_ROWS - 2, GRID_ROWS - 1)


def _na_bias_table(rpb):
    wr, wc = NA_WIN_ROWS, NA_WIN_COLS
    r = np.asarray(_NA_CLASS_ROWS)
    ridx = np.clip(r - wr // 2, 0, GRID_ROWS - wr)[:, None] + np.arange(wr)[None, :]
    dr = ridx - r[:, None] + (wr - 1)
    col = np.arange(GRID_W)
    cstart = np.clip(col - wc // 2, 0, GRID_W - wc)
    col_ok = (col[None, :] >= cstart[:, None]) & (col[None, :] < cstart[:, None] + wc)
    dc = np.clip(col[None, :] - col[:, None] + (wc - 1), 0, 2 * wc - 2)
    pick_r = (dr[:, :, None] == np.arange(2 * wr - 1)[None, None, :]).astype(np.float32)
    pick_c = (dc[:, :, None] == np.arange(2 * wc - 1)[None, None, :]).astype(np.float32)
    hp = lax.Precision.HIGHEST
    by_col = jnp.einsum("hrc,qkc->hrqk", rpb.astype(F32), jnp.asarray(pick_c), precision=hp)
    bias = jnp.einsum("hrqk,cwr->hcqwk", by_col, jnp.asarray(pick_r), precision=hp)
    bias = bias.reshape(N_NA_HEADS, NA_CLASSES, GRID_W, NA_BAND)
    mask = np.tile(col_ok, (1, wr))
    return jnp.where(mask[None, None], bias, NEG_INF)


def _head_rms_gain(o, g):
    ms = jnp.sum(o * o, axis=-1, keepdims=True) * (1.0 / HEAD_DIM)
    return o * lax.rsqrt(ms + EPS) * g


def _pair_store(ref, rows, o, g, odd, post=None):
    lane = lax.broadcasted_iota(jnp.int32, o.shape, 1)
    mine = (lane >= HEAD_DIM) if odd else (lane < HEAD_DIM)
    y = _head_rms_gain(jnp.where(mine, o, 0.0), g)
    if post is not None:
        y = y * post
    if odd:
        y = y + ref[rows, :].astype(F32)
    ref[rows, :] = y.astype(BF16)


def _for_head_parity(body):
    is_odd = pl.program_id(1) % 2 == 1
    pl.when(jnp.logical_not(is_odd))(functools.partial(body, False))
    pl.when(is_odd)(functools.partial(body, True))


def _na_kernel(q_ref, k_ref, v_ref, kc_ref, vc_ref, qc_ref, bias_ref, g_ref, o_ref, oc_ref,
               *, want_ctx):
    _for_head_parity(functools.partial(
        _na_body, q_ref, k_ref, v_ref, kc_ref, vc_ref, qc_ref, bias_ref, g_ref, o_ref, oc_ref,
        want_ctx))


def _na_body(q_ref, k_ref, v_ref, kc_ref, vc_ref, qc_ref, bias_ref, g_ref, o_ref, oc_ref,
             want_ctx, odd):
    kc = kc_ref[...]
    vc = vc_ref[...]
    g = g_ref[0]
    half = NA_WIN_ROWS // 2
    last = GRID_ROWS - NA_WIN_ROWS

    def rows(it, carry):
        r0 = it * NA_ROWS_PER_ITER
        qrows, bands, scores = [], [], []
        for d in range(NA_ROWS_PER_ITER):
            r = r0 + d
            start = jnp.clip(r - half, 0, last)
            cls = jnp.where(r < half, r, jnp.where(r > last + half, r - last, half))
            qrow = pl.ds(pl.multiple_of(r * GRID_W, GRID_W), GRID_W)
            band = pl.ds(pl.multiple_of(start * GRID_W, GRID_W), NA_BAND)
            q = q_ref[qrow, :]
            s = lax.dot_general(q, k_ref[band, :], (((1,), (1,)), ((), ())),
                                preferred_element_type=F32)
            sc = lax.dot_general(q, kc, (((1,), (1,)), ((), ())), preferred_element_type=F32)
            qrows.append(qrow)
            bands.append(band)
            scores.append((s, sc, cls))
        probs = []
        for s, sc, cls in scores:
            bt = bias_ref[0, cls]
            s = jnp.where(bt > 0.5 * NEG_INF, s + bt, NEG_INF)
            m = jnp.maximum(jnp.max(s, axis=-1, keepdims=True),
                            jnp.max(sc, axis=-1, keepdims=True))
            p = jnp.exp(s - m)
            pc = jnp.exp(sc - m)
            l = jnp.sum(p, axis=-1, keepdims=True) + jnp.sum(pc, axis=-1, keepdims=True)
            probs.append((p.astype(BF16), pc.astype(BF16), l))
        for qrow, band, (p, pc, l) in zip(qrows, bands, probs):
            o = (jnp.dot(p, v_ref[band, :], preferred_element_type=F32)
                 + jnp.dot(pc, vc, preferred_element_type=F32)) / l
            _pair_store(o_ref, qrow, o, g, odd)
        return carry

    lax.fori_loop(0, GRID_ROWS // NA_ROWS_PER_ITER, rows, 0)

    if want_ctx:
        sc = lax.dot_general(qc_ref[...], kc, (((1,), (1,)), ((), ())),
                             preferred_element_type=F32)
        m = jnp.max(sc, axis=-1, keepdims=True)
        pc = jnp.exp(sc - m)
        l = jnp.sum(pc, axis=-1, keepdims=True)
        o = jnp.dot(pc.astype(BF16), vc, preferred_element_type=F32) / l
        _pair_store(oc_ref, slice(None), o, g, odd)
    elif not odd:
        oc_ref[...] = jnp.zeros(oc_ref.shape, oc_ref.dtype)


def _pair_specs(rows, first_block=0):
    return (pl.BlockSpec((rows, HB), lambda b, h: (first_block + b, h)),
            pl.BlockSpec((rows, HB), lambda b, h: (first_block + b, h // 2)))


def _neigh(nq, nk, nv, bias_tab, gn, want_ctx):
    lat, lat_pair = _pair_specs(SEQ)
    ctx, ctx_pair = _pair_specs(CTX_LEN, T_LAT // CTX_LEN)
    est = 2 * (4 * SEQ * HB * 2 + 4 * CTX_LEN * HB * 2 + NA_CLASSES * GRID_W * NA_BAND * 4)
    return pl.pallas_call(
        functools.partial(_na_kernel, want_ctx=want_ctx),
        out_shape=[jax.ShapeDtypeStruct((T_LAT, _YN_W), BF16),
                   jax.ShapeDtypeStruct((T_CTX, _YN_W), BF16)],
        grid=(BATCH, N_NA_HEADS),
        in_specs=[lat, lat_pair, lat_pair, ctx_pair, ctx_pair, ctx,
                  pl.BlockSpec((1, NA_CLASSES, GRID_W, NA_BAND), lambda b, h: (h, 0, 0, 0)),
                  pl.BlockSpec((1, 1, HB), lambda b, h: (h, 0, 0))],
        out_specs=[_pair_specs(SEQ)[1], _pair_specs(CTX_LEN)[1]],
        compiler_params=pltpu.CompilerParams(
            dimension_semantics=("arbitrary", "arbitrary"), vmem_limit_bytes=_vmem_limit(est)),
        name="neigh_attn",
    )(nq, nk, nv, nk, nv, nq, bias_tab, gn)


RET_NCHUNK = SEQ // RET_CHUNK
RET_NCHUNK_CTX = CTX_LEN // RET_CHUNK
RET_CHUNKS_PER_ITER = 16


def _ret_tables(log_g2):
    lf = log_g2[0][:, None, None]
    lb = log_g2[1][:, None, None]
    pos = jnp.arange(RET_CHUNK, dtype=F32)
    i = pos[None, :, None]
    j = pos[None, None, :]
    diff = i - j
    dm = jnp.where(diff > 0, jnp.exp(lf * jnp.maximum(diff, 0.0)),
                   jnp.where(diff < 0, jnp.exp(lb * jnp.maximum(-diff, 0.0)), 2.0)) * 0.5
    fwd_lane = (jnp.arange(LANES) < HEAD_DIM)[None, None, :]
    xi = jnp.where(fwd_lane, jnp.exp(lf * (i + 1.0)), jnp.exp(lb * (RET_CHUNK - i)))
    zt = jnp.where(fwd_lane, jnp.exp(lf * (RET_CHUNK - 1.0 - i)), jnp.exp(lb * i))
    fwd_row = (jnp.arange(LANES) < HEAD_DIM)[None, :, None]
    dec = jnp.where(fwd_row, jnp.exp(lf * RET_CHUNK), jnp.exp(lb * RET_CHUNK))
    dec = jnp.broadcast_to(dec, (N_RET_HEADS, LANES, LANES))
    return dm.astype(F32), xi.astype(F32), zt.astype(F32), dec.astype(F32)


def _ret_kernel(*refs, want_ctx):
    _for_head_parity(functools.partial(_ret_body, *refs, want_ctx))


def _ret_body(q_ref, k_ref, v_ref, gt_ref, qc_ref, kc_ref, vc_ref, gtc_ref,
              dm_ref, xi_ref, zt_ref, dec_ref, g_ref, o_ref, oc_ref,
              u_sc, s_sc, uc_sc, sc_sc, want_ctx, odd):
    dm = dm_ref[0]
    xi = xi_ref[0]
    zt = zt_ref[0]
    dec = dec_ref[0]
    dec_f = dec[0:HEAD_DIM]
    dec_b = dec[HEAD_DIM:LANES]
    g = g_ref[0]
    C = RET_CHUNK

    def chunk_rows(n):
        return pl.ds(pl.multiple_of(n * C, C), C)

    def chunk_state_update(kr, vr, usc, ns):
        kzs = [(kr[chunk_rows(n), :].astype(F32) * zt).T.astype(BF16) for n in ns]
        for n, kz in zip(ns, kzs):
            usc[n] = jnp.dot(kz, vr[chunk_rows(n), :], preferred_element_type=F32)

    def chunk_out(qr, kr, vr, gtr, ssc, outr, ns):
        qds = [qr[chunk_rows(n), :] for n in ns]
        s2s = [lax.dot_general(qd, kr[chunk_rows(n), :], (((1,), (1,)), ((), ())),
                               preferred_element_type=F32) for n, qd in zip(ns, qds)]
        outs = []
        for n, qd, s2 in zip(ns, qds, s2s):
            inner = jnp.dot((s2 * dm).astype(BF16), vr[chunk_rows(n), :],
                            preferred_element_type=F32)
            qx = (qd.astype(F32) * xi).astype(BF16)
            outs.append(inner + jnp.dot(qx, ssc[n].astype(BF16), preferred_element_type=F32))
        for n, o in zip(ns, outs):
            gate = gtr[chunk_rows(n), :].astype(F32)
            _pair_store(outr, chunk_rows(n), o, g, odd, post=gate * jax.nn.sigmoid(gate))

    def scan_states(usc, ssc, nchunk, init_f, init_b):
        def fwd(n, sf):
            ssc[n, 0:HEAD_DIM, :] = sf
            return dec_f * sf + usc[n, 0:HEAD_DIM, :]

        def bwd(t, sb):
            n = nchunk - 1 - t
            ssc[n, HEAD_DIM:LANES, :] = sb
            return dec_b * sb + usc[n, HEAD_DIM:LANES, :]

        return (lax.fori_loop(0, nchunk, fwd, init_f), lax.fori_loop(0, nchunk, bwd, init_b))

    zero = jnp.zeros((HEAD_DIM, LANES), F32)
    ctx_chunks = list(range(RET_NCHUNK_CTX))
    chunk_state_update(kc_ref, vc_ref, uc_sc, ctx_chunks)
    ctx_f, ctx_b = scan_states(uc_sc, sc_sc, RET_NCHUNK_CTX, zero, zero)
    if want_ctx:
        chunk_out(qc_ref, kc_ref, vc_ref, gtc_ref, sc_sc, oc_ref, ctx_chunks)
    elif not odd:
        oc_ref[...] = jnp.zeros(oc_ref.shape, oc_ref.dtype)

    def upd(it, carry):
        chunk_state_update(k_ref, v_ref, u_sc,
                           [it * RET_CHUNKS_PER_ITER + d for d in range(RET_CHUNKS_PER_ITER)])
        return carry

    lax.fori_loop(0, RET_NCHUNK // RET_CHUNKS_PER_ITER, upd, 0)
    scan_states(u_sc, s_sc, RET_NCHUNK, ctx_f, ctx_b)

    def out(it, carry):
        chunk_out(q_ref, k_ref, v_ref, gt_ref, s_sc, o_ref,
                  [it * RET_CHUNKS_PER_ITER + d for d in range(RET_CHUNKS_PER_ITER)])
        return carry

    lax.fori_loop(0, RET_NCHUNK // RET_CHUNKS_PER_ITER, out, 0)


def _retention(rq, rk, rv, rg, tables, gr, want_ctx):
    lat, lat_pair = _pair_specs(SEQ)
    ctx, ctx_pair = _pair_specs(CTX_LEN, T_LAT // CTX_LEN)
    tab = pl.BlockSpec((1, LANES, LANES), lambda b, h: (h, 0, 0))
    est = (2 * 5 * SEQ * HB * 2 + 2 * RET_NCHUNK * LANES * LANES * 4 + 8 * LANES * LANES * 4)
    return pl.pallas_call(
        functools.partial(_ret_kernel, want_ctx=want_ctx),
        out_shape=[jax.ShapeDtypeStruct((T_LAT, _YR_W), BF16),
                   jax.ShapeDtypeStruct((T_CTX, _YR_W), BF16)],
        grid=(BATCH, N_RET_HEADS),
        in_specs=[lat, lat, lat_pair, lat_pair, ctx, ctx, ctx_pair, ctx_pair, tab, tab, tab, tab,
                  pl.BlockSpec((1, 1, HB), lambda b, h: (h, 0, 0))],
        out_specs=[_pair_specs(SEQ)[1], _pair_specs(CTX_LEN)[1]],
        scratch_shapes=[
            pltpu.VMEM((RET_NCHUNK, LANES, LANES), F32),
            pltpu.VMEM((RET_NCHUNK, LANES, LANES), F32),
            pltpu.VMEM((RET_NCHUNK_CTX, LANES, LANES), F32),
            pltpu.VMEM((RET_NCHUNK_CTX, LANES, LANES), F32),
        ],
        compiler_params=pltpu.CompilerParams(
            dimension_semantics=("arbitrary", "arbitrary"), vmem_limit_bytes=_vmem_limit(est)),
        name="retention",
    )(rq, rk, rv, rg, rq, rk, rv, rg, *tables, gr)


MOE_SLOT = TM
META_ROWS = 8
_META_GATE0 = 2


def _route_t(logt):
    row = lax.broadcasted_iota(jnp.int32, logt.shape, 0).astype(F32)
    p = jnp.exp(logt - jnp.max(logt, axis=0, keepdims=True))
    best = None
    for grp in range(N_EXPERT_GROUPS):
        lo = float(grp * EXPERTS_PER_GROUP)
        ing = (row >= lo) & (row < lo + EXPERTS_PER_GROUP)
        pg = jnp.where(ing, p, -1.0)
        m1 = jnp.max(pg, axis=0, keepdims=True)
        i1 = jnp.min(jnp.where(pg == m1, row, float(N_EXPERTS)), axis=0, keepdims=True)
        pg2 = jnp.where(row == i1, -1.0, pg)
        m2 = jnp.max(pg2, axis=0, keepdims=True)
        i2 = jnp.min(jnp.where(pg2 == m2, row, float(N_EXPERTS)), axis=0, keepdims=True)
        cand = (m1 + m2, m1, m2, i1, i2, jnp.zeros_like(m1) + lo)
        if best is None:
            best = cand
        else:
            better = cand[0] > best[0]
            best = tuple(jnp.where(better, c, b) for c, b in zip(cand, best))
    _, m1, m2, i1, i2, base = best
    w = m1 + m2
    gates = [jnp.where(i1 == base + e, m1 / w, jnp.where(i2 == base + e, m2 / w, 0.0))
             for e in range(EXPERTS_PER_GROUP)]
    return base * (1.0 / EXPERTS_PER_GROUP), gates


def _merge_kernel(ya_ref, ynl_ref, ync_ref, yrl_ref, yrc_ref, xl_ref, xc_ref, mod_ref, g2_ref,
                  wo_ref, wr_ref, brt_ref,
                  xn_ref, h2_ref, metat_ref, metac_ref, ctab_ref, tot_ref, carry_sc, *, split_ctx):
    @pl.when(pl.program_id(0) == 0)
    def _():
        carry_sc[...] = jnp.zeros(carry_sc.shape, F32)

    mod = mod_ref[0]
    gt1 = mod[:, 2 * D_MODEL:3 * D_MODEL]
    sh2 = mod[:, 3 * D_MODEL:4 * D_MODEL]
    sc2 = mod[:, 4 * D_MODEL:5 * D_MODEL]
    yn = _pick_rows(ynl_ref, ync_ref, split_ctx)
    yr = _pick_rows(yrl_ref, yrc_ref, split_ctx)
    m = (jnp.dot(ya_ref[...], wo_ref[0:_YA_W, :], preferred_element_type=F32)
         + jnp.dot(yn, wo_ref[_YA_W:_YA_W + _YN_W, :], preferred_element_type=F32)
         + jnp.dot(yr, wo_ref[_YA_W + _YN_W:MIX_PAD, :], preferred_element_type=F32))
    x = _pick_rows(xl_ref, xc_ref, split_ctx) + gt1 * m
    xn_ref[...] = x
    ms = jnp.mean(x * x, axis=-1, keepdims=True)
    h2 = x * lax.rsqrt(ms + EPS) * g2_ref[...] * (1.0 + sc2) + sh2
    h2_ref[...] = h2.astype(BF16)

    ntile = TP // LANES
    h_hi = h2.astype(BF16)
    h_lo = (h2 - h_hi.astype(F32)).astype(BF16)
    wr = wr_ref[...]
    w_hi = wr.astype(BF16)
    w_lo = (wr - w_hi.astype(F32)).astype(BF16)
    logits = (jnp.dot(h_hi, w_hi, preferred_element_type=F32)
              + jnp.dot(h_lo, w_hi, preferred_element_type=F32)
              + jnp.dot(h_hi, w_lo, preferred_element_type=F32))
    logt = logits.T[0:N_EXPERTS, :]
    gsel, gates = _route_t(logt + jnp.concatenate([brt_ref[...]] * ntile, axis=1))

    grow = lax.broadcasted_iota(jnp.int32, (META_ROWS, TP), 0).astype(F32)
    onehot = jnp.where(grow == gsel, 1.0, 0.0)
    earlier = (lax.broadcasted_iota(jnp.int32, (TP, TP), 0)
               < lax.broadcasted_iota(jnp.int32, (TP, TP), 1))
    excl = jnp.dot(onehot.astype(BF16), jnp.where(earlier, 1.0, 0.0).astype(BF16),
                   preferred_element_type=F32)
    carry = carry_sc[...]
    rank = jnp.sum(onehot * (jnp.concatenate([carry] * ntile, axis=1) + excl),
                   axis=0, keepdims=True)
    for sub in range(TP // TM):
        ctab_ref[sub] = carry
        carry = carry + jnp.sum(onehot[:, sub * TM:(sub + 1) * TM], axis=1, keepdims=True)
    carry_sc[...] = carry
    tot_ref[...] = carry

    metat = jnp.concatenate([gsel, rank] + gates
                            + [jnp.zeros((META_ROWS - _META_GATE0 - EXPERTS_PER_GROUP, TP), F32)],
                            axis=0)
    metat_ref[...] = metat
    metac_ref[...] = jnp.concatenate([metat, jnp.zeros((LANES - META_ROWS, TP), F32)], axis=0).T


def _merge(ya, yn_lat, yn_ctx, yr_lat, yr_ctx, x_lat, x_ctx, split_ctx, mod3, g2, wo_pad, wr_pad,
           brt, ntiles):
    rows = ntiles * TP
    est = (2 * MIX_PAD * D_MODEL * 2 + 4 * TP * MIX_PAD * 2 + 10 * TP * D_MODEL * 4
           + 4 * TP * TP * 4)
    const = lambda i: (0, 0)
    return pl.pallas_call(
        functools.partial(_merge_kernel, split_ctx=split_ctx),
        out_shape=[jax.ShapeDtypeStruct((rows, D_MODEL), F32),
                   jax.ShapeDtypeStruct((rows, D_MODEL), BF16),
                   jax.ShapeDtypeStruct((META_ROWS, rows), F32),
                   jax.ShapeDtypeStruct((rows, LANES), F32),
                   jax.ShapeDtypeStruct((ntiles * (TP // TM), META_ROWS, LANES), F32),
                   jax.ShapeDtypeStruct((META_ROWS, LANES), F32)],
        grid=(ntiles,),
        in_specs=([pl.BlockSpec((TP, _YA_W), lambda i: (i, 0))]
                  + _lat_ctx_specs(_YN_W, split_ctx) + _lat_ctx_specs(_YR_W, split_ctx)
                  + _lat_ctx_specs(D_MODEL, split_ctx) + [
            pl.BlockSpec((1, 1, 6 * D_MODEL), lambda i: (_ptile_mod_row(i), 0, 0)),
            pl.BlockSpec((1, D_MODEL), const),
            pl.BlockSpec((MIX_PAD, D_MODEL), const),
            pl.BlockSpec((D_MODEL, LANES), const),
            pl.BlockSpec((N_EXPERTS, LANES), const),
        ]),
        out_specs=[pl.BlockSpec((TP, D_MODEL), lambda i: (i, 0)),
                   pl.BlockSpec((TP, D_MODEL), lambda i: (i, 0)),
                   pl.BlockSpec((META_ROWS, TP), lambda i: (0, i)),
                   pl.BlockSpec((TP, LANES), lambda i: (i, 0)),
                   pl.BlockSpec((TP // TM, META_ROWS, LANES), lambda i: (i, 0, 0)),
                   pl.BlockSpec((META_ROWS, LANES), const)],
        scratch_shapes=[pltpu.VMEM((META_ROWS, LANES), F32)],
        compiler_params=pltpu.CompilerParams(
            dimension_semantics=("arbitrary",), vmem_limit_bytes=_vmem_limit(est)),
        name="merge_outproj_router",
    )(ya, yn_lat, yn_ctx, yr_lat, yr_ctx, x_lat, x_ctx, mod3, g2, wo_pad, wr_pad, brt)


def _moe_plan(ctab, tot, ntiles):
    grp = N_EXPERT_GROUPS
    i32 = jnp.int32
    a = ctab[:, :grp, 0].astype(i32).T
    totg = tot[:grp, 0].astype(i32)
    b = jnp.concatenate([a[:, 1:], totg[:, None]], axis=1)
    nslot = (totg + MOE_SLOT - 1) // MOE_SLOT
    slot_end = jnp.cumsum(nslot)
    slot_base = slot_end - nslot
    total_slots = slot_end[-1]
    nchunk = ntiles // (MOE_CHUNK // TM)
    ac = a[:, ::MOE_CHUNK // TM]
    bc = jnp.concatenate([ac[:, 1:], totg[:, None]], axis=1)
    first_j = ac // MOE_SLOT
    last_j = (jnp.maximum(bc, 1) - 1) // MOE_SLOT
    npairs = jnp.where(bc > ac, last_j - first_j + 1, 0).reshape(-1)
    cum = jnp.cumsum(npairs)
    start = cum - npairs
    total_pairs = cum[-1]
    n_pairs_max = grp * nchunk + ntiles + grp
    pidx = jnp.arange(n_pairs_max, dtype=i32)
    p = jnp.minimum(pidx, total_pairs - 1)
    gc = jnp.sum((cum[None, :] <= p[:, None]).astype(i32), axis=1)
    slot = slot_base[gc // nchunk] + first_j.reshape(-1)[gc] + (p - start[gc])
    valid = pidx < total_pairs
    prev_slot = jnp.concatenate([jnp.full((1,), -1, i32), slot[:-1]])
    next_slot = jnp.concatenate([slot[1:], jnp.full((1,), -1, i32)])
    is_first = valid & (slot != prev_slot)
    is_last = valid & ((slot != next_slot) | (pidx == total_pairs - 1))
    n_slots_max = ntiles + grp
    fill = jnp.logical_not(valid) & (total_slots < n_slots_max)
    slot = jnp.where(fill, jnp.minimum(total_slots + pidx - total_pairs, n_slots_max - 1), slot)
    flags = (is_first.astype(i32) + 2 * is_last.astype(i32) + 4 * valid.astype(i32)
             + 8 * fill.astype(i32))
    sidx = jnp.arange(n_slots_max, dtype=i32)
    sgrp = jnp.minimum(jnp.sum((slot_end[None, :] <= sidx[:, None]).astype(i32), axis=1), grp - 1)
    sr0 = (sidx - slot_base[sgrp]) * MOE_SLOT
    row0 = slot_base[:, None] * MOE_SLOT + a
    row1 = slot_base[:, None] * MOE_SLOT + jnp.maximum(b, a + 1) - 1
    wb = jnp.stack([row0 // MOE_SLOT, row1 // MOE_SLOT], axis=1)
    wb = jnp.clip(wb, 0, total_slots - 1).reshape(-1).astype(i32)
    return dict(pslot=slot.astype(i32), pchunk=(gc % nchunk).astype(i32), pflag=flags,
                sgrp=sgrp, sr0=sr0.astype(i32), wb=wb, sbase=slot_base.astype(i32),
                n_pairs=n_pairs_max, n_slots=n_slots_max)


def _moe_kernel(pslot_ref, pchunk_ref, pflag_ref, sgrp_ref, sr0_ref,
                h_ref, metat_ref, metac_ref, w1_ref, w3_ref, w2_ref, o_ref, x_sc, g_sc):
    p = pl.program_id(0)
    flags = pflag_ref[p]
    slot = pslot_ref[p]

    @pl.when((flags & 8) != 0)
    def _():
        o_ref[...] = jnp.zeros(o_ref.shape, o_ref.dtype)

    @pl.when((flags & 1) != 0)
    def _():
        x_sc[...] = jnp.zeros(x_sc.shape, F32)
        g_sc[...] = jnp.zeros(g_sc.shape, F32)

    @pl.when((flags & 4) != 0)
    def _():
        mt = metat_ref[...].astype(jnp.int32)
        want = lax.broadcasted_iota(jnp.int32, (MOE_SLOT, MOE_CHUNK), 0) + sr0_ref[slot]
        sel = (mt[1:2, :] == want) & (mt[0:1, :] == sgrp_ref[slot])
        pm = jnp.where(sel, 1.0, 0.0).astype(BF16)
        x_sc[...] += jnp.dot(pm, h_ref[...], preferred_element_type=F32)
        mc = metac_ref[...]
        hi = mc.astype(BF16)
        r1 = mc - hi.astype(F32)
        mid = r1.astype(BF16)
        lo = (r1 - mid.astype(F32)).astype(BF16)
        g_sc[...] += (jnp.dot(pm, hi, preferred_element_type=F32)
                      + jnp.dot(pm, mid, preferred_element_type=F32)
                      + jnp.dot(pm, lo, preferred_element_type=F32))

    @pl.when((flags & 2) != 0)
    def _():
        x = x_sc[...].astype(BF16)
        gs = g_sc[...]
        acc = jnp.zeros((MOE_SLOT, D_MODEL), F32)
        for e in range(EXPERTS_PER_GROUP):
            a = jnp.dot(x, w1_ref[0, 0, e], preferred_element_type=F32)
            b = jnp.dot(x, w3_ref[0, 0, e], preferred_element_type=F32)
            gate = gs[:, _META_GATE0 + e:_META_GATE0 + e + 1]
            act = (a * jax.nn.sigmoid(a)) * b * gate
            acc = acc + jnp.dot(act.astype(BF16), w2_ref[0, 0, e], preferred_element_type=F32)
        o_ref[...] = acc.astype(BF16)


def _moe_sorted(plan, h2, metat, metac, w1g, w3g, w2g, layer):
    est = (2 * 3 * EXPERTS_PER_GROUP * D_MODEL * D_EXPERT * 2 + 6 * MOE_CHUNK * D_MODEL * 2
           + 2 * MOE_SLOT * D_MODEL * 4 + 6 * MOE_SLOT * D_EXPERT * 4
           + 4 * MOE_CHUNK * MOE_SLOT * 4)
    wmap = lambda p, ps, pc, pf, sg, sr: (layer, sg[ps[p]], 0, 0, 0)
    grid_spec = pltpu.PrefetchScalarGridSpec(
        num_scalar_prefetch=5,
        grid=(plan["n_pairs"],),
        in_specs=[
            pl.BlockSpec((MOE_CHUNK, D_MODEL), lambda p, ps, pc, pf, sg, sr: (pc[p], 0)),
            pl.BlockSpec((META_ROWS, MOE_CHUNK), lambda p, ps, pc, pf, sg, sr: (0, pc[p])),
            pl.BlockSpec((MOE_CHUNK, LANES), lambda p, ps, pc, pf, sg, sr: (pc[p], 0)),
            pl.BlockSpec((1, 1, EXPERTS_PER_GROUP, D_MODEL, D_EXPERT), wmap),
            pl.BlockSpec((1, 1, EXPERTS_PER_GROUP, D_MODEL, D_EXPERT), wmap),
            pl.BlockSpec((1, 1, EXPERTS_PER_GROUP, D_EXPERT, D_MODEL), wmap),
        ],
        out_specs=pl.BlockSpec((MOE_SLOT, D_MODEL), lambda p, ps, pc, pf, sg, sr: (ps[p], 0)),
        scratch_shapes=[pltpu.VMEM((MOE_SLOT, D_MODEL), F32), pltpu.VMEM((MOE_SLOT, LANES), F32)],
    )
    return pl.pallas_call(
        _moe_kernel,
        out_shape=jax.ShapeDtypeStruct((plan["n_slots"] * MOE_SLOT, D_MODEL), BF16),
        grid_spec=grid_spec,
        compiler_params=pltpu.CompilerParams(
            dimension_semantics=("arbitrary",), vmem_limit_bytes=_vmem_limit(est)),
        name="moe_sorted_experts",
    )(plan["pslot"], plan["pchunk"], plan["pflag"], plan["sgrp"], plan["sr0"],
      h2, metat, metac, w1g, w3g, w2g)


_COMBINE_WINDOWS = 2 * N_EXPERT_GROUPS


def _combine_kernel(wb_ref, sb_ref, *refs, ntiles, final):
    y_refs = refs[:_COMBINE_WINDOWS]
    metac_ref, xn_ref, mod_ref, fg_ref, o_ref, m_sc = refs[_COMBINE_WINDOWS:]
    i = pl.program_id(0)
    mc = metac_ref[...]
    grp = mc[:, 0:1].astype(jnp.int32)
    rank = mc[:, 1:2].astype(jnp.int32)
    col = lax.broadcasted_iota(jnp.int32, (TM, MOE_SLOT), 1)

    def window(g, k):
        blk = wb_ref[(2 * g + k) * ntiles + i]
        sel = (grp == g) & (rank + sb_ref[g] * MOE_SLOT == col + blk * MOE_SLOT)
        return jnp.dot(jnp.where(sel, 1.0, 0.0).astype(BF16), y_refs[2 * g + k][...],
                       preferred_element_type=F32)

    m = window(0, 0)
    for g in range(1, N_EXPERT_GROUPS):
        m = m + window(g, 0)
    m_sc[...] = m
    for g in range(N_EXPERT_GROUPS):
        @pl.when(wb_ref[(2 * g + 1) * ntiles + i] != wb_ref[(2 * g) * ntiles + i])
        def _(g=g):
            m_sc[...] += window(g, 1)
    gt2 = mod_ref[0][:, 5 * D_MODEL:6 * D_MODEL]
    x = xn_ref[...] + gt2 * m_sc[...]
    if final:
        ms = jnp.mean(x * x, axis=-1, keepdims=True)
        x = x * lax.rsqrt(ms + EPS) * fg_ref[...]
    o_ref[...] = x


def _combine(plan, ys, metac, xn, mod3, fg, ntiles, final):
    def ymap(w):
        return lambda i, wb, sb: (wb[w * ntiles + i], 0)

    est = (2 * _COMBINE_WINDOWS * MOE_SLOT * D_MODEL * 2 + 8 * TM * D_MODEL * 4
           + 4 * TM * MOE_SLOT * 4)
    grid_spec = pltpu.PrefetchScalarGridSpec(
        num_scalar_prefetch=2,
        grid=(ntiles,),
        in_specs=[pl.BlockSpec((MOE_SLOT, D_MODEL), ymap(w)) for w in range(_COMBINE_WINDOWS)] + [
            pl.BlockSpec((TM, LANES), lambda i, wb, sb: (i, 0)),
            pl.BlockSpec((TM, D_MODEL), lambda i, wb, sb: (i, 0)),
            pl.BlockSpec((1, 1, 6 * D_MODEL), lambda i, wb, sb: (_tile_mod_row(i), 0, 0)),
            pl.BlockSpec((1, D_MODEL), lambda i, wb, sb: (0, 0)),
        ],
        out_specs=pl.BlockSpec((TM, D_MODEL), lambda i, wb, sb: (i, 0)),
        scratch_shapes=[pltpu.VMEM((TM, D_MODEL), F32)],
    )
    return pl.pallas_call(
        functools.partial(_combine_kernel, ntiles=ntiles, final=final),
        out_shape=jax.ShapeDtypeStruct((ntiles * TM, D_MODEL), F32),
        grid_spec=grid_spec,
        compiler_params=pltpu.CompilerParams(
            dimension_semantics=("arbitrary",), vmem_limit_bytes=_vmem_limit(est)),
        name="moe_combine",
    )(plan["wb"], plan["sbase"], *([ys] * _COMBINE_WINDOWS), metac, xn, mod3, fg)


def _rope_tables():
    t = np.arange(SEQ)
    nf = HEAD_DIM // 4
    inv = (np.float32(ROPE_THETA) ** (-np.arange(nf, dtype=np.float32) / np.float32(nf)))
    inv = inv.astype(np.float32)
    ang_r = (t // GRID_W).astype(np.float32)[:, None] * inv[None, :]
    ang_c = (t % GRID_W).astype(np.float32)[:, None] * inv[None, :]
    cr, sr, cc, sc = np.cos(ang_r), np.sin(ang_r), np.cos(ang_c), np.sin(ang_c)
    zeros = np.zeros((SEQ, HEAD_DIM), np.float32)
    cs = np.concatenate([cr, cr, cc, cc, zeros], axis=-1)
    sn = np.concatenate([-sr, sr, -sc, sc, zeros], axis=-1)
    ident = np.concatenate([np.ones((T_CTX, HEAD_DIM), np.float32),
                            np.zeros((T_CTX, HEAD_DIM), np.float32)], axis=-1)
    cs = np.concatenate([cs, ident], axis=0).astype(np.float32)
    sn = np.concatenate([sn, np.zeros((T_CTX, LANES), np.float32)], axis=0).astype(np.float32)
    return jnp.asarray(cs), jnp.asarray(sn)


def _pad_out_weight(w_out_l):
    w = w_out_l[:ATT_Q].reshape(N_ATT_HEADS, HEAD_DIM, D_MODEL)
    w = jnp.concatenate([w, jnp.zeros_like(w)], axis=1).reshape(_YA_W, D_MODEL)
    return jnp.concatenate([w, w_out_l[ATT_Q:]], axis=0).astype(BF16)


def _pad_in_weight(w_in_l):
    parts = []
    for _, src, heads, mode in _SECTIONS:
        w = w_in_l[:, src:src + heads * HEAD_DIM]
        if mode != "dense":
            w = w.reshape(D_MODEL, heads, HEAD_DIM)
            zero = jnp.zeros_like(w)
            if mode == "par":
                even = (np.arange(heads) % 2 == 0)[None, :, None]
                halves = [jnp.where(even, w, zero), jnp.where(even, zero, w)]
            else:
                halves = [w, w if mode == "dup" else zero]
            w = jnp.concatenate(halves, axis=-1).reshape(D_MODEL, heads * HB)
        parts.append(w)
    return jnp.concatenate(parts, axis=-1).astype(BF16)


def kernel(x, c, ctx, c_ctx, w_ada, b_ada, norm1_g, norm2_g, w_in, q_norm_g, k_norm_g, na_rpb,
           ret_decay, mix_g, w_out, w_router, b_router, w_exp1, w_exp3, w_exp2, final_g):
    cs_tab, sn_tab = _rope_tables()

    cvec = jnp.concatenate([c, c_ctx[None, :], jnp.zeros((8 - BATCH - 1, D_MODEL), F32)], axis=0)
    mod_all = _ada_mod(cvec, w_ada, b_ada)

    wr_pad = jnp.concatenate([w_router, jnp.zeros((D_MODEL, LANES - N_EXPERTS), F32)], axis=1)
    brt = jnp.broadcast_to(b_router[:, None], (N_EXPERTS, LANES))
    zero_lane = jnp.zeros((HEAD_DIM,), F32)

    x_lat = x.reshape(T_LAT, D_MODEL)
    x_ctx = ctx.reshape(T_CTX, D_MODEL)

    gshape = (DEPTH, N_EXPERT_GROUPS, EXPERTS_PER_GROUP)
    w1g = w_exp1.astype(BF16).reshape(gshape + (D_MODEL, D_EXPERT))
    w3g = w_exp3.astype(BF16).reshape(gshape + (D_MODEL, D_EXPERT))
    w2g = w_exp2.astype(BF16).reshape(gshape + (D_EXPERT, D_MODEL))

    for l in range(DEPTH):
        last = l == DEPTH - 1
        want_ctx = not last
        split_ctx = l == 0
        mod3 = mod_all[l].reshape(8, 1, 6 * D_MODEL)
        w_pad = _pad_in_weight(w_in[l])
        qg = jnp.concatenate([q_norm_g[l], zero_lane])[None, :]
        kg = jnp.concatenate([k_norm_g[l], zero_lane])[None, :]
        aq, ak, av, nq, nk, nv, rq, rk, rv, rg = _inproj(
            x_lat, x_ctx, split_ctx, mod3, norm1_g[l][None, :], w_pad, cs_tab, sn_tab, qg, kg)

        ga = jnp.broadcast_to(mix_g[l][:ATT_Q].reshape(N_ATT_HEADS, HEAD_DIM, 1),
                              (N_ATT_HEADS, HEAD_DIM, LANES))
        ga = jnp.concatenate([ga, jnp.zeros_like(ga)], axis=1).reshape(N_ATT_HEADS * HB, LANES)
        gn = _pair_gains(mix_g[l][ATT_Q:ATT_Q + NA_W], N_NA_HEADS)
        gr = _pair_gains(mix_g[l][ATT_Q + NA_W:], N_RET_HEADS)

        ya = _gqa(aq, ak, av, ga, want_ctx)
        yn_lat, yn_ctx = _neigh(nq, nk, nv, _na_bias_table(na_rpb[l]), gn, want_ctx)
        log_g2 = jax.nn.log_sigmoid(ret_decay[l].astype(F32))
        yr_lat, yr_ctx = _retention(rq, rk, rv, rg, _ret_tables(log_g2), gr, want_ctx)

        wo_pad = _pad_out_weight(w_out[l])
        ntiles = LAT_TILES if last else ALL_TILES
        xn, h2, metat, metac, ctab, tot = _merge(
            ya, yn_lat, yn_ctx, yr_lat, yr_ctx, x_lat, x_ctx, split_ctx, mod3,
            norm2_g[l][None, :], wo_pad, wr_pad, brt, ntiles // (TP // TM))
        plan = _moe_plan(ctab, tot, ntiles)
        ys = _moe_sorted(plan, h2, metat, metac, w1g, w3g, w2g, l)
        x_lat = _combine(plan, ys, metac, xn, mod3, final_g[None, :], ntiles, last)
        x_ctx = x_lat

    return x_lat.reshape(BATCH, SEQ, D_MODEL)
```

```python
import functools

import numpy as np
import jax
import jax.numpy as jnp
from jax import lax
from jax.experimental import pallas as pl
from jax.experimental.pallas import tpu as pltpu

D_MODEL = 1024
BATCH = 2
SEQ = 8192
DEPTH = 2
GRID_W = 64
GRID_ROWS = SEQ // GRID_W
CTX_LEN = 256
HEAD_DIM = 64
N_ATT_HEADS = 6
N_ATT_KV = 2
ATT_GRP = N_ATT_HEADS // N_ATT_KV
N_NA_HEADS = 4
N_RET_HEADS = 6
ATT_Q = N_ATT_HEADS * HEAD_DIM
ATT_KV = N_ATT_KV * HEAD_DIM
NA_W = N_NA_HEADS * HEAD_DIM
RET_W = N_RET_HEADS * HEAD_DIM
NA_WIN_ROWS = 8
NA_WIN_COLS = 16
RET_CHUNK = 128
ROPE_THETA = 10000.0
N_EXPERTS = 16
N_EXPERT_GROUPS = 4
EXPERTS_PER_GROUP = N_EXPERTS // N_EXPERT_GROUPS
D_EXPERT = 512
EPS = 1e-6
NEG_INF = -1e30

LANES = 128
VMEM_LIMIT_CAP = 56 * 1024 * 1024

T_LAT = BATCH * SEQ
T_CTX = BATCH * CTX_LEN
T_ALL = T_LAT + T_CTX
TM = 256
LAT_TILES_PER_BATCH = SEQ // TM
LAT_TILES = T_LAT // TM
CTX_TILES = T_CTX // TM
ALL_TILES = LAT_TILES + CTX_TILES
TP = TM
MOE_CHUNK = 2 * TM
P_LAT_TILES_PER_BATCH = SEQ // TP
P_LAT_TILES = T_LAT // TP
P_ALL_TILES = T_ALL // TP
HB = LANES

F32 = jnp.float32
BF16 = jnp.bfloat16

_SECTIONS = (
    ("aq", 0, N_ATT_HEADS, "pad"),
    ("ak", ATT_Q, N_ATT_KV, "pad"),
    ("av", ATT_Q + ATT_KV, N_ATT_KV, "pad"),
    ("nq", ATT_Q + 2 * ATT_KV, N_NA_HEADS, "par"),
    ("nk", ATT_Q + 2 * ATT_KV + NA_W, N_NA_HEADS, "dense"),
    ("nv", ATT_Q + 2 * ATT_KV + 2 * NA_W, N_NA_HEADS, "dense"),
    ("rq", ATT_Q + 2 * ATT_KV + 3 * NA_W, N_RET_HEADS, "dup"),
    ("rk", ATT_Q + 2 * ATT_KV + 3 * NA_W + RET_W, N_RET_HEADS, "dup"),
    ("rv", ATT_Q + 2 * ATT_KV + 3 * NA_W + 2 * RET_W, N_RET_HEADS, "dense"),
    ("rg", ATT_Q + 2 * ATT_KV + 3 * NA_W + 3 * RET_W, N_RET_HEADS, "dense"),
)
_SEC_OFF = {}
_off = 0
for _name, _src, _heads, _mode in _SECTIONS:
    _width = _heads * (HEAD_DIM if _mode == "dense" else HB)
    _SEC_OFF[_name] = (_off, _width)
    _off += _width
NC_PAD = _off
_FEATURE_MAJOR = ("aq", "av")
LOG2E = 1.4426950408889634
_YA_W = N_ATT_HEADS * HB
_YN_W = NA_W
_YR_W = RET_W
MIX_PAD = _YA_W + _YN_W + _YR_W


def _vmem_limit(nbytes):
    return int(min(VMEM_LIMIT_CAP, max(16 * 1024 * 1024, 2 * nbytes)))


def _pair_gains(v, heads):
    v = v.reshape(heads, 1, HEAD_DIM).astype(F32)
    even = (np.arange(heads) % 2 == 0)[:, None, None]
    zero = jnp.zeros_like(v)
    return jnp.concatenate([jnp.where(even, v, zero), jnp.where(even, zero, v)], axis=-1)


def _tile_mod_row(i, per_batch=LAT_TILES_PER_BATCH):
    return jnp.where(i < per_batch, 0, jnp.where(i < BATCH * per_batch, 1, 2))


def _ptile_mod_row(i):
    return _tile_mod_row(i, P_LAT_TILES_PER_BATCH)


ADA_TN = 1536


def _ada_kernel(c_ref, w_ref, b_ref, o_ref):
    c = c_ref[...]
    s = c * jax.nn.sigmoid(c)
    o_ref[0] = jnp.dot(s, w_ref[0], preferred_element_type=F32,
                       precision=lax.Precision.HIGHEST) + b_ref[0]


def _ada_mod(cvec, w_ada, b_ada):
    n = 6 * D_MODEL
    return pl.pallas_call(
        _ada_kernel,
        out_shape=jax.ShapeDtypeStruct((DEPTH, 8, n), F32),
        grid=(DEPTH, n // ADA_TN),
        in_specs=[
            pl.BlockSpec((8, D_MODEL), lambda l, j: (0, 0)),
            pl.BlockSpec((1, D_MODEL, ADA_TN), lambda l, j: (l, 0, j)),
            pl.BlockSpec((1, 1, ADA_TN), lambda l, j: (l, 0, j)),
        ],
        out_specs=pl.BlockSpec((1, 8, ADA_TN), lambda l, j: (l, 0, j)),
        compiler_params=pltpu.CompilerParams(
            dimension_semantics=("arbitrary", "arbitrary"),
            vmem_limit_bytes=_vmem_limit(2 * D_MODEL * ADA_TN * 4)),
        name="ada_mod",
    )(cvec, w_ada, b_ada.reshape(DEPTH, 1, n))


def _rope_swap(t):
    lane = lax.broadcasted_iota(jnp.int32, t.shape, 1)
    first_half = (lane % 32) < 16
    return jnp.where(first_half, pltpu.roll(t, LANES - 16, 1), pltpu.roll(t, 16, 1))


def _pick_rows(lat_ref, ctx_ref, split_ctx):
    if not split_ctx:
        return lat_ref[...]
    return jnp.where(pl.program_id(0) >= P_LAT_TILES, ctx_ref[...], lat_ref[...])


def _lat_ctx_specs(width, split_ctx):
    if split_ctx:
        return [pl.BlockSpec((TP, width), lambda i, *_: (jnp.minimum(i, P_LAT_TILES - 1), 0)),
                pl.BlockSpec((TP, width), lambda i, *_: (jnp.maximum(i - P_LAT_TILES, 0), 0))]
    return [pl.BlockSpec((TP, width), lambda i, *_: (i, 0)),
            pl.BlockSpec((TP, width), lambda i, *_: (0, 0))]


def _inproj_kernel(xl_ref, xc_ref, mod_ref, g1_ref, w_ref, cs_ref, sn_ref, qg_ref, kg_ref,
                   aq_ref, ak_ref, av_ref, nq_ref, nk_ref, nv_ref,
                   rq_ref, rk_ref, rv_ref, rg_ref, *, split_ctx):
    x = _pick_rows(xl_ref, xc_ref, split_ctx)
    mod = mod_ref[0]
    sh1 = mod[:, 0:D_MODEL]
    sc1 = mod[:, D_MODEL:2 * D_MODEL]
    ms = jnp.mean(x * x, axis=-1, keepdims=True)
    h = x * lax.rsqrt(ms + EPS) * g1_ref[...]
    h = (h * (1.0 + sc1) + sh1).astype(BF16)
    cs = cs_ref[...]
    sn = sn_ref[...]

    sections = {}

    def proj(name, hidx):
        if name not in sections:
            off, width = _SEC_OFF[name]
            sections[name] = jnp.dot(h, w_ref[:, off:off + width], preferred_element_type=F32)
        return sections[name][:, hidx * HB:(hidx + 1) * HB]

    def normed_rope(z, g):
        ss = jnp.sum(z * z, axis=-1, keepdims=True)
        zn = z * lax.rsqrt(ss * (1.0 / HEAD_DIM) + EPS) * g
        return zn * cs + _rope_swap(zn) * sn

    scale = HEAD_DIM ** -0.5
    for hh in range(N_ATT_HEADS):
        z = normed_rope(proj("aq", hh), qg_ref[...]) * (scale * LOG2E)
        aq_ref[hh * HB:(hh + 1) * HB, :] = z.T.astype(BF16)
    for hh in range(N_ATT_KV):
        z = normed_rope(proj("ak", hh), kg_ref[...])
        ak_ref[:, hh * HB:(hh + 1) * HB] = z.astype(BF16)
        zv = proj("av", hh)
        lane = lax.broadcasted_iota(jnp.int32, zv.shape, 1)
        av_ref[hh * HB:(hh + 1) * HB, :] = jnp.where(lane == HEAD_DIM, 1.0, zv).T.astype(BF16)
    def whole(name):
        off, width = _SEC_OFF[name]
        return jnp.dot(h, w_ref[:, off:off + width], preferred_element_type=F32)

    nq_ref[...] = (whole("nq") * scale).astype(BF16)
    nk_ref[...] = whole("nk").astype(BF16)
    nv_ref[...] = whole("nv").astype(BF16)
    rq_ref[...] = whole("rq").astype(BF16)
    rk_ref[...] = (whole("rk") * scale).astype(BF16)
    rv_ref[...] = whole("rv").astype(BF16)
    rg_ref[...] = whole("rg").astype(BF16)


def _inproj(x_lat, x_ctx, split_ctx, mod3, g1, w_pad, cs_tab, sn_tab, qg, kg):
    names = [s[0] for s in _SECTIONS]
    widths = [_SEC_OFF[n][1] for n in names]

    def tab_map(i):
        return (jnp.where(i < P_LAT_TILES, i % P_LAT_TILES_PER_BATCH, P_LAT_TILES_PER_BATCH), 0)

    est = (D_MODEL * NC_PAD * 2 + 4 * TP * D_MODEL * 4 + 2 * TP * NC_PAD * 2
           + 6 * TP * D_MODEL * 4)
    return pl.pallas_call(
        functools.partial(_inproj_kernel, split_ctx=split_ctx),
        out_shape=[jax.ShapeDtypeStruct((w, T_ALL) if n in _FEATURE_MAJOR else (T_ALL, w), BF16)
                   for n, w in zip(names, widths)],
        grid=(P_ALL_TILES,),
        in_specs=_lat_ctx_specs(D_MODEL, split_ctx) + [
            pl.BlockSpec((1, 1, 6 * D_MODEL), lambda i: (_ptile_mod_row(i), 0, 0)),
            pl.BlockSpec((1, D_MODEL), lambda i: (0, 0)),
            pl.BlockSpec((D_MODEL, NC_PAD), lambda i: (0, 0), pipeline_mode=pl.Buffered(1)),
            pl.BlockSpec((TP, HB), tab_map),
            pl.BlockSpec((TP, HB), tab_map),
            pl.BlockSpec((1, HB), lambda i: (0, 0)),
            pl.BlockSpec((1, HB), lambda i: (0, 0)),
        ],
        out_specs=[pl.BlockSpec((w, TP), lambda i: (0, i)) if n in _FEATURE_MAJOR
                   else pl.BlockSpec((TP, w), lambda i: (i, 0)) for n, w in zip(names, widths)],
        compiler_params=pltpu.CompilerParams(
            dimension_semantics=("arbitrary",), vmem_limit_bytes=_vmem_limit(est)),
        name="norm_inproj",
    )(x_lat, x_ctx, mod3, g1, w_pad, cs_tab, sn_tab, qg, kg)


ATT_TK = 8192
ATT_NK = SEQ // ATT_TK
ATT_CK = 256
ATT_VROWS = HEAD_DIM + 16


def _attn_kernel(qt_ref, k_ref, vt_ref, kc_ref, vct_ref, g_ref, o_ref, m_sc, acc_sc):
    i = pl.program_id(2)
    j = pl.program_id(3)
    is_ctx_q = i >= LAT_TILES_PER_BATCH

    @pl.when(j == 0)
    def _():
        m_sc[...] = jnp.full(m_sc.shape, -jnp.inf, F32)
        acc_sc[...] = jnp.zeros(acc_sc.shape, F32)

    def scores(kr, c, ck):
        k = kr[c * ck:(c + 1) * ck, :]
        return [jnp.dot(k, qt_ref[hh * HB:(hh + 1) * HB, :], preferred_element_type=F32)
                for hh in range(ATT_GRP)]

    def step(kr, vtr, nkeys):
        ck = min(ATT_CK, nkeys)
        nchunk = nkeys // ck
        ss = scores(kr, 0, ck)
        for c in range(nchunk):
            cur = ss
            if c + 1 < nchunk:
                ss = scores(kr, c + 1, ck)
            vt = vtr[0:ATT_VROWS, c * ck:(c + 1) * ck]
            ps, alphas = [], []
            for hh in range(ATT_GRP):
                m_prev = m_sc[hh]
                m_new = jnp.maximum(m_prev, jnp.max(cur[hh], axis=0, keepdims=True))
                alpha = jnp.exp2(m_prev - m_new)
                p = jnp.exp2(cur[hh] - m_new)
                m_sc[hh] = m_new
                ps.append(p.astype(BF16))
                alphas.append(alpha)
            for hh in range(ATT_GRP):
                acc_sc[hh] = alphas[hh] * acc_sc[hh] + jnp.dot(vt, ps[hh],
                                                               preferred_element_type=F32)

    @pl.when(jnp.logical_not(is_ctx_q))
    def _():
        step(k_ref, vt_ref, ATT_TK)

    @pl.when(j == ATT_NK - 1)
    def _():
        step(kc_ref, vct_ref, CTX_LEN)
        for hh in range(ATT_GRP):
            acc = acc_sc[hh]
            o = acc[0:HEAD_DIM] / acc[HEAD_DIM:HEAD_DIM + 1]
            ms = jnp.sum(o * o, axis=0, keepdims=True) * (1.0 / HEAD_DIM)
            g = g_ref[hh * HB:hh * HB + HEAD_DIM, :]
            y = o * lax.rsqrt(ms + EPS) * jnp.concatenate([g] * (TM // LANES), axis=1)
            y = jnp.concatenate([y, jnp.zeros_like(y)], axis=0)
            o_ref[:, hh * HB:(hh + 1) * HB] = y.T.astype(BF16)


def _gqa(aqt, ak, avt, ga_cols, want_ctx):
    nq = LAT_TILES_PER_BATCH + (1 if want_ctx else 0)

    def q_tile(b, i):
        return jnp.where(i < LAT_TILES_PER_BATCH, b * LAT_TILES_PER_BATCH + i, LAT_TILES + b)

    est = (2 * ATT_GRP * HB * TM * 2 + 4 * ATT_TK * HB * 2 + 4 * TM * HB * 2
           + ATT_GRP * TM * HB * 4 * 3 + 8 * TM * ATT_TK * 4)
    return pl.pallas_call(
        _attn_kernel,
        out_shape=jax.ShapeDtypeStruct((T_ALL if want_ctx else T_LAT, N_ATT_HEADS * HB), BF16),
        grid=(BATCH, N_ATT_KV, nq, ATT_NK),
        in_specs=[
            pl.BlockSpec((ATT_GRP * HB, TM), lambda b, c, i, j: (c, q_tile(b, i))),
            pl.BlockSpec((ATT_TK, HB), lambda b, c, i, j: (b * ATT_NK + j, c)),
            pl.BlockSpec((HB, ATT_TK), lambda b, c, i, j: (c, b * ATT_NK + j)),
            pl.BlockSpec((CTX_LEN, HB), lambda b, c, i, j: (LAT_TILES + b, c)),
            pl.BlockSpec((HB, CTX_LEN), lambda b, c, i, j: (c, LAT_TILES + b)),
            pl.BlockSpec((ATT_GRP * HB, LANES), lambda b, c, i, j: (c, 0)),
        ],
        out_specs=pl.BlockSpec((TM, ATT_GRP * HB), lambda b, c, i, j: (q_tile(b, i), c)),
        scratch_shapes=[
            pltpu.VMEM((ATT_GRP, 1, TM), F32),
            pltpu.VMEM((ATT_GRP, ATT_VROWS, TM), F32),
        ],
        compiler_params=pltpu.CompilerParams(
            dimension_semantics=("arbitrary",) * 4, vmem_limit_bytes=_vmem_limit(est)),
        name="gqa_attn",
    )(aqt, ak, avt, ak, avt, ga_cols)


NA_BAND = NA_WIN_ROWS * GRID_W
NA_CLASSES = 8
NA_ROWS_PER_ITER = 16
_NA_CLASS_ROWS = (0, 1, 2, 3, GRID_ROWS // 2, GRID_ROWS - 3, GRID_ROWS - 2, GRID_ROWS - 1)


def _na_bias_table(rpb):
    wr, wc = NA_WIN_ROWS, NA_WIN_COLS
    r = np.asarray(_NA_CLASS_ROWS)
    ridx = np.clip(r - wr // 2, 0, GRID_ROWS - wr)[:, None] + np.arange(wr)[None, :]
    dr = ridx - r[:, None] + (wr - 1)
    col = np.arange(GRID_W)
    cstart = np.clip(col - wc // 2, 0, GRID_W - wc)
    col_ok = (col[None, :] >= cstart[:, None]) & (col[None, :] < cstart[:, None] + wc)
    dc = np.clip(col[None, :] - col[:, None] + (wc - 1), 0, 2 * wc - 2)
    pick_r = (dr[:, :, None] == np.arange(2 * wr - 1)[None, None, :]).astype(np.float32)
    pick_c = (dc[:, :, None] == np.arange(2 * wc - 1)[None, None, :]).astype(np.float32)
    hp = lax.Precision.HIGHEST
    by_col = jnp.einsum("hrc,qkc->hrqk", rpb.astype(F32), jnp.asarray(pick_c), precision=hp)
    bias = jnp.einsum("hrqk,cwr->hcqwk", by_col, jnp.asarray(pick_r), precision=hp)
    bias = bias.reshape(N_NA_HEADS, NA_CLASSES, GRID_W, NA_BAND)
    mask = np.tile(col_ok, (1, wr))
    return jnp.where(mask[None, None], bias, NEG_INF)


def _head_rms_gain(o, g):
    ms = jnp.sum(o * o, axis=-1, keepdims=True) * (1.0 / HEAD_DIM)
    return o * lax.rsqrt(ms + EPS) * g


def _pair_store(ref, rows, o, g, odd, post=None):
    lane = lax.broadcasted_iota(jnp.int32, o.shape, 1)
    mine = (lane >= HEAD_DIM) if odd else (lane < HEAD_DIM)
    y = _head_rms_gain(jnp.where(mine, o, 0.0), g)
    if post is not None:
        y = y * post
    if odd:
        y = y + ref[rows, :].astype(F32)
    ref[rows, :] = y.astype(BF16)


def _for_head_parity(body):
    is_odd = pl.program_id(1) % 2 == 1
    pl.when(jnp.logical_not(is_odd))(functools.partial(body, False))
    pl.when(is_odd)(functools.partial(body, True))


def _na_kernel(q_ref, k_ref, v_ref, kc_ref, vc_ref, qc_ref, bias_ref, g_ref, o_ref, oc_ref,
               *, want_ctx):
    _for_head_parity(functools.partial(
        _na_body, q_ref, k_ref, v_ref, kc_ref, vc_ref, qc_ref, bias_ref, g_ref, o_ref, oc_ref,
        want_ctx))


def _na_body(q_ref, k_ref, v_ref, kc_ref, vc_ref, qc_ref, bias_ref, g_ref, o_ref, oc_ref,
             want_ctx, odd):
    kc = kc_ref[...]
    vc = vc_ref[...]
    g = g_ref[0]
    half = NA_WIN_ROWS // 2
    last = GRID_ROWS - NA_WIN_ROWS

    def rows(it, carry):
        r0 = it * NA_ROWS_PER_ITER
        qrows, bands, scores = [], [], []
        for d in range(NA_ROWS_PER_ITER):
            r = r0 + d
            start = jnp.clip(r - half, 0, last)
            cls = jnp.where(r < half, r, jnp.where(r > last + half, r - last, half))
            qrow = pl.ds(pl.multiple_of(r * GRID_W, GRID_W), GRID_W)
            band = pl.ds(pl.multiple_of(start * GRID_W, GRID_W), NA_BAND)
            q = q_ref[qrow, :]
            s = lax.dot_general(q, k_ref[band, :], (((1,), (1,)), ((), ())),
                                preferred_element_type=F32)
            sc = lax.dot_general(q, kc, (((1,), (1,)), ((), ())), preferred_element_type=F32)
            qrows.append(qrow)
            bands.append(band)
            scores.append((s, sc, cls))
        probs = []
        for s, sc, cls in scores:
            bt = bias_ref[0, cls]
            s = jnp.where(bt > 0.5 * NEG_INF, s + bt, NEG_INF)
            m = jnp.maximum(jnp.max(s, axis=-1, keepdims=True),
                            jnp.max(sc, axis=-1, keepdims=True))
            p = jnp.exp(s - m)
            pc = jnp.exp(sc - m)
            l = jnp.sum(p, axis=-1, keepdims=True) + jnp.sum(pc, axis=-1, keepdims=True)
            probs.append((p.astype(BF16), pc.astype(BF16), l))
        for qrow, band, (p, pc, l) in zip(qrows, bands, probs):
            o = (jnp.dot(p, v_ref[band, :], preferred_element_type=F32)
                 + jnp.dot(pc, vc, preferred_element_type=F32)) / l
            _pair_store(o_ref, qrow, o, g, odd)
        return carry

    lax.fori_loop(0, GRID_ROWS // NA_ROWS_PER_ITER, rows, 0)

    if want_ctx:
        sc = lax.dot_general(qc_ref[...], kc, (((1,), (1,)), ((), ())),
                             preferred_element_type=F32)
        m = jnp.max(sc, axis=-1, keepdims=True)
        pc = jnp.exp(sc - m)
        l = jnp.sum(pc, axis=-1, keepdims=True)
        o = jnp.dot(pc.astype(BF16), vc, preferred_element_type=F32) / l
        _pair_store(oc_ref, slice(None), o, g, odd)
    elif not odd:
        oc_ref[...] = jnp.zeros(oc_ref.shape, oc_ref.dtype)


def _pair_specs(rows, first_block=0):
    return (pl.BlockSpec((rows, HB), lambda b, h: (first_block + b, h)),
            pl.BlockSpec((rows, HB), lambda b, h: (first_block + b, h // 2)))


def _neigh(nq, nk, nv, bias_tab, gn, want_ctx):
    lat, lat_pair = _pair_specs(SEQ)
    ctx, ctx_pair = _pair_specs(CTX_LEN, T_LAT // CTX_LEN)
    est = 2 * (4 * SEQ * HB * 2 + 4 * CTX_LEN * HB * 2 + NA_CLASSES * GRID_W * NA_BAND * 4)
    return pl.pallas_call(
        functools.partial(_na_kernel, want_ctx=want_ctx),
        out_shape=[jax.ShapeDtypeStruct((T_LAT, _YN_W), BF16),
                   jax.ShapeDtypeStruct((T_CTX, _YN_W), BF16)],
        grid=(BATCH, N_NA_HEADS),
        in_specs=[lat, lat_pair, lat_pair, ctx_pair, ctx_pair, ctx,
                  pl.BlockSpec((1, NA_CLASSES, GRID_W, NA_BAND), lambda b, h: (h, 0, 0, 0)),
                  pl.BlockSpec((1, 1, HB), lambda b, h: (h, 0, 0))],
        out_specs=[_pair_specs(SEQ)[1], _pair_specs(CTX_LEN)[1]],
        compiler_params=pltpu.CompilerParams(
            dimension_semantics=("arbitrary", "arbitrary"), vmem_limit_bytes=_vmem_limit(est)),
        name="neigh_attn",
    )(nq, nk, nv, nk, nv, nq, bias_tab, gn)


RET_NCHUNK = SEQ // RET_CHUNK
RET_NCHUNK_CTX = CTX_LEN // RET_CHUNK
RET_CHUNKS_PER_ITER = 16


def _ret_tables(log_g2):
    lf = log_g2[0][:, None, None]
    lb = log_g2[1][:, None, None]
    pos = jnp.arange(RET_CHUNK, dtype=F32)
    i = pos[None, :, None]
    j = pos[None, None, :]
    diff = i - j
    dm = jnp.where(diff > 0, jnp.exp(lf * jnp.maximum(diff, 0.0)),
                   jnp.where(diff < 0, jnp.exp(lb * jnp.maximum(-diff, 0.0)), 2.0)) * 0.5
    fwd_lane = (jnp.arange(LANES) < HEAD_DIM)[None, None, :]
    xi = jnp.where(fwd_lane, jnp.exp(lf * (i + 1.0)), jnp.exp(lb * (RET_CHUNK - i)))
    zt = jnp.where(fwd_lane, jnp.exp(lf * (RET_CHUNK - 1.0 - i)), jnp.exp(lb * i))
    fwd_row = (jnp.arange(LANES) < HEAD_DIM)[None, :, None]
    dec = jnp.where(fwd_row, jnp.exp(lf * RET_CHUNK), jnp.exp(lb * RET_CHUNK))
    dec = jnp.broadcast_to(dec, (N_RET_HEADS, LANES, LANES))
    return dm.astype(F32), xi.astype(F32), zt.astype(F32), dec.astype(F32)


def _ret_kernel(*refs, want_ctx):
    _for_head_parity(functools.partial(_ret_body, *refs, want_ctx))


def _ret_body(q_ref, k_ref, v_ref, gt_ref, qc_ref, kc_ref, vc_ref, gtc_ref,
              dm_ref, xi_ref, zt_ref, dec_ref, g_ref, o_ref, oc_ref,
              u_sc, s_sc, uc_sc, sc_sc, want_ctx, odd):
    dm = dm_ref[0]
    xi = xi_ref[0]
    zt = zt_ref[0]
    dec = dec_ref[0]
    dec_f = dec[0:HEAD_DIM]
    dec_b = dec[HEAD_DIM:LANES]
    g = g_ref[0]
    C = RET_CHUNK

    def chunk_rows(n):
        return pl.ds(pl.multiple_of(n * C, C), C)

    def chunk_state_update(kr, vr, usc, ns):
        kzs = [(kr[chunk_rows(n), :].astype(F32) * zt).T.astype(BF16) for n in ns]
        for n, kz in zip(ns, kzs):
            usc[n] = jnp.dot(kz, vr[chunk_rows(n), :], preferred_element_type=F32)

    def chunk_out(qr, kr, vr, gtr, ssc, outr, ns):
        qds = [qr[chunk_rows(n), :] for n in ns]
        s2s = [lax.dot_general(qd, kr[chunk_rows(n), :], (((1,), (1,)), ((), ())),
                               preferred_element_type=F32) for n, qd in zip(ns, qds)]
        outs = []
        for n, qd, s2 in zip(ns, qds, s2s):
            inner = jnp.dot((s2 * dm).astype(BF16), vr[chunk_rows(n), :],
                            preferred_element_type=F32)
            qx = (qd.astype(F32) * xi).astype(BF16)
            outs.append(inner + jnp.dot(qx, ssc[n].astype(BF16), preferred_element_type=F32))
        for n, o in zip(ns, outs):
            gate = gtr[chunk_rows(n), :].astype(F32)
            _pair_store(outr, chunk_rows(n), o, g, odd, post=gate * jax.nn.sigmoid(gate))

    def scan_states(usc, ssc, nchunk, init_f, init_b):
        def fwd(n, sf):
            ssc[n, 0:HEAD_DIM, :] = sf
            return dec_f * sf + usc[n, 0:HEAD_DIM, :]

        def bwd(t, sb):
            n = nchunk - 1 - t
            ssc[n, HEAD_DIM:LANES, :] = sb
            return dec_b * sb + usc[n, HEAD_DIM:LANES, :]

        return (lax.fori_loop(0, nchunk, fwd, init_f), lax.fori_loop(0, nchunk, bwd, init_b))

    zero = jnp.zeros((HEAD_DIM, LANES), F32)
    ctx_chunks = list(range(RET_NCHUNK_CTX))
    chunk_state_update(kc_ref, vc_ref, uc_sc, ctx_chunks)
    ctx_f, ctx_b = scan_states(uc_sc, sc_sc, RET_NCHUNK_CTX, zero, zero)
    if want_ctx:
        chunk_out(qc_ref, kc_ref, vc_ref, gtc_ref, sc_sc, oc_ref, ctx_chunks)
    elif not odd:
        oc_ref[...] = jnp.zeros(oc_ref.shape, oc_ref.dtype)

    def upd(it, carry):
        chunk_state_update(k_ref, v_ref, u_sc,
                           [it * RET_CHUNKS_PER_ITER + d for d in range(RET_CHUNKS_PER_ITER)])
        return carry

    lax.fori_loop(0, RET_NCHUNK // RET_CHUNKS_PER_ITER, upd, 0)
    scan_states(u_sc, s_sc, RET_NCHUNK, ctx_f, ctx_b)

    def out(it, carry):
        chunk_out(q_ref, k_ref, v_ref, gt_ref, s_sc, o_ref,
                  [it * RET_CHUNKS_PER_ITER + d for d in range(RET_CHUNKS_PER_ITER)])
        return carry

    lax.fori_loop(0, RET_NCHUNK // RET_CHUNKS_PER_ITER, out, 0)


def _retention(rq, rk, rv, rg, tables, gr, want_ctx):
    lat, lat_pair = _pair_specs(SEQ)
    ctx, ctx_pair = _pair_specs(CTX_LEN, T_LAT // CTX_LEN)
    tab = pl.BlockSpec((1, LANES, LANES), lambda b, h: (h, 0, 0))
    est = (2 * 5 * SEQ * HB * 2 + 2 * RET_NCHUNK * LANES * LANES * 4 + 8 * LANES * LANES * 4)
    return pl.pallas_call(
        functools.partial(_ret_kernel, want_ctx=want_ctx),
        out_shape=[jax.ShapeDtypeStruct((T_LAT, _YR_W), BF16),
                   jax.ShapeDtypeStruct((T_CTX, _YR_W), BF16)],
        grid=(BATCH, N_RET_HEADS),
        in_specs=[lat, lat, lat_pair, lat_pair, ctx, ctx, ctx_pair, ctx_pair, tab, tab, tab, tab,
                  pl.BlockSpec((1, 1, HB), lambda b, h: (h, 0, 0))],
        out_specs=[_pair_specs(SEQ)[1], _pair_specs(CTX_LEN)[1]],
        scratch_shapes=[
            pltpu.VMEM((RET_NCHUNK, LANES, LANES), F32),
            pltpu.VMEM((RET_NCHUNK, LANES, LANES), F32),
            pltpu.VMEM((RET_NCHUNK_CTX, LANES, LANES), F32),
            pltpu.VMEM((RET_NCHUNK_CTX, LANES, LANES), F32),
        ],
        compiler_params=pltpu.CompilerParams(
            dimension_semantics=("arbitrary", "arbitrary"), vmem_limit_bytes=_vmem_limit(est)),
        name="retention",
    )(rq, rk, rv, rg, rq, rk, rv, rg, *tables, gr)


MOE_SLOT = TM
META_ROWS = 8
_META_GATE0 = 2


def _route_t(logt):
    row = lax.broadcasted_iota(jnp.int32, logt.shape, 0).astype(F32)
    p = jnp.exp(logt - jnp.max(logt, axis=0, keepdims=True))
    best = None
    for grp in range(N_EXPERT_GROUPS):
        lo = float(grp * EXPERTS_PER_GROUP)
        ing = (row >= lo) & (row < lo + EXPERTS_PER_GROUP)
        pg = jnp.where(ing, p, -1.0)
        m1 = jnp.max(pg, axis=0, keepdims=True)
        i1 = jnp.min(jnp.where(pg == m1, row, float(N_EXPERTS)), axis=0, keepdims=True)
        pg2 = jnp.where(row == i1, -1.0, pg)
        m2 = jnp.max(pg2, axis=0, keepdims=True)
        i2 = jnp.min(jnp.where(pg2 == m2, row, float(N_EXPERTS)), axis=0, keepdims=True)
        cand = (m1 + m2, m1, m2, i1, i2, jnp.zeros_like(m1) + lo)
        if best is None:
            best = cand
        else:
            better = cand[0] > best[0]
            best = tuple(jnp.where(better, c, b) for c, b in zip(cand, best))
    _, m1, m2, i1, i2, base = best
    w = m1 + m2
    gates = [jnp.where(i1 == base + e, m1 / w, jnp.where(i2 == base + e, m2 / w, 0.0))
             for e in range(EXPERTS_PER_GROUP)]
    return base * (1.0 / EXPERTS_PER_GROUP), gates


def _merge_kernel(ya_ref, ynl_ref, ync_ref, yrl_ref, yrc_ref, xl_ref, xc_ref, mod_ref, g2_ref,
                  wo_ref, wr_ref, brt_ref,
                  xn_ref, h2_ref, metat_ref, metac_ref, ctab_ref, tot_ref, carry_sc, *, split_ctx):
    @pl.when(pl.program_id(0) == 0)
    def _():
        carry_sc[...] = jnp.zeros(carry_sc.shape, F32)

    mod = mod_ref[0]
    gt1 = mod[:, 2 * D_MODEL:3 * D_MODEL]
    sh2 = mod[:, 3 * D_MODEL:4 * D_MODEL]
    sc2 = mod[:, 4 * D_MODEL:5 * D_MODEL]
    yn = _pick_rows(ynl_ref, ync_ref, split_ctx)
    yr = _pick_rows(yrl_ref, yrc_ref, split_ctx)
    m = (jnp.dot(ya_ref[...], wo_ref[0:_YA_W, :], preferred_element_type=F32)
         + jnp.dot(yn, wo_ref[_YA_W:_YA_W + _YN_W, :], preferred_element_type=F32)
         + jnp.dot(yr, wo_ref[_YA_W + _YN_W:MIX_PAD, :], preferred_element_type=F32))
    x = _pick_rows(xl_ref, xc_ref, split_ctx) + gt1 * m
    xn_ref[...] = x
    ms = jnp.mean(x * x, axis=-1, keepdims=True)
    h2 = x * lax.rsqrt(ms + EPS) * g2_ref[...] * (1.0 + sc2) + sh2
    h2_ref[...] = h2.astype(BF16)

    ntile = TP // LANES
    h_hi = h2.astype(BF16)
    h_lo = (h2 - h_hi.astype(F32)).astype(BF16)
    wr = wr_ref[...]
    w_hi = wr.astype(BF16)
    w_lo = (wr - w_hi.astype(F32)).astype(BF16)
    logits = (jnp.dot(h_hi, w_hi, preferred_element_type=F32)
              + jnp.dot(h_lo, w_hi, preferred_element_type=F32)
              + jnp.dot(h_hi, w_lo, preferred_element_type=F32))
    logt = logits.T[0:N_EXPERTS, :]
    gsel, gates = _route_t(logt + jnp.concatenate([brt_ref[...]] * ntile, axis=1))

    grow = lax.broadcasted_iota(jnp.int32, (META_ROWS, TP), 0).astype(F32)
    onehot = jnp.where(grow == gsel, 1.0, 0.0)
    earlier = (lax.broadcasted_iota(jnp.int32, (TP, TP), 0)
               < lax.broadcasted_iota(jnp.int32, (TP, TP), 1))
    excl = jnp.dot(onehot.astype(BF16), jnp.where(earlier, 1.0, 0.0).astype(BF16),
                   preferred_element_type=F32)
    carry = carry_sc[...]
    rank = jnp.sum(onehot * (jnp.concatenate([carry] * ntile, axis=1) + excl),
                   axis=0, keepdims=True)
    for sub in range(TP // TM):
        ctab_ref[sub] = carry
        carry = carry + jnp.sum(onehot[:, sub * TM:(sub + 1) * TM], axis=1, keepdims=True)
    carry_sc[...] = carry
    tot_ref[...] = carry

    metat = jnp.concatenate([gsel, rank] + gates
                            + [jnp.zeros((META_ROWS - _META_GATE0 - EXPERTS_PER_GROUP, TP), F32)],
                            axis=0)
    metat_ref[...] = metat
    hi = metat.astype(BF16).astype(F32)
    mid = (metat - hi).astype(BF16).astype(F32)
    lo = (metat - hi - mid).astype(BF16).astype(F32)
    metac_ref[...] = jnp.concatenate(
        [metat, hi, mid, lo, jnp.zeros((LANES - 4 * META_ROWS, TP), F32)], axis=0).T


def _merge(ya, yn_lat, yn_ctx, yr_lat, yr_ctx, x_lat, x_ctx, split_ctx, mod3, g2, wo_pad, wr_pad,
           brt, ntiles):
    rows = ntiles * TP
    est = (2 * MIX_PAD * D_MODEL * 2 + 4 * TP * MIX_PAD * 2 + 10 * TP * D_MODEL * 4
           + 4 * TP * TP * 4)
    const = lambda i: (0, 0)
    return pl.pallas_call(
        functools.partial(_merge_kernel, split_ctx=split_ctx),
        out_shape=[jax.ShapeDtypeStruct((rows, D_MODEL), F32),
                   jax.ShapeDtypeStruct((rows, D_MODEL), BF16),
                   jax.ShapeDtypeStruct((META_ROWS, rows), F32),
                   jax.ShapeDtypeStruct((rows, LANES), F32),
                   jax.ShapeDtypeStruct((ntiles * (TP // TM), META_ROWS, LANES), F32),
                   jax.ShapeDtypeStruct((META_ROWS, LANES), F32)],
        grid=(ntiles,),
        in_specs=([pl.BlockSpec((TP, _YA_W), lambda i: (i, 0))]
                  + _lat_ctx_specs(_YN_W, split_ctx) + _lat_ctx_specs(_YR_W, split_ctx)
                  + _lat_ctx_specs(D_MODEL, split_ctx) + [
            pl.BlockSpec((1, 1, 6 * D_MODEL), lambda i: (_ptile_mod_row(i), 0, 0)),
            pl.BlockSpec((1, D_MODEL), const),
            pl.BlockSpec((MIX_PAD, D_MODEL), const),
            pl.BlockSpec((D_MODEL, LANES), const),
            pl.BlockSpec((N_EXPERTS, LANES), const),
        ]),
        out_specs=[pl.BlockSpec((TP, D_MODEL), lambda i: (i, 0)),
                   pl.BlockSpec((TP, D_MODEL), lambda i: (i, 0)),
                   pl.BlockSpec((META_ROWS, TP), lambda i: (0, i)),
                   pl.BlockSpec((TP, LANES), lambda i: (i, 0)),
                   pl.BlockSpec((TP // TM, META_ROWS, LANES), lambda i: (i, 0, 0)),
                   pl.BlockSpec((META_ROWS, LANES), const)],
        scratch_shapes=[pltpu.VMEM((META_ROWS, LANES), F32)],
        compiler_params=pltpu.CompilerParams(
            dimension_semantics=("arbitrary",), vmem_limit_bytes=_vmem_limit(est)),
        name="merge_outproj_router",
    )(ya, yn_lat, yn_ctx, yr_lat, yr_ctx, x_lat, x_ctx, mod3, g2, wo_pad, wr_pad, brt)


def _moe_plan(ctab, tot, ntiles):
    grp = N_EXPERT_GROUPS
    i32 = jnp.int32
    a = ctab[:, :grp, 0].astype(i32).T
    totg = tot[:grp, 0].astype(i32)
    b = jnp.concatenate([a[:, 1:], totg[:, None]], axis=1)
    nslot = (totg + MOE_SLOT - 1) // MOE_SLOT
    slot_end = jnp.cumsum(nslot)
    slot_base = slot_end - nslot
    total_slots = slot_end[-1]
    nchunk = ntiles // (MOE_CHUNK // TM)
    ac = a[:, ::MOE_CHUNK // TM]
    bc = jnp.concatenate([ac[:, 1:], totg[:, None]], axis=1)
    first_j = ac // MOE_SLOT
    last_j = (jnp.maximum(bc, 1) - 1) // MOE_SLOT
    npairs = jnp.where(bc > ac, last_j - first_j + 1, 0).reshape(-1)
    cum = jnp.cumsum(npairs)
    start = cum - npairs
    total_pairs = cum[-1]
    n_pairs_max = grp * nchunk + ntiles + grp
    pidx = jnp.arange(n_pairs_max, dtype=i32)
    p = jnp.minimum(pidx, total_pairs - 1)
    gc = jnp.sum((cum[None, :] <= p[:, None]).astype(i32), axis=1)
    slot = slot_base[gc // nchunk] + first_j.reshape(-1)[gc] + (p - start[gc])
    valid = pidx < total_pairs
    prev_slot = jnp.concatenate([jnp.full((1,), -1, i32), slot[:-1]])
    next_slot = jnp.concatenate([slot[1:], jnp.full((1,), -1, i32)])
    is_first = valid & (slot != prev_slot)
    is_last = valid & ((slot != next_slot) | (pidx == total_pairs - 1))
    n_slots_max = ntiles + grp
    fill = jnp.logical_not(valid) & (total_slots < n_slots_max)
    slot = jnp.where(fill, jnp.minimum(total_slots + pidx - total_pairs, n_slots_max - 1), slot)
    flags = (is_first.astype(i32) + 2 * is_last.astype(i32) + 4 * valid.astype(i32)
             + 8 * fill.astype(i32))
    sidx = jnp.arange(n_slots_max, dtype=i32)
    sgrp = jnp.minimum(jnp.sum((slot_end[None, :] <= sidx[:, None]).astype(i32), axis=1), grp - 1)
    sr0 = (sidx - slot_base[sgrp]) * MOE_SLOT
    row0 = slot_base[:, None] * MOE_SLOT + a
    row1 = slot_base[:, None] * MOE_SLOT + jnp.maximum(b, a + 1) - 1
    wb = jnp.stack([row0 // MOE_SLOT, row1 // MOE_SLOT], axis=1)
    wb = jnp.clip(wb, 0, total_slots - 1).reshape(-1).astype(i32)
    return dict(pslot=slot.astype(i32), pchunk=(gc % nchunk).astype(i32), pflag=flags,
                sgrp=sgrp, sr0=sr0.astype(i32), wb=wb, sbase=slot_base.astype(i32),
                n_pairs=n_pairs_max, n_slots=n_slots_max)


def _moe_kernel(pslot_ref, pchunk_ref, pflag_ref, sgrp_ref, sr0_ref,
                h_ref, metat_ref, metac_ref, w1_ref, w3_ref, w2_ref, o_ref, x_sc, g_sc):
    p = pl.program_id(0)
    flags = pflag_ref[p]
    slot = pslot_ref[p]

    @pl.when((flags & 8) != 0)
    def _():
        o_ref[...] = jnp.zeros(o_ref.shape, o_ref.dtype)

    @pl.when((flags & 1) != 0)
    def _():
        x_sc[...] = jnp.zeros(x_sc.shape, F32)
        g_sc[...] = jnp.zeros(g_sc.shape, F32)

    @pl.when((flags & 4) != 0)
    def _():
        mt = metat_ref[...].astype(jnp.int32)
        want = lax.broadcasted_iota(jnp.int32, (MOE_SLOT, MOE_CHUNK), 0) + sr0_ref[slot]
        sel = (mt[1:2, :] == want) & (mt[0:1, :] == sgrp_ref[slot])
        pm = jnp.where(sel, 1.0, 0.0).astype(BF16)
        x_sc[...] += jnp.dot(pm, h_ref[...], preferred_element_type=F32)
        g_sc[...] += jnp.dot(pm, metac_ref[...].astype(BF16), preferred_element_type=F32)

    @pl.when((flags & 2) != 0)
    def _():
        x = x_sc[...].astype(BF16)
        pieces = g_sc[...]
        gs = (pieces[:, META_ROWS:2 * META_ROWS] + pieces[:, 2 * META_ROWS:3 * META_ROWS]
              + pieces[:, 3 * META_ROWS:4 * META_ROWS])
        acc = jnp.zeros((MOE_SLOT, D_MODEL), F32)
        for e in range(EXPERTS_PER_GROUP):
            a = jnp.dot(x, w1_ref[0, 0, e], preferred_element_type=F32)
            b = jnp.dot(x, w3_ref[0, 0, e], preferred_element_type=F32)
            gate = gs[:, _META_GATE0 + e:_META_GATE0 + e + 1]
            act = (a * jax.nn.sigmoid(a)) * b * gate
            acc = acc + jnp.dot(act.astype(BF16), w2_ref[0, 0, e], preferred_element_type=F32)
        o_ref[...] = acc.astype(BF16)


def _moe_sorted(plan, h2, metat, metac, w1g, w3g, w2g, layer):
    est = (2 * 3 * EXPERTS_PER_GROUP * D_MODEL * D_EXPERT * 2 + 6 * MOE_CHUNK * D_MODEL * 2
           + 2 * MOE_SLOT * D_MODEL * 4 + 6 * MOE_SLOT * D_EXPERT * 4
           + 4 * MOE_CHUNK * MOE_SLOT * 4)
    wmap = lambda p, ps, pc, pf, sg, sr: (layer, sg[ps[p]], 0, 0, 0)
    grid_spec = pltpu.PrefetchScalarGridSpec(
        num_scalar_prefetch=5,
        grid=(plan["n_pairs"],),
        in_specs=[
            pl.BlockSpec((MOE_CHUNK, D_MODEL), lambda p, ps, pc, pf, sg, sr: (pc[p], 0)),
            pl.BlockSpec((META_ROWS, MOE_CHUNK), lambda p, ps, pc, pf, sg, sr: (0, pc[p])),
            pl.BlockSpec((MOE_CHUNK, LANES), lambda p, ps, pc, pf, sg, sr: (pc[p], 0)),
            pl.BlockSpec((1, 1, EXPERTS_PER_GROUP, D_MODEL, D_EXPERT), wmap),
            pl.BlockSpec((1, 1, EXPERTS_PER_GROUP, D_MODEL, D_EXPERT), wmap),
            pl.BlockSpec((1, 1, EXPERTS_PER_GROUP, D_EXPERT, D_MODEL), wmap),
        ],
        out_specs=pl.BlockSpec((MOE_SLOT, D_MODEL), lambda p, ps, pc, pf, sg, sr: (ps[p], 0)),
        scratch_shapes=[pltpu.VMEM((MOE_SLOT, D_MODEL), F32), pltpu.VMEM((MOE_SLOT, LANES), F32)],
    )
    return pl.pallas_call(
        _moe_kernel,
        out_shape=jax.ShapeDtypeStruct((plan["n_slots"] * MOE_SLOT, D_MODEL), BF16),
        grid_spec=grid_spec,
        compiler_params=pltpu.CompilerParams(
            dimension_semantics=("arbitrary",), vmem_limit_bytes=_vmem_limit(est)),
        name="moe_sorted_experts",
    )(plan["pslot"], plan["pchunk"], plan["pflag"], plan["sgrp"], plan["sr0"],
      h2, metat, metac, w1g, w3g, w2g)


_COMBINE_WINDOWS = 2 * N_EXPERT_GROUPS


def _combine_kernel(wb_ref, sb_ref, *refs, ntiles, final):
    y_refs = refs[:_COMBINE_WINDOWS]
    metac_ref, xn_ref, mod_ref, fg_ref, o_ref, m_sc = refs[_COMBINE_WINDOWS:]
    i = pl.program_id(0)
    mc = metac_ref[...]
    grp = mc[:, 0:1].astype(jnp.int32)
    rank = mc[:, 1:2].astype(jnp.int32)
    col = lax.broadcasted_iota(jnp.int32, (TM, MOE_SLOT), 1)

    def window(g, k):
        blk = wb_ref[(2 * g + k) * ntiles + i]
        sel = (grp == g) & (rank + sb_ref[g] * MOE_SLOT == col + blk * MOE_SLOT)
        return jnp.dot(jnp.where(sel, 1.0, 0.0).astype(BF16), y_refs[2 * g + k][...],
                       preferred_element_type=F32)

    m = window(0, 0)
    for g in range(1, N_EXPERT_GROUPS):
        m = m + window(g, 0)
    m_sc[...] = m
    for g in range(N_EXPERT_GROUPS):
        @pl.when(wb_ref[(2 * g + 1) * ntiles + i] != wb_ref[(2 * g) * ntiles + i])
        def _(g=g):
            m_sc[...] += window(g, 1)
    gt2 = mod_ref[0][:, 5 * D_MODEL:6 * D_MODEL]
    x = xn_ref[...] + gt2 * m_sc[...]
    if final:
        ms = jnp.mean(x * x, axis=-1, keepdims=True)
        x = x * lax.rsqrt(ms + EPS) * fg_ref[...]
    o_ref[...] = x


def _combine(plan, ys, metac, xn, mod3, fg, ntiles, final):
    def ymap(w):
        return lambda i, wb, sb: (wb[w * ntiles + i], 0)

    est = (2 * _COMBINE_WINDOWS * MOE_SLOT * D_MODEL * 2 + 8 * TM * D_MODEL * 4
           + 4 * TM * MOE_SLOT * 4)
    grid_spec = pltpu.PrefetchScalarGridSpec(
        num_scalar_prefetch=2,
        grid=(ntiles,),
        in_specs=[pl.BlockSpec((MOE_SLOT, D_MODEL), ymap(w)) for w in range(_COMBINE_WINDOWS)] + [
            pl.BlockSpec((TM, LANES), lambda i, wb, sb: (i, 0)),
            pl.BlockSpec((TM, D_MODEL), lambda i, wb, sb: (i, 0)),
            pl.BlockSpec((1, 1, 6 * D_MODEL), lambda i, wb, sb: (_tile_mod_row(i), 0, 0)),
            pl.BlockSpec((1, D_MODEL), lambda i, wb, sb: (0, 0)),
        ],
        out_specs=pl.BlockSpec((TM, D_MODEL), lambda i, wb, sb: (i, 0)),
        scratch_shapes=[pltpu.VMEM((TM, D_MODEL), F32)],
    )
    return pl.pallas_call(
        functools.partial(_combine_kernel, ntiles=ntiles, final=final),
        out_shape=jax.ShapeDtypeStruct((ntiles * TM, D_MODEL), F32),
        grid_spec=grid_spec,
        compiler_params=pltpu.CompilerParams(
            dimension_semantics=("arbitrary",), vmem_limit_bytes=_vmem_limit(est)),
        name="moe_combine",
    )(plan["wb"], plan["sbase"], *([ys] * _COMBINE_WINDOWS), metac, xn, mod3, fg)


def _rope_tables():
    t = np.arange(SEQ)
    nf = HEAD_DIM // 4
    inv = (np.float32(ROPE_THETA) ** (-np.arange(nf, dtype=np.float32) / np.float32(nf)))
    inv = inv.astype(np.float32)
    ang_r = (t // GRID_W).astype(np.float32)[:, None] * inv[None, :]
    ang_c = (t % GRID_W).astype(np.float32)[:, None] * inv[None, :]
    cr, sr, cc, sc = np.cos(ang_r), np.sin(ang_r), np.cos(ang_c), np.sin(ang_c)
    zeros = np.zeros((SEQ, HEAD_DIM), np.float32)
    cs = np.concatenate([cr, cr, cc, cc, zeros], axis=-1)
    sn = np.concatenate([-sr, sr, -sc, sc, zeros], axis=-1)
    ident = np.concatenate([np.ones((T_CTX, HEAD_DIM), np.float32),
                            np.zeros((T_CTX, HEAD_DIM), np.float32)], axis=-1)
    cs = np.concatenate([cs, ident], axis=0).astype(np.float32)
    sn = np.concatenate([sn, np.zeros((T_CTX, LANES), np.float32)], axis=0).astype(np.float32)
    return jnp.asarray(cs), jnp.asarray(sn)


def _pad_out_weight(w_out_l):
    w = w_out_l[:ATT_Q].reshape(N_ATT_HEADS, HEAD_DIM, D_MODEL)
    w = jnp.concatenate([w, jnp.zeros_like(w)], axis=1).reshape(_YA_W, D_MODEL)
    return jnp.concatenate([w, w_out_l[ATT_Q:]], axis=0).astype(BF16)


def _pad_in_weight(w_in_l):
    parts = []
    for _, src, heads, mode in _SECTIONS:
        w = w_in_l[:, src:src + heads * HEAD_DIM]
        if mode != "dense":
            w = w.reshape(D_MODEL, heads, HEAD_DIM)
            zero = jnp.zeros_like(w)
            if mode == "par":
                even = (np.arange(heads) % 2 == 0)[None, :, None]
                halves = [jnp.where(even, w, zero), jnp.where(even, zero, w)]
            else:
                halves = [w, w if mode == "dup" else zero]
            w = jnp.concatenate(halves, axis=-1).reshape(D_MODEL, heads * HB)
        parts.append(w)
    return jnp.concatenate(parts, axis=-1).astype(BF16)


def kernel(x, c, ctx, c_ctx, w_ada, b_ada, norm1_g, norm2_g, w_in, q_norm_g, k_norm_g, na_rpb,
           ret_decay, mix_g, w_out, w_router, b_router, w_exp1, w_exp3, w_exp2, final_g):
    cs_tab, sn_tab = _rope_tables()

    cvec = jnp.concatenate([c, c_ctx[None, :], jnp.zeros((8 - BATCH - 1, D_MODEL), F32)], axis=0)
    mod_all = _ada_mod(cvec, w_ada, b_ada)

    wr_pad = jnp.concatenate([w_router, jnp.zeros((D_MODEL, LANES - N_EXPERTS), F32)], axis=1)
    brt = jnp.broadcast_to(b_router[:, None], (N_EXPERTS, LANES))
    zero_lane = jnp.zeros((HEAD_DIM,), F32)

    x_lat = x.reshape(T_LAT, D_MODEL)
    x_ctx = ctx.reshape(T_CTX, D_MODEL)

    gshape = (DEPTH, N_EXPERT_GROUPS, EXPERTS_PER_GROUP)
    w1g = w_exp1.astype(BF16).reshape(gshape + (D_MODEL, D_EXPERT))
    w3g = w_exp3.astype(BF16).reshape(gshape + (D_MODEL, D_EXPERT))
    w2g = w_exp2.astype(BF16).reshape(gshape + (D_EXPERT, D_MODEL))

    for l in range(DEPTH):
        last = l == DEPTH - 1
        want_ctx = not last
        split_ctx = l == 0
        mod3 = mod_all[l].reshape(8, 1, 6 * D_MODEL)
        w_pad = _pad_in_weight(w_in[l])
        qg = jnp.concatenate([q_norm_g[l], zero_lane])[None, :]
        kg = jnp.concatenate([k_norm_g[l], zero_lane])[None, :]
        aq, ak, av, nq, nk, nv, rq, rk, rv, rg = _inproj(
            x_lat, x_ctx, split_ctx, mod3, norm1_g[l][None, :], w_pad, cs_tab, sn_tab, qg, kg)

        ga = jnp.broadcast_to(mix_g[l][:ATT_Q].reshape(N_ATT_HEADS, HEAD_DIM, 1),
                              (N_ATT_HEADS, HEAD_DIM, LANES))
        ga = jnp.concatenate([ga, jnp.zeros_like(ga)], axis=1).reshape(N_ATT_HEADS * HB, LANES)
        gn = _pair_gains(mix_g[l][ATT_Q:ATT_Q + NA_W], N_NA_HEADS)
        gr = _pair_gains(mix_g[l][ATT_Q + NA_W:], N_RET_HEADS)

        ya = _gqa(aq, ak, av, ga, want_ctx)
        yn_lat, yn_ctx = _neigh(nq, nk, nv, _na_bias_table(na_rpb[l]), gn, want_ctx)
        log_g2 = jax.nn.log_sigmoid(ret_decay[l].astype(F32))
        yr_lat, yr_ctx = _retention(rq, rk, rv, rg, _ret_tables(log_g2), gr, want_ctx)

        wo_pad = _pad_out_weight(w_out[l])
        ntiles = LAT_TILES if last else ALL_TILES
        xn, h2, metat, metac, ctab, tot = _merge(
            ya, yn_lat, yn_ctx, yr_lat, yr_ctx, x_lat, x_ctx, split_ctx, mod3,
            norm2_g[l][None, :], wo_pad, wr_pad, brt, ntiles // (TP // TM))
        plan = _moe_plan(ctab, tot, ntiles)
        ys = _moe_sorted(plan, h2, metat, metac, w1g, w3g, w2g, l)
        x_lat = _combine(plan, ys, metac, xn, mod3, final_g[None, :], ntiles, last)
        x_ctx = x_lat

    return x_lat.reshape(BATCH, SEQ, D_MODEL)
```

```python
import functools

import numpy as np
import jax
import jax.numpy as jnp
from jax import lax
from jax.experimental import pallas as pl
from jax.experimental.pallas import tpu as pltpu

D_MODEL = 1024
BATCH = 2
SEQ = 8192
DEPTH = 2
GRID_W = 64
GRID_ROWS = SEQ // GRID_W
CTX_LEN = 256
HEAD_DIM = 64
N_ATT_HEADS = 6
N_ATT_KV = 2
ATT_GRP = N_ATT_HEADS // N_ATT_KV
N_NA_HEADS = 4
N_RET_HEADS = 6
ATT_Q = N_ATT_HEADS * HEAD_DIM
ATT_KV = N_ATT_KV * HEAD_DIM
NA_W = N_NA_HEADS * HEAD_DIM
RET_W = N_RET_HEADS * HEAD_DIM
NA_WIN_ROWS = 8
NA_WIN_COLS = 16
RET_CHUNK = 128
ROPE_THETA = 10000.0
N_EXPERTS = 16
N_EXPERT_GROUPS = 4
EXPERTS_PER_GROUP = N_EXPERTS // N_EXPERT_GROUPS
D_EXPERT = 512
EPS = 1e-6
NEG_INF = -1e30

LANES = 128
VMEM_LIMIT_CAP = 56 * 1024 * 1024

T_LAT = BATCH * SEQ
T_CTX = BATCH * CTX_LEN
T_ALL = T_LAT + T_CTX
TM = 256
LAT_TILES_PER_BATCH = SEQ // TM
LAT_TILES = T_LAT // TM
CTX_TILES = T_CTX // TM
ALL_TILES = LAT_TILES + CTX_TILES
TP = TM
MOE_CHUNK = 2 * TM
P_LAT_TILES_PER_BATCH = SEQ // TP
P_LAT_TILES = T_LAT // TP
P_ALL_TILES = T_ALL // TP
HB = LANES

F32 = jnp.float32
BF16 = jnp.bfloat16

_SECTIONS = (
    ("aq", 0, N_ATT_HEADS, "pad"),
    ("ak", ATT_Q, N_ATT_KV, "pad"),
    ("av", ATT_Q + ATT_KV, N_ATT_KV, "pad"),
    ("nq", ATT_Q + 2 * ATT_KV, N_NA_HEADS, "par"),
    ("nk", ATT_Q + 2 * ATT_KV + NA_W, N_NA_HEADS, "dense"),
    ("nv", ATT_Q + 2 * ATT_KV + 2 * NA_W, N_NA_HEADS, "dense"),
    ("rq", ATT_Q + 2 * ATT_KV + 3 * NA_W, N_RET_HEADS, "dup"),
    ("rk", ATT_Q + 2 * ATT_KV + 3 * NA_W + RET_W, N_RET_HEADS, "dup"),
    ("rv", ATT_Q + 2 * ATT_KV + 3 * NA_W + 2 * RET_W, N_RET_HEADS, "dense"),
    ("rg", ATT_Q + 2 * ATT_KV + 3 * NA_W + 3 * RET_W, N_RET_HEADS, "dense"),
)
_SEC_OFF = {}
_off = 0
for _name, _src, _heads, _mode in _SECTIONS:
    _width = _heads * (HEAD_DIM if _mode == "dense" else HB)
    _SEC_OFF[_name] = (_off, _width)
    _off += _width
NC_PAD = _off
_FEATURE_MAJOR = ("aq", "av")
LOG2E = 1.4426950408889634
_YA_W = N_ATT_HEADS * HB
_YN_W = NA_W
_YR_W = RET_W
MIX_PAD = _YA_W + _YN_W + _YR_W


def _vmem_limit(nbytes):
    return int(min(VMEM_LIMIT_CAP, max(16 * 1024 * 1024, 2 * nbytes)))


def _pair_gains(v, heads):
    v = v.reshape(heads, 1, HEAD_DIM).astype(F32)
    even = (np.arange(heads) % 2 == 0)[:, None, None]
    zero = jnp.zeros_like(v)
    return jnp.concatenate([jnp.where(even, v, zero), jnp.where(even, zero, v)], axis=-1)


def _tile_mod_row(i, per_batch=LAT_TILES_PER_BATCH):
    return jnp.where(i < per_batch, 0, jnp.where(i < BATCH * per_batch, 1, 2))


def _ptile_mod_row(i):
    return _tile_mod_row(i, P_LAT_TILES_PER_BATCH)


ADA_TN = 1536


def _ada_kernel(c_ref, w_ref, b_ref, o_ref):
    c = c_ref[...]
    s = c * jax.nn.sigmoid(c)
    o_ref[0] = jnp.dot(s, w_ref[0], preferred_element_type=F32,
                       precision=lax.Precision.HIGHEST) + b_ref[0]


def _ada_mod(cvec, w_ada, b_ada):
    n = 6 * D_MODEL
    return pl.pallas_call(
        _ada_kernel,
        out_shape=jax.ShapeDtypeStruct((DEPTH, 8, n), F32),
        grid=(DEPTH, n // ADA_TN),
        in_specs=[
            pl.BlockSpec((8, D_MODEL), lambda l, j: (0, 0)),
            pl.BlockSpec((1, D_MODEL, ADA_TN), lambda l, j: (l, 0, j)),
            pl.BlockSpec((1, 1, ADA_TN), lambda l, j: (l, 0, j)),
        ],
        out_specs=pl.BlockSpec((1, 8, ADA_TN), lambda l, j: (l, 0, j)),
        compiler_params=pltpu.CompilerParams(
            dimension_semantics=("arbitrary", "arbitrary"),
            vmem_limit_bytes=_vmem_limit(2 * D_MODEL * ADA_TN * 4)),
        name="ada_mod",
    )(cvec, w_ada, b_ada.reshape(DEPTH, 1, n))


def _rope_swap(t):
    lane = lax.broadcasted_iota(jnp.int32, t.shape, 1)
    first_half = (lane % 32) < 16
    return jnp.where(first_half, pltpu.roll(t, LANES - 16, 1), pltpu.roll(t, 16, 1))


def _pick_rows(lat_ref, ctx_ref, split_ctx):
    if not split_ctx:
        return lat_ref[...]
    return jnp.where(pl.program_id(0) >= P_LAT_TILES, ctx_ref[...], lat_ref[...])


def _lat_ctx_specs(width, split_ctx):
    if split_ctx:
        return [pl.BlockSpec((TP, width), lambda i, *_: (jnp.minimum(i, P_LAT_TILES - 1), 0)),
                pl.BlockSpec((TP, width), lambda i, *_: (jnp.maximum(i - P_LAT_TILES, 0), 0))]
    return [pl.BlockSpec((TP, width), lambda i, *_: (i, 0)),
            pl.BlockSpec((TP, width), lambda i, *_: (0, 0))]


def _inproj_kernel(xl_ref, xc_ref, mod_ref, g1_ref, w_ref, cs_ref, sn_ref, qg_ref, kg_ref,
                   aq_ref, ak_ref, av_ref, nq_ref, nk_ref, nv_ref,
                   rq_ref, rk_ref, rv_ref, rg_ref, *, split_ctx):
    x = _pick_rows(xl_ref, xc_ref, split_ctx)
    mod = mod_ref[0]
    sh1 = mod[:, 0:D_MODEL]
    sc1 = mod[:, D_MODEL:2 * D_MODEL]
    ms = jnp.mean(x * x, axis=-1, keepdims=True)
    h = x * lax.rsqrt(ms + EPS) * g1_ref[...]
    h = (h * (1.0 + sc1) + sh1).astype(BF16)
    cs = cs_ref[...]
    sn = sn_ref[...]

    sections = {}

    def proj(name, hidx):
        if name not in sections:
            off, width = _SEC_OFF[name]
            sections[name] = jnp.dot(h, w_ref[:, off:off + width], preferred_element_type=F32)
        return sections[name][:, hidx * HB:(hidx + 1) * HB]

    def normed_rope(z, g):
        ss = jnp.sum(z * z, axis=-1, keepdims=True)
        zn = z * lax.rsqrt(ss * (1.0 / HEAD_DIM) + EPS) * g
        return zn * cs + _rope_swap(zn) * sn

    scale = HEAD_DIM ** -0.5
    for hh in range(N_ATT_HEADS):
        z = normed_rope(proj("aq", hh), qg_ref[...]) * (scale * LOG2E)
        aq_ref[hh * HB:(hh + 1) * HB, :] = z.T.astype(BF16)
    for hh in range(N_ATT_KV):
        z = normed_rope(proj("ak", hh), kg_ref[...])
        ak_ref[:, hh * HB:(hh + 1) * HB] = z.astype(BF16)
        zv = proj("av", hh)
        lane = lax.broadcasted_iota(jnp.int32, zv.shape, 1)
        av_ref[hh * HB:(hh + 1) * HB, :] = jnp.where(lane == HEAD_DIM, 1.0, zv).T.astype(BF16)
    def whole(name):
        off, width = _SEC_OFF[name]
        return jnp.dot(h, w_ref[:, off:off + width], preferred_element_type=F32)

    nq_ref[...] = (whole("nq") * scale).astype(BF16)
    nk_ref[...] = whole("nk").astype(BF16)
    nv_ref[...] = whole("nv").astype(BF16)
    rq_ref[...] = whole("rq").astype(BF16)
    rk_ref[...] = (whole("rk") * scale).astype(BF16)
    rv_ref[...] = whole("rv").astype(BF16)
    rg_ref[...] = whole("rg").astype(BF16)


def _inproj(x_lat, x_ctx, split_ctx, mod3, g1, w_pad, cs_tab, sn_tab, qg, kg):
    names = [s[0] for s in _SECTIONS]
    widths = [_SEC_OFF[n][1] for n in names]

    def tab_map(i):
        return (jnp.where(i < P_LAT_TILES, i % P_LAT_TILES_PER_BATCH, P_LAT_TILES_PER_BATCH), 0)

    est = (D_MODEL * NC_PAD * 2 + 4 * TP * D_MODEL * 4 + 2 * TP * NC_PAD * 2
           + 6 * TP * D_MODEL * 4)
    return pl.pallas_call(
        functools.partial(_inproj_kernel, split_ctx=split_ctx),
        out_shape=[jax.ShapeDtypeStruct((w, T_ALL) if n in _FEATURE_MAJOR else (T_ALL, w), BF16)
                   for n, w in zip(names, widths)],
        grid=(P_ALL_TILES,),
        in_specs=_lat_ctx_specs(D_MODEL, split_ctx) + [
            pl.BlockSpec((1, 1, 6 * D_MODEL), lambda i: (_ptile_mod_row(i), 0, 0)),
            pl.BlockSpec((1, D_MODEL), lambda i: (0, 0)),
            pl.BlockSpec((D_MODEL, NC_PAD), lambda i: (0, 0), pipeline_mode=pl.Buffered(1)),
            pl.BlockSpec((TP, HB), tab_map),
            pl.BlockSpec((TP, HB), tab_map),
            pl.BlockSpec((1, HB), lambda i: (0, 0)),
            pl.BlockSpec((1, HB), lambda i: (0, 0)),
        ],
        out_specs=[pl.BlockSpec((w, TP), lambda i: (0, i)) if n in _FEATURE_MAJOR
                   else pl.BlockSpec((TP, w), lambda i: (i, 0)) for n, w in zip(names, widths)],
        compiler_params=pltpu.CompilerParams(
            dimension_semantics=("arbitrary",), vmem_limit_bytes=_vmem_limit(est)),
        name="norm_inproj",
    )(x_lat, x_ctx, mod3, g1, w_pad, cs_tab, sn_tab, qg, kg)


ATT_TK = 8192
ATT_NK = SEQ // ATT_TK
ATT_CK = 256
ATT_VROWS = HEAD_DIM + 16


def _attn_kernel(qt_ref, k_ref, vt_ref, kc_ref, vct_ref, g_ref, o_ref, m_sc, acc_sc):
    i = pl.program_id(2)
    j = pl.program_id(3)
    is_ctx_q = i >= LAT_TILES_PER_BATCH

    @pl.when(j == 0)
    def _():
        m_sc[...] = jnp.full(m_sc.shape, -jnp.inf, F32)
        acc_sc[...] = jnp.zeros(acc_sc.shape, F32)

    def scores(kr, c, ck):
        k = kr[c * ck:(c + 1) * ck, :]
        return [jnp.dot(k, qt_ref[hh * HB:(hh + 1) * HB, :], preferred_element_type=F32)
                for hh in range(ATT_GRP)]

    def step(kr, vtr, nkeys):
        ck = min(ATT_CK, nkeys)
        nchunk = nkeys // ck
        ss = scores(kr, 0, ck)
        for c in range(nchunk):
            cur = ss
            if c + 1 < nchunk:
                ss = scores(kr, c + 1, ck)
            vt = vtr[0:ATT_VROWS, c * ck:(c + 1) * ck]
            ps, alphas = [], []
            for hh in range(ATT_GRP):
                m_prev = m_sc[hh]
                m_new = jnp.maximum(m_prev, jnp.max(cur[hh], axis=0, keepdims=True))
                alpha = jnp.exp2(m_prev - m_new)
                p = jnp.exp2(cur[hh] - m_new)
                m_sc[hh] = m_new
                ps.append(p.astype(BF16))
                alphas.append(alpha)
            for hh in range(ATT_GRP):
                acc_sc[hh] = alphas[hh] * acc_sc[hh] + jnp.dot(vt, ps[hh],
                                                               preferred_element_type=F32)

    @pl.when(jnp.logical_not(is_ctx_q))
    def _():
        step(k_ref, vt_ref, ATT_TK)

    @pl.when(j == ATT_NK - 1)
    def _():
        step(kc_ref, vct_ref, CTX_LEN)
        for hh in range(ATT_GRP):
            acc = acc_sc[hh]
            o = acc[0:HEAD_DIM] / acc[HEAD_DIM:HEAD_DIM + 1]
            ms = jnp.sum(o * o, axis=0, keepdims=True) * (1.0 / HEAD_DIM)
            g = g_ref[hh * HB:hh * HB + HEAD_DIM, :]
            y = o * lax.rsqrt(ms + EPS) * jnp.concatenate([g] * (TM // LANES), axis=1)
            y = jnp.concatenate([y, jnp.zeros_like(y)], axis=0)
            o_ref[:, hh * HB:(hh + 1) * HB] = y.T.astype(BF16)


def _gqa(aqt, ak, avt, ga_cols, want_ctx):
    nq = LAT_TILES_PER_BATCH + (1 if want_ctx else 0)

    def q_tile(b, i):
        return jnp.where(i < LAT_TILES_PER_BATCH, b * LAT_TILES_PER_BATCH + i, LAT_TILES + b)

    est = (2 * ATT_GRP * HB * TM * 2 + 4 * ATT_TK * HB * 2 + 4 * TM * HB * 2
           + ATT_GRP * TM * HB * 4 * 3 + 8 * TM * ATT_TK * 4)
    return pl.pallas_call(
        _attn_kernel,
        out_shape=jax.ShapeDtypeStruct((T_ALL if want_ctx else T_LAT, N_ATT_HEADS * HB), BF16),
        grid=(BATCH, N_ATT_KV, nq, ATT_NK),
        in_specs=[
            pl.BlockSpec((ATT_GRP * HB, TM), lambda b, c, i, j: (c, q_tile(b, i))),
            pl.BlockSpec((ATT_TK, HB), lambda b, c, i, j: (b * ATT_NK + j, c)),
            pl.BlockSpec((HB, ATT_TK), lambda b, c, i, j: (c, b * ATT_NK + j)),
            pl.BlockSpec((CTX_LEN, HB), lambda b, c, i, j: (LAT_TILES + b, c)),
            pl.BlockSpec((HB, CTX_LEN), lambda b, c, i, j: (c, LAT_TILES + b)),
            pl.BlockSpec((ATT_GRP * HB, LANES), lambda b, c, i, j: (c, 0)),
        ],
        out_specs=pl.BlockSpec((TM, ATT_GRP * HB), lambda b, c, i, j: (q_tile(b, i), c)),
        scratch_shapes=[
            pltpu.VMEM((ATT_GRP, 1, TM), F32),
            pltpu.VMEM((ATT_GRP, ATT_VROWS, TM), F32),
        ],
        compiler_params=pltpu.CompilerParams(
            dimension_semantics=("arbitrary",) * 4, vmem_limit_bytes=_vmem_limit(est)),
        name="gqa_attn",
    )(aqt, ak, avt, ak, avt, ga_cols)


NA_BAND = NA_WIN_ROWS * GRID_W
NA_CLASSES = 8
NA_ROWS_PER_ITER = 32
_NA_CLASS_ROWS = (0, 1, 2, 3, GRID_ROWS // 2, GRID_ROWS - 3, GRID_ROWS - 2, GRID_ROWS - 1)


def _na_bias_table(rpb):
    wr, wc = NA_WIN_ROWS, NA_WIN_COLS
    r = np.asarray(_NA_CLASS_ROWS)
    ridx = np.clip(r - wr // 2, 0, GRID_ROWS - wr)[:, None] + np.arange(wr)[None, :]
    dr = ridx - r[:, None] + (wr - 1)
    col = np.arange(GRID_W)
    cstart = np.clip(col - wc // 2, 0, GRID_W - wc)
    col_ok = (col[None, :] >= cstart[:, None]) & (col[None, :] < cstart[:, None] + wc)
    dc = np.clip(col[None, :] - col[:, None] + (wc - 1), 0, 2 * wc - 2)
    pick_r = (dr[:, :, None] == np.arange(2 * wr - 1)[None, None, :]).astype(np.float32)
    pick_c = (dc[:, :, None] == np.arange(2 * wc - 1)[None, None, :]).astype(np.float32)
    hp = lax.Precision.HIGHEST
    by_col = jnp.einsum("hrc,qkc->hrqk", rpb.astype(F32), jnp.asarray(pick_c), precision=hp)
    bias = jnp.einsum("hrqk,cwr->hcqwk", by_col, jnp.asarray(pick_r), precision=hp)
    bias = bias.reshape(N_NA_HEADS, NA_CLASSES, GRID_W, NA_BAND)
    mask = np.tile(col_ok, (1, wr))
    return jnp.where(mask[None, None], bias, NEG_INF)


def _head_rms_gain(o, g):
    ms = jnp.sum(o * o, axis=-1, keepdims=True) * (1.0 / HEAD_DIM)
    return o * lax.rsqrt(ms + EPS) * g


def _pair_store(ref, rows, o, g, odd, post=None):
    lane = lax.broadcasted_iota(jnp.int32, o.shape, 1)
    mine = (lane >= HEAD_DIM) if odd else (lane < HEAD_DIM)
    y = _head_rms_gain(jnp.where(mine, o, 0.0), g)
    if post is not None:
        y = y * post
    if odd:
        y = y + ref[rows, :].astype(F32)
    ref[rows, :] = y.astype(BF16)


def _for_head_parity(body):
    is_odd = pl.program_id(1) % 2 == 1
    pl.when(jnp.logical_not(is_odd))(functools.partial(body, False))
    pl.when(is_odd)(functools.partial(body, True))


def _na_kernel(q_ref, k_ref, v_ref, kc_ref, vc_ref, qc_ref, bias_ref, g_ref, o_ref, oc_ref,
               *, want_ctx):
    _for_head_parity(functools.partial(
        _na_body, q_ref, k_ref, v_ref, kc_ref, vc_ref, qc_ref, bias_ref, g_ref, o_ref, oc_ref,
        want_ctx))


def _na_body(q_ref, k_ref, v_ref, kc_ref, vc_ref, qc_ref, bias_ref, g_ref, o_ref, oc_ref,
             want_ctx, odd):
    kc = kc_ref[...]
    vc = vc_ref[...]
    g = g_ref[0]
    half = NA_WIN_ROWS // 2
    last = GRID_ROWS - NA_WIN_ROWS

    def rows(it, carry):
        r0 = it * NA_ROWS_PER_ITER
        qrows, bands, scores = [], [], []
        for d in range(NA_ROWS_PER_ITER):
            r = r0 + d
            start = jnp.clip(r - half, 0, last)
            cls = jnp.where(r < half, r, jnp.where(r > last + half, r - last, half))
            qrow = pl.ds(pl.multiple_of(r * GRID_W, GRID_W), GRID_W)
            band = pl.ds(pl.multiple_of(start * GRID_W, GRID_W), NA_BAND)
            q = q_ref[qrow, :]
            s = lax.dot_general(q, k_ref[band, :], (((1,), (1,)), ((), ())),
                                preferred_element_type=F32)
            sc = lax.dot_general(q, kc, (((1,), (1,)), ((), ())), preferred_element_type=F32)
            qrows.append(qrow)
            bands.append(band)
            scores.append((s, sc, cls))
        probs = []
        for s, sc, cls in scores:
            bt = bias_ref[0, cls]
            s = jnp.where(bt > 0.5 * NEG_INF, s + bt, NEG_INF)
            m = jnp.maximum(jnp.max(s, axis=-1, keepdims=True),
                            jnp.max(sc, axis=-1, keepdims=True))
            p = jnp.exp(s - m)
            pc = jnp.exp(sc - m)
            l = jnp.sum(p, axis=-1, keepdims=True) + jnp.sum(pc, axis=-1, keepdims=True)
            probs.append((p.astype(BF16), pc.astype(BF16), l))
        for qrow, band, (p, pc, l) in zip(qrows, bands, probs):
            o = (jnp.dot(p, v_ref[band, :], preferred_element_type=F32)
                 + jnp.dot(pc, vc, preferred_element_type=F32)) / l
            _pair_store(o_ref, qrow, o, g, odd)
        return carry

    lax.fori_loop(0, GRID_ROWS // NA_ROWS_PER_ITER, rows, 0)

    if want_ctx:
        sc = lax.dot_general(qc_ref[...], kc, (((1,), (1,)), ((), ())),
                             preferred_element_type=F32)
        m = jnp.max(sc, axis=-1, keepdims=True)
        pc = jnp.exp(sc - m)
        l = jnp.sum(pc, axis=-1, keepdims=True)
        o = jnp.dot(pc.astype(BF16), vc, preferred_element_type=F32) / l
        _pair_store(oc_ref, slice(None), o, g, odd)
    elif not odd:
        oc_ref[...] = jnp.zeros(oc_ref.shape, oc_ref.dtype)


def _pair_specs(rows, first_block=0):
    return (pl.BlockSpec((rows, HB), lambda b, h: (first_block + b, h)),
            pl.BlockSpec((rows, HB), lambda b, h: (first_block + b, h // 2)))


def _neigh(nq, nk, nv, bias_tab, gn, want_ctx):
    lat, lat_pair = _pair_specs(SEQ)
    ctx, ctx_pair = _pair_specs(CTX_LEN, T_LAT // CTX_LEN)
    est = 2 * (4 * SEQ * HB * 2 + 4 * CTX_LEN * HB * 2 + NA_CLASSES * GRID_W * NA_BAND * 4)
    return pl.pallas_call(
        functools.partial(_na_kernel, want_ctx=want_ctx),
        out_shape=[jax.ShapeDtypeStruct((T_LAT, _YN_W), BF16),
                   jax.ShapeDtypeStruct((T_CTX, _YN_W), BF16)],
        grid=(BATCH, N_NA_HEADS),
        in_specs=[lat, lat_pair, lat_pair, ctx_pair, ctx_pair, ctx,
                  pl.BlockSpec((1, NA_CLASSES, GRID_W, NA_BAND), lambda b, h: (h, 0, 0, 0)),
                  pl.BlockSpec((1, 1, HB), lambda b, h: (h, 0, 0))],
        out_specs=[_pair_specs(SEQ)[1], _pair_specs(CTX_LEN)[1]],
        compiler_params=pltpu.CompilerParams(
            dimension_semantics=("arbitrary", "arbitrary"), vmem_limit_bytes=_vmem_limit(est)),
        name="neigh_attn",
    )(nq, nk, nv, nk, nv, nq, bias_tab, gn)


RET_NCHUNK = SEQ // RET_CHUNK
RET_NCHUNK_CTX = CTX_LEN // RET_CHUNK
RET_CHUNKS_PER_ITER = 16


def _ret_tables(log_g2):
    lf = log_g2[0][:, None, None]
    lb = log_g2[1][:, None, None]
    pos = jnp.arange(RET_CHUNK, dtype=F32)
    i = pos[None, :, None]
    j = pos[None, None, :]
    diff = i - j
    dm = jnp.where(diff > 0, jnp.exp(lf * jnp.maximum(diff, 0.0)),
                   jnp.where(diff < 0, jnp.exp(lb * jnp.maximum(-diff, 0.0)), 2.0)) * 0.5
    fwd_lane = (jnp.arange(LANES) < HEAD_DIM)[None, None, :]
    xi = jnp.where(fwd_lane, jnp.exp(lf * (i + 1.0)), jnp.exp(lb * (RET_CHUNK - i)))
    zt = jnp.where(fwd_lane, jnp.exp(lf * (RET_CHUNK - 1.0 - i)), jnp.exp(lb * i))
    fwd_row = (jnp.arange(LANES) < HEAD_DIM)[None, :, None]
    dec = jnp.where(fwd_row, jnp.exp(lf * RET_CHUNK), jnp.exp(lb * RET_CHUNK))
    dec = jnp.broadcast_to(dec, (N_RET_HEADS, LANES, LANES))
    return dm.astype(F32), xi.astype(F32), zt.astype(F32), dec.astype(F32)


def _ret_kernel(*refs, want_ctx):
    _for_head_parity(functools.partial(_ret_body, *refs, want_ctx))


def _ret_body(q_ref, k_ref, v_ref, gt_ref, qc_ref, kc_ref, vc_ref, gtc_ref,
              dm_ref, xi_ref, zt_ref, dec_ref, g_ref, o_ref, oc_ref,
              u_sc, s_sc, uc_sc, sc_sc, want_ctx, odd):
    dm = dm_ref[0]
    xi = xi_ref[0]
    zt = zt_ref[0]
    dec = dec_ref[0]
    dec_f = dec[0:HEAD_DIM]
    dec_b = dec[HEAD_DIM:LANES]
    g = g_ref[0]
    C = RET_CHUNK

    def chunk_rows(n):
        return pl.ds(pl.multiple_of(n * C, C), C)

    def chunk_state_update(kr, vr, usc, ns):
        kzs = [(kr[chunk_rows(n), :].astype(F32) * zt).T.astype(BF16) for n in ns]
        for n, kz in zip(ns, kzs):
            usc[n] = jnp.dot(kz, vr[chunk_rows(n), :], preferred_element_type=F32)

    def chunk_out(qr, kr, vr, gtr, ssc, outr, ns):
        qds = [qr[chunk_rows(n), :] for n in ns]
        s2s = [lax.dot_general(qd, kr[chunk_rows(n), :], (((1,), (1,)), ((), ())),
                               preferred_element_type=F32) for n, qd in zip(ns, qds)]
        outs = []
        for n, qd, s2 in zip(ns, qds, s2s):
            inner = jnp.dot((s2 * dm).astype(BF16), vr[chunk_rows(n), :],
                            preferred_element_type=F32)
            qx = (qd.astype(F32) * xi).astype(BF16)
            outs.append(inner + jnp.dot(qx, ssc[n].astype(BF16), preferred_element_type=F32))
        for n, o in zip(ns, outs):
            gate = gtr[chunk_rows(n), :].astype(F32)
            _pair_store(outr, chunk_rows(n), o, g, odd, post=gate * jax.nn.sigmoid(gate))

    def scan_states(usc, ssc, nchunk, init_f, init_b):
        def fwd(n, sf):
            ssc[n, 0:HEAD_DIM, :] = sf
            return dec_f * sf + usc[n, 0:HEAD_DIM, :]

        def bwd(t, sb):
            n = nchunk - 1 - t
            ssc[n, HEAD_DIM:LANES, :] = sb
            return dec_b * sb + usc[n, HEAD_DIM:LANES, :]

        return (lax.fori_loop(0, nchunk, fwd, init_f), lax.fori_loop(0, nchunk, bwd, init_b))

    zero = jnp.zeros((HEAD_DIM, LANES), F32)
    ctx_chunks = list(range(RET_NCHUNK_CTX))
    chunk_state_update(kc_ref, vc_ref, uc_sc, ctx_chunks)
    ctx_f, ctx_b = scan_states(uc_sc, sc_sc, RET_NCHUNK_CTX, zero, zero)
    if want_ctx:
        chunk_out(qc_ref, kc_ref, vc_ref, gtc_ref, sc_sc, oc_ref, ctx_chunks)
    elif not odd:
        oc_ref[...] = jnp.zeros(oc_ref.shape, oc_ref.dtype)

    def upd(it, carry):
        chunk_state_update(k_ref, v_ref, u_sc,
                           [it * RET_CHUNKS_PER_ITER + d for d in range(RET_CHUNKS_PER_ITER)])
        return carry

    lax.fori_loop(0, RET_NCHUNK // RET_CHUNKS_PER_ITER, upd, 0)
    scan_states(u_sc, s_sc, RET_NCHUNK, ctx_f, ctx_b)

    def out(it, carry):
        chunk_out(q_ref, k_ref, v_ref, gt_ref, s_sc, o_ref,
                  [it * RET_CHUNKS_PER_ITER + d for d in range(RET_CHUNKS_PER_ITER)])
        return carry

    lax.fori_loop(0, RET_NCHUNK // RET_CHUNKS_PER_ITER, out, 0)


def _retention(rq, rk, rv, rg, tables, gr, want_ctx):
    lat, lat_pair = _pair_specs(SEQ)
    ctx, ctx_pair = _pair_specs(CTX_LEN, T_LAT // CTX_LEN)
    tab = pl.BlockSpec((1, LANES, LANES), lambda b, h: (h, 0, 0))
    est = (2 * 5 * SEQ * HB * 2 + 2 * RET_NCHUNK * LANES * LANES * 4 + 8 * LANES * LANES * 4)
    return pl.pallas_call(
        functools.partial(_ret_kernel, want_ctx=want_ctx),
        out_shape=[jax.ShapeDtypeStruct((T_LAT, _YR_W), BF16),
                   jax.ShapeDtypeStruct((T_CTX, _YR_W), BF16)],
        grid=(BATCH, N_RET_HEADS),
        in_specs=[lat, lat, lat_pair, lat_pair, ctx, ctx, ctx_pair, ctx_pair, tab, tab, tab, tab,
                  pl.BlockSpec((1, 1, HB), lambda b, h: (h, 0, 0))],
        out_specs=[_pair_specs(SEQ)[1], _pair_specs(CTX_LEN)[1]],
        scratch_shapes=[
            pltpu.VMEM((RET_NCHUNK, LANES, LANES), F32),
            pltpu.VMEM((RET_NCHUNK, LANES, LANES), F32),
            pltpu.VMEM((RET_NCHUNK_CTX, LANES, LANES), F32),
            pltpu.VMEM((RET_NCHUNK_CTX, LANES, LANES), F32),
        ],
        compiler_params=pltpu.CompilerParams(
            dimension_semantics=("arbitrary", "arbitrary"), vmem_limit_bytes=_vmem_limit(est)),
        name="retention",
    )(rq, rk, rv, rg, rq, rk, rv, rg, *tables, gr)


MOE_SLOT = TM
META_ROWS = 8
_META_GATE0 = 2


def _route_t(logt):
    row = lax.broadcasted_iota(jnp.int32, logt.shape, 0).astype(F32)
    p = jnp.exp(logt - jnp.max(logt, axis=0, keepdims=True))
    best = None
    for grp in range(N_EXPERT_GROUPS):
        lo = float(grp * EXPERTS_PER_GROUP)
        ing = (row >= lo) & (row < lo + EXPERTS_PER_GROUP)
        pg = jnp.where(ing, p, -1.0)
        m1 = jnp.max(pg, axis=0, keepdims=True)
        i1 = jnp.min(jnp.where(pg == m1, row, float(N_EXPERTS)), axis=0, keepdims=True)
        pg2 = jnp.where(row == i1, -1.0, pg)
        m2 = jnp.max(pg2, axis=0, keepdims=True)
        i2 = jnp.min(jnp.where(pg2 == m2, row, float(N_EXPERTS)), axis=0, keepdims=True)
        cand = (m1 + m2, m1, m2, i1, i2, jnp.zeros_like(m1) + lo)
        if best is None:
            best = cand
        else:
            better = cand[0] > best[0]
            best = tuple(jnp.where(better, c, b) for c, b in zip(cand, best))
    _, m1, m2, i1, i2, base = best
    w = m1 + m2
    gates = [jnp.where(i1 == base + e, m1 / w, jnp.where(i2 == base + e, m2 / w, 0.0))
             for e in range(EXPERTS_PER_GROUP)]
    return base * (1.0 / EXPERTS_PER_GROUP), gates


def _merge_kernel(ya_ref, ynl_ref, ync_ref, yrl_ref, yrc_ref, xl_ref, xc_ref, mod_ref, g2_ref,
                  wo_ref, wr_ref, brt_ref,
                  xn_ref, h2_ref, metat_ref, metac_ref, ctab_ref, tot_ref, carry_sc, *, split_ctx):
    @pl.when(pl.program_id(0) == 0)
    def _():
        carry_sc[...] = jnp.zeros(carry_sc.shape, F32)

    mod = mod_ref[0]
    gt1 = mod[:, 2 * D_MODEL:3 * D_MODEL]
    sh2 = mod[:, 3 * D_MODEL:4 * D_MODEL]
    sc2 = mod[:, 4 * D_MODEL:5 * D_MODEL]
    yn = _pick_rows(ynl_ref, ync_ref, split_ctx)
    yr = _pick_rows(yrl_ref, yrc_ref, split_ctx)
    m = (jnp.dot(ya_ref[...], wo_ref[0:_YA_W, :], preferred_element_type=F32)
         + jnp.dot(yn, wo_ref[_YA_W:_YA_W + _YN_W, :], preferred_element_type=F32)
         + jnp.dot(yr, wo_ref[_YA_W + _YN_W:MIX_PAD, :], preferred_element_type=F32))
    x = _pick_rows(xl_ref, xc_ref, split_ctx) + gt1 * m
    xn_ref[...] = x
    ms = jnp.mean(x * x, axis=-1, keepdims=True)
    h2 = x * lax.rsqrt(ms + EPS) * g2_ref[...] * (1.0 + sc2) + sh2
    h2_ref[...] = h2.astype(BF16)

    ntile = TP // LANES
    h_hi = h2.astype(BF16)
    h_lo = (h2 - h_hi.astype(F32)).astype(BF16)
    wr = wr_ref[...]
    w_hi = wr.astype(BF16)
    w_lo = (wr - w_hi.astype(F32)).astype(BF16)
    both = jnp.dot(h_hi, jnp.concatenate([w_hi, w_lo], axis=1), preferred_element_type=F32)
    logits = (both[:, 0:LANES] + both[:, LANES:2 * LANES]
              + jnp.dot(h_lo, w_hi, preferred_element_type=F32))
    logt = logits.T[0:N_EXPERTS, :]
    gsel, gates = _route_t(logt + jnp.concatenate([brt_ref[...]] * ntile, axis=1))

    grow = lax.broadcasted_iota(jnp.int32, (META_ROWS, TP), 0).astype(F32)
    onehot = jnp.where(grow == gsel, 1.0, 0.0)
    earlier = (lax.broadcasted_iota(jnp.int32, (TP, TP), 0)
               < lax.broadcasted_iota(jnp.int32, (TP, TP), 1))
    excl = jnp.dot(onehot.astype(BF16), jnp.where(earlier, 1.0, 0.0).astype(BF16),
                   preferred_element_type=F32)
    carry = carry_sc[...]
    rank = jnp.sum(onehot * (jnp.concatenate([carry] * ntile, axis=1) + excl),
                   axis=0, keepdims=True)
    for sub in range(TP // TM):
        ctab_ref[sub] = carry
        carry = carry + jnp.sum(onehot[:, sub * TM:(sub + 1) * TM], axis=1, keepdims=True)
    carry_sc[...] = carry
    tot_ref[...] = carry

    metat = jnp.concatenate([gsel, rank] + gates
                            + [jnp.zeros((META_ROWS - _META_GATE0 - EXPERTS_PER_GROUP, TP), F32)],
                            axis=0)
    metat_ref[...] = metat
    hi = metat.astype(BF16).astype(F32)
    mid = (metat - hi).astype(BF16).astype(F32)
    lo = (metat - hi - mid).astype(BF16).astype(F32)
    metac_ref[...] = jnp.concatenate(
        [metat, hi, mid, lo, jnp.zeros((LANES - 4 * META_ROWS, TP), F32)], axis=0).T


def _merge(ya, yn_lat, yn_ctx, yr_lat, yr_ctx, x_lat, x_ctx, split_ctx, mod3, g2, wo_pad, wr_pad,
           brt, ntiles):
    rows = ntiles * TP
    est = (2 * MIX_PAD * D_MODEL * 2 + 4 * TP * MIX_PAD * 2 + 10 * TP * D_MODEL * 4
           + 4 * TP * TP * 4)
    const = lambda i: (0, 0)
    return pl.pallas_call(
        functools.partial(_merge_kernel, split_ctx=split_ctx),
        out_shape=[jax.ShapeDtypeStruct((rows, D_MODEL), F32),
                   jax.ShapeDtypeStruct((rows, D_MODEL), BF16),
                   jax.ShapeDtypeStruct((META_ROWS, rows), F32),
                   jax.ShapeDtypeStruct((rows, LANES), F32),
                   jax.ShapeDtypeStruct((ntiles * (TP // TM), META_ROWS, LANES), F32),
                   jax.ShapeDtypeStruct((META_ROWS, LANES), F32)],
        grid=(ntiles,),
        in_specs=([pl.BlockSpec((TP, _YA_W), lambda i: (i, 0))]
                  + _lat_ctx_specs(_YN_W, split_ctx) + _lat_ctx_specs(_YR_W, split_ctx)
                  + _lat_ctx_specs(D_MODEL, split_ctx) + [
            pl.BlockSpec((1, 1, 6 * D_MODEL), lambda i: (_ptile_mod_row(i), 0, 0)),
            pl.BlockSpec((1, D_MODEL), const),
            pl.BlockSpec((MIX_PAD, D_MODEL), const),
            pl.BlockSpec((D_MODEL, LANES), const),
            pl.BlockSpec((N_EXPERTS, LANES), const),
        ]),
        out_specs=[pl.BlockSpec((TP, D_MODEL), lambda i: (i, 0)),
                   pl.BlockSpec((TP, D_MODEL), lambda i: (i, 0)),
                   pl.BlockSpec((META_ROWS, TP), lambda i: (0, i)),
                   pl.BlockSpec((TP, LANES), lambda i: (i, 0)),
                   pl.BlockSpec((TP // TM, META_ROWS, LANES), lambda i: (i, 0, 0)),
                   pl.BlockSpec((META_ROWS, LANES), const)],
        scratch_shapes=[pltpu.VMEM((META_ROWS, LANES), F32)],
        compiler_params=pltpu.CompilerParams(
            dimension_semantics=("arbitrary",), vmem_limit_bytes=_vmem_limit(est)),
        name="merge_outproj_router",
    )(ya, yn_lat, yn_ctx, yr_lat, yr_ctx, x_lat, x_ctx, mod3, g2, wo_pad, wr_pad, brt)


def _moe_plan(ctab, tot, ntiles):
    grp = N_EXPERT_GROUPS
    i32 = jnp.int32
    a = ctab[:, :grp, 0].astype(i32).T
    totg = tot[:grp, 0].astype(i32)
    b = jnp.concatenate([a[:, 1:], totg[:, None]], axis=1)
    nslot = (totg + MOE_SLOT - 1) // MOE_SLOT
    slot_end = jnp.cumsum(nslot)
    slot_base = slot_end - nslot
    total_slots = slot_end[-1]
    nchunk = ntiles // (MOE_CHUNK // TM)
    ac = a[:, ::MOE_CHUNK // TM]
    bc = jnp.concatenate([ac[:, 1:], totg[:, None]], axis=1)
    first_j = ac // MOE_SLOT
    last_j = (jnp.maximum(bc, 1) - 1) // MOE_SLOT
    npairs = jnp.where(bc > ac, last_j - first_j + 1, 0).reshape(-1)
    cum = jnp.cumsum(npairs)
    start = cum - npairs
    total_pairs = cum[-1]
    n_pairs_max = grp * nchunk + ntiles + grp
    pidx = jnp.arange(n_pairs_max, dtype=i32)
    p = jnp.minimum(pidx, total_pairs - 1)
    gc = jnp.sum((cum[None, :] <= p[:, None]).astype(i32), axis=1)
    slot = slot_base[gc // nchunk] + first_j.reshape(-1)[gc] + (p - start[gc])
    valid = pidx < total_pairs
    prev_slot = jnp.concatenate([jnp.full((1,), -1, i32), slot[:-1]])
    next_slot = jnp.concatenate([slot[1:], jnp.full((1,), -1, i32)])
    is_first = valid & (slot != prev_slot)
    is_last = valid & ((slot != next_slot) | (pidx == total_pairs - 1))
    n_slots_max = ntiles + grp
    fill = jnp.logical_not(valid) & (total_slots < n_slots_max)
    slot = jnp.where(fill, jnp.minimum(total_slots + pidx - total_pairs, n_slots_max - 1), slot)
    flags = (is_first.astype(i32) + 2 * is_last.astype(i32) + 4 * valid.astype(i32)
             + 8 * fill.astype(i32))
    sidx = jnp.arange(n_slots_max, dtype=i32)
    sgrp = jnp.minimum(jnp.sum((slot_end[None, :] <= sidx[:, None]).astype(i32), axis=1), grp - 1)
    sr0 = (sidx - slot_base[sgrp]) * MOE_SLOT
    row0 = slot_base[:, None] * MOE_SLOT + a
    row1 = slot_base[:, None] * MOE_SLOT + jnp.maximum(b, a + 1) - 1
    wb = jnp.stack([row0 // MOE_SLOT, row1 // MOE_SLOT], axis=1)
    wb = jnp.clip(wb, 0, total_slots - 1).reshape(-1).astype(i32)
    return dict(pslot=slot.astype(i32), pchunk=(gc % nchunk).astype(i32), pflag=flags,
                sgrp=sgrp, sr0=sr0.astype(i32), wb=wb, sbase=slot_base.astype(i32),
                n_pairs=n_pairs_max, n_slots=n_slots_max)


def _moe_kernel(pslot_ref, pchunk_ref, pflag_ref, sgrp_ref, sr0_ref,
                h_ref, metat_ref, metac_ref, w1_ref, w3_ref, w2_ref, o_ref, x_sc, g_sc):
    p = pl.program_id(0)
    flags = pflag_ref[p]
    slot = pslot_ref[p]

    @pl.when((flags & 8) != 0)
    def _():
        o_ref[...] = jnp.zeros(o_ref.shape, o_ref.dtype)

    @pl.when((flags & 1) != 0)
    def _():
        x_sc[...] = jnp.zeros(x_sc.shape, F32)
        g_sc[...] = jnp.zeros(g_sc.shape, F32)

    @pl.when((flags & 4) != 0)
    def _():
        mt = metat_ref[...].astype(jnp.int32)
        want = lax.broadcasted_iota(jnp.int32, (MOE_SLOT, MOE_CHUNK), 0) + sr0_ref[slot]
        sel = (mt[1:2, :] == want) & (mt[0:1, :] == sgrp_ref[slot])
        pm = jnp.where(sel, 1.0, 0.0).astype(BF16)
        x_sc[...] += jnp.dot(pm, h_ref[...], preferred_element_type=F32)
        g_sc[...] += jnp.dot(pm, metac_ref[...].astype(BF16), preferred_element_type=F32)

    @pl.when((flags & 2) != 0)
    def _():
        x = x_sc[...].astype(BF16)
        pieces = g_sc[...]
        gs = (pieces[:, META_ROWS:2 * META_ROWS] + pieces[:, 2 * META_ROWS:3 * META_ROWS]
              + pieces[:, 3 * META_ROWS:4 * META_ROWS])
        acc = jnp.zeros((MOE_SLOT, D_MODEL), F32)
        for e in range(EXPERTS_PER_GROUP):
            a = jnp.dot(x, w1_ref[0, 0, e], preferred_element_type=F32)
            b = jnp.dot(x, w3_ref[0, 0, e], preferred_element_type=F32)
            gate = gs[:, _META_GATE0 + e:_META_GATE0 + e + 1]
            act = (a * jax.nn.sigmoid(a)) * b * gate
            acc = acc + jnp.dot(act.astype(BF16), w2_ref[0, 0, e], preferred_element_type=F32)
        o_ref[...] = acc.astype(BF16)


def _moe_sorted(plan, h2, metat, metac, w1g, w3g, w2g, layer):
    est = (2 * 3 * EXPERTS_PER_GROUP * D_MODEL * D_EXPERT * 2 + 6 * MOE_CHUNK * D_MODEL * 2
           + 2 * MOE_SLOT * D_MODEL * 4 + 6 * MOE_SLOT * D_EXPERT * 4
           + 4 * MOE_CHUNK * MOE_SLOT * 4)
    wmap = lambda p, ps, pc, pf, sg, sr: (layer, sg[ps[p]], 0, 0, 0)
    grid_spec = pltpu.PrefetchScalarGridSpec(
        num_scalar_prefetch=5,
        grid=(plan["n_pairs"],),
        in_specs=[
            pl.BlockSpec((MOE_CHUNK, D_MODEL), lambda p, ps, pc, pf, sg, sr: (pc[p], 0)),
            pl.BlockSpec((META_ROWS, MOE_CHUNK), lambda p, ps, pc, pf, sg, sr: (0, pc[p])),
            pl.BlockSpec((MOE_CHUNK, LANES), lambda p, ps, pc, pf, sg, sr: (pc[p], 0)),
            pl.BlockSpec((1, 1, EXPERTS_PER_GROUP, D_MODEL, D_EXPERT), wmap),
            pl.BlockSpec((1, 1, EXPERTS_PER_GROUP, D_MODEL, D_EXPERT), wmap),
            pl.BlockSpec((1, 1, EXPERTS_PER_GROUP, D_EXPERT, D_MODEL), wmap),
        ],
        out_specs=pl.BlockSpec((MOE_SLOT, D_MODEL), lambda p, ps, pc, pf, sg, sr: (ps[p], 0)),
        scratch_shapes=[pltpu.VMEM((MOE_SLOT, D_MODEL), F32), pltpu.VMEM((MOE_SLOT, LANES), F32)],
    )
    return pl.pallas_call(
        _moe_kernel,
        out_shape=jax.ShapeDtypeStruct((plan["n_slots"] * MOE_SLOT, D_MODEL), BF16),
        grid_spec=grid_spec,
        compiler_params=pltpu.CompilerParams(
            dimension_semantics=("arbitrary",), vmem_limit_bytes=_vmem_limit(est)),
        name="moe_sorted_experts",
    )(plan["pslot"], plan["pchunk"], plan["pflag"], plan["sgrp"], plan["sr0"],
      h2, metat, metac, w1g, w3g, w2g)


_COMBINE_WINDOWS = 2 * N_EXPERT_GROUPS


def _combine_kernel(wb_ref, sb_ref, *refs, ntiles, final):
    y_refs = refs[:_COMBINE_WINDOWS]
    metac_ref, xn_ref, mod_ref, fg_ref, o_ref, m_sc = refs[_COMBINE_WINDOWS:]
    i = pl.program_id(0)
    mc = metac_ref[...]
    grp = mc[:, 0:1].astype(jnp.int32)
    rank = mc[:, 1:2].astype(jnp.int32)
    col = lax.broadcasted_iota(jnp.int32, (TM, MOE_SLOT), 1)

    def window(g, k):
        blk = wb_ref[(2 * g + k) * ntiles + i]
        sel = (grp == g) & (rank + sb_ref[g] * MOE_SLOT == col + blk * MOE_SLOT)
        return jnp.dot(jnp.where(sel, 1.0, 0.0).astype(BF16), y_refs[2 * g + k][...],
                       preferred_element_type=F32)

    m = window(0, 0)
    for g in range(1, N_EXPERT_GROUPS):
        m = m + window(g, 0)
    m_sc[...] = m
    for g in range(N_EXPERT_GROUPS):
        @pl.when(wb_ref[(2 * g + 1) * ntiles + i] != wb_ref[(2 * g) * ntiles + i])
        def _(g=g):
            m_sc[...] += window(g, 1)
    gt2 = mod_ref[0][:, 5 * D_MODEL:6 * D_MODEL]
    x = xn_ref[...] + gt2 * m_sc[...]
    if final:
        ms = jnp.mean(x * x, axis=-1, keepdims=True)
        x = x * lax.rsqrt(ms + EPS) * fg_ref[...]
    o_ref[...] = x


def _combine(plan, ys, metac, xn, mod3, fg, ntiles, final):
    def ymap(w):
        return lambda i, wb, sb: (wb[w * ntiles + i], 0)

    est = (2 * _COMBINE_WINDOWS * MOE_SLOT * D_MODEL * 2 + 8 * TM * D_MODEL * 4
           + 4 * TM * MOE_SLOT * 4)
    grid_spec = pltpu.PrefetchScalarGridSpec(
        num_scalar_prefetch=2,
        grid=(ntiles,),
        in_specs=[pl.BlockSpec((MOE_SLOT, D_MODEL), ymap(w)) for w in range(_COMBINE_WINDOWS)] + [
            pl.BlockSpec((TM, LANES), lambda i, wb, sb: (i, 0)),
            pl.BlockSpec((TM, D_MODEL), lambda i, wb, sb: (i, 0)),
            pl.BlockSpec((1, 1, 6 * D_MODEL), lambda i, wb, sb: (_tile_mod_row(i), 0, 0)),
            pl.BlockSpec((1, D_MODEL), lambda i, wb, sb: (0, 0)),
        ],
        out_specs=pl.BlockSpec((TM, D_MODEL), lambda i, wb, sb: (i, 0)),
        scratch_shapes=[pltpu.VMEM((TM, D_MODEL), F32)],
    )
    return pl.pallas_call(
        functools.partial(_combine_kernel, ntiles=ntiles, final=final),
        out_shape=jax.ShapeDtypeStruct((ntiles * TM, D_MODEL), F32),
        grid_spec=grid_spec,
        compiler_params=pltpu.CompilerParams(
            dimension_semantics=("arbitrary",), vmem_limit_bytes=_vmem_limit(est)),
        name="moe_combine",
    )(plan["wb"], plan["sbase"], *([ys] * _COMBINE_WINDOWS), metac, xn, mod3, fg)


def _rope_tables():
    t = np.arange(SEQ)
    nf = HEAD_DIM // 4
    inv = (np.float32(ROPE_THETA) ** (-np.arange(nf, dtype=np.float32) / np.float32(nf)))
    inv = inv.astype(np.float32)
    ang_r = (t // GRID_W).astype(np.float32)[:, None] * inv[None, :]
    ang_c = (t % GRID_W).astype(np.float32)[:, None] * inv[None, :]
    cr, sr, cc, sc = np.cos(ang_r), np.sin(ang_r), np.cos(ang_c), np.sin(ang_c)
    zeros = np.zeros((SEQ, HEAD_DIM), np.float32)
    cs = np.concatenate([cr, cr, cc, cc, zeros], axis=-1)
    sn = np.concatenate([-sr, sr, -sc, sc, zeros], axis=-1)
    ident = np.concatenate([np.ones((T_CTX, HEAD_DIM), np.float32),
                            np.zeros((T_CTX, HEAD_DIM), np.float32)], axis=-1)
    cs = np.concatenate([cs, ident], axis=0).astype(np.float32)
    sn = np.concatenate([sn, np.zeros((T_CTX, LANES), np.float32)], axis=0).astype(np.float32)
    return jnp.asarray(cs), jnp.asarray(sn)


def _pad_out_weight(w_out_l):
    w = w_out_l[:ATT_Q].reshape(N_ATT_HEADS, HEAD_DIM, D_MODEL)
    w = jnp.concatenate([w, jnp.zeros_like(w)], axis=1).reshape(_YA_W, D_MODEL)
    return jnp.concatenate([w, w_out_l[ATT_Q:]], axis=0).astype(BF16)


def _pad_in_weight(w_in_l):
    parts = []
    for _, src, heads, mode in _SECTIONS:
        w = w_in_l[:, src:src + heads * HEAD_DIM]
        if mode != "dense":
            w = w.reshape(D_MODEL, heads, HEAD_DIM)
            zero = jnp.zeros_like(w)
            if mode == "par":
                even = (np.arange(heads) % 2 == 0)[None, :, None]
                halves = [jnp.where(even, w, zero), jnp.where(even, zero, w)]
            else:
                halves = [w, w if mode == "dup" else zero]
            w = jnp.concatenate(halves, axis=-1).reshape(D_MODEL, heads * HB)
        parts.append(w)
    return jnp.concatenate(parts, axis=-1).astype(BF16)


def kernel(x, c, ctx, c_ctx, w_ada, b_ada, norm1_g, norm2_g, w_in, q_norm_g, k_norm_g, na_rpb,
           ret_decay, mix_g, w_out, w_router, b_router, w_exp1, w_exp3, w_exp2, final_g):
    cs_tab, sn_tab = _rope_tables()

    cvec = jnp.concatenate([c, c_ctx[None, :], jnp.zeros((8 - BATCH - 1, D_MODEL), F32)], axis=0)
    mod_all = _ada_mod(cvec, w_ada, b_ada)

    wr_pad = jnp.concatenate([w_router, jnp.zeros((D_MODEL, LANES - N_EXPERTS), F32)], axis=1)
    brt = jnp.broadcast_to(b_router[:, None], (N_EXPERTS, LANES))
    zero_lane = jnp.zeros((HEAD_DIM,), F32)

    x_lat = x.reshape(T_LAT, D_MODEL)
    x_ctx = ctx.reshape(T_CTX, D_MODEL)

    gshape = (DEPTH, N_EXPERT_GROUPS, EXPERTS_PER_GROUP)
    w1g = w_exp1.astype(BF16).reshape(gshape + (D_MODEL, D_EXPERT))
    w3g = w_exp3.astype(BF16).reshape(gshape + (D_MODEL, D_EXPERT))
    w2g = w_exp2.astype(BF16).reshape(gshape + (D_EXPERT, D_MODEL))

    for l in range(DEPTH):
        last = l == DEPTH - 1
        want_ctx = not last
        split_ctx = l == 0
        mod3 = mod_all[l].reshape(8, 1, 6 * D_MODEL)
        w_pad = _pad_in_weight(w_in[l])
        qg = jnp.concatenate([q_norm_g[l], zero_lane])[None, :]
        kg = jnp.concatenate([k_norm_g[l], zero_lane])[None, :]
        aq, ak, av, nq, nk, nv, rq, rk, rv, rg = _inproj(
            x_lat, x_ctx, split_ctx, mod3, norm1_g[l][None, :], w_pad, cs_tab, sn_tab, qg, kg)

        ga = jnp.broadcast_to(mix_g[l][:ATT_Q].reshape(N_ATT_HEADS, HEAD_DIM, 1),
                              (N_ATT_HEADS, HEAD_DIM, LANES))
        ga = jnp.concatenate([ga, jnp.zeros_like(ga)], axis=1).reshape(N_ATT_HEADS * HB, LANES)
        gn = _pair_gains(mix_g[l][ATT_Q:ATT_Q + NA_W], N_NA_HEADS)
        gr = _pair_gains(mix_g[l][ATT_Q + NA_W:], N_RET_HEADS)

        ya = _gqa(aq, ak, av, ga, want_ctx)
        yn_lat, yn_ctx = _neigh(nq, nk, nv, _na_bias_table(na_rpb[l]), gn, want_ctx)
        log_g2 = jax.nn.log_sigmoid(ret_decay[l].astype(F32))
        yr_lat, yr_ctx = _retention(rq, rk, rv, rg, _ret_tables(log_g2), gr, want_ctx)

        wo_pad = _pad_out_weight(w_out[l])
        ntiles = LAT_TILES if last else ALL_TILES
        xn, h2, metat, metac, ctab, tot = _merge(
            ya, yn_lat, yn_ctx, yr_lat, yr_ctx, x_lat, x_ctx, split_ctx, mod3,
            norm2_g[l][None, :], wo_pad, wr_pad, brt, ntiles // (TP // TM))
        plan = _moe_plan(ctab, tot, ntiles)
        ys = _moe_sorted(plan, h2, metat, metac, w1g, w3g, w2g, l)
        x_lat = _combine(plan, ys, metac, xn, mod3, final_g[None, :], ntiles, last)
        x_ctx = x_lat

    return x_lat.reshape(BATCH, SEQ, D_MODEL)
```
